```python
import jax, jax.numpy as jnp
from jax import lax
import numpy as np

D_MODEL = 2048
BATCH = 4
SEQ = 2048
DEPTH = 1
DEC_BATCH = 128
DEC_SEQ = 8
PAST_LEN = 8192
PAGE_SIZE = 128

N_META = 16
HEAD_DIM = 64
ATTN_WIDTH = D_MODEL // 2
CONV_CH = D_MODEL - ATTN_WIDTH
N_HEADS = ATTN_WIDTH // HEAD_DIM
N_KV_HEADS = 4
GQA_GROUP = N_HEADS // N_KV_HEADS
KV_WIDTH = N_KV_HEADS * HEAD_DIM
IN_COLS = ATTN_WIDTH + 2 * KV_WIDTH + 2 * CONV_CH
WINDOW = 128
BLOCK = 128
ROPE_THETA = 500000.0
ROT_DIM = HEAD_DIM // 4
CONV_K = 31
D_FF = 4 * D_MODEL
RMS_EPS = 1e-6
LN_EPS = 1e-5
MASK_VALUE = -1e30

kernel_name = 'hybrid_swa_sink_conformer_conv_decoder'


def rms_norm(x, g):
    xf = x.astype(jnp.float32)
    y = xf * lax.rsqrt(jnp.mean(xf * xf, axis=-1, keepdims=True) + RMS_EPS)
    return (y * g.astype(jnp.float32)).astype(x.dtype)


def partial_rope(x, pos):
    half = ROT_DIM // 2
    inv = jnp.power(jnp.float32(ROPE_THETA), -jnp.arange(half, dtype=jnp.float32) * 2.0 / ROT_DIM)
    ang = pos.astype(jnp.float32)[:, None] * inv[None, :]
    cos = jnp.cos(ang)[:, None, :]
    sin = jnp.sin(ang)[:, None, :]
    xr = x[..., :ROT_DIM].astype(jnp.float32)
    x1, x2 = xr[..., :half], xr[..., half:]
    rot = jnp.concatenate([x1 * cos - x2 * sin, x2 * cos + x1 * sin], axis=-1)
    return jnp.concatenate([rot.astype(x.dtype), x[..., ROT_DIM:]], axis=-1)


def mixer_inputs(h, g, w_in):
    xn = rms_norm(h, g)
    z = jnp.einsum('btd,dc->btc', xn, w_in)
    q, k, v, a, b = jnp.split(z, [ATTN_WIDTH, ATTN_WIDTH + KV_WIDTH, ATTN_WIDTH + 2 * KV_WIDTH,
                                  ATTN_WIDTH + 2 * KV_WIDTH + CONV_CH], axis=-1)
    lead = h.shape[:-1]
    q = q.reshape(lead + (N_HEADS, HEAD_DIM))
    k = k.reshape(lead + (N_KV_HEADS, HEAD_DIM))
    v = v.reshape(lead + (N_KV_HEADS, HEAD_DIM))
    u = a * jax.nn.sigmoid(b)
    return q, k, v, u


def sink_probs(s, mask, sink):
    s = jnp.where(mask, s, MASK_VALUE)
    m = jnp.maximum(jnp.max(s, axis=-1, keepdims=True), sink)
    e = jnp.exp(s - m)
    return e / (jnp.sum(e, axis=-1, keepdims=True) + jnp.exp(sink - m))


def swa_prompt(q, k, v, sinks):
    n = q.shape[0]
    pad = (-N_META) % BLOCK
    padf = lambda t: jnp.pad(t, ((0, 0), (pad, 0), (0, 0), (0, 0)))
    lp = q.shape[1] + pad
    nb = lp // BLOCK
    qb = padf(q).reshape(n, nb, BLOCK, N_KV_HEADS, GQA_GROUP, HEAD_DIM)
    kb = padf(k).reshape(n, nb, BLOCK, N_KV_HEADS, HEAD_DIM)
    vb = padf(v).reshape(n, nb, BLOCK, N_KV_HEADS, HEAD_DIM)
    prev = lambda t: jnp.pad(t, ((0, 0), (1, 0), (0, 0), (0, 0), (0, 0)))[:, :-1]
    kk = jnp.concatenate([prev(kb), kb], axis=2)
    vv = jnp.concatenate([prev(vb), vb], axis=2)
    qpos = (jnp.arange(lp, dtype=jnp.int32) - pad).reshape(nb, BLOCK)
    kpos = jnp.concatenate([qpos - BLOCK, qpos], axis=1)
    diff = qpos[:, :, None] - kpos[:, None, :]
    mask = (diff >= 0) & (diff < WINDOW) & (kpos[:, None, :] >= 0)
    s = jnp.einsum('bnqkgd,bnskd->bnkgqs', qb, kk, preferred_element_type=jnp.float32)
    s = s * (HEAD_DIM ** -0.5)
    sink = sinks.astype(jnp.float32).reshape(N_KV_HEADS, GQA_GROUP)[None, None, :, :, None, None]
    p = sink_probs(s, mask[None, :, None, None], sink)
    o = jnp.einsum('bnkgqs,bnskd->bnqkgd', p.astype(v.dtype), vv)
    return o.reshape(n, lp, ATTN_WIDTH)[:, pad:]


def swa_sample(q, kk, vv, sinks):
    n, t = q.shape[:2]
    w = kk.shape[1] - t
    qpos = PAST_LEN + jnp.arange(t, dtype=jnp.int32)
    kpos = PAST_LEN - w + jnp.arange(w + t, dtype=jnp.int32)
    diff = qpos[:, None] - kpos[None, :]
    mask = (diff >= 0) & (diff < WINDOW)
    qg = q.reshape(n, t, N_KV_HEADS, GQA_GROUP, HEAD_DIM)
    s = jnp.einsum('bqkgd,bskd->bkgqs', qg, kk, preferred_element_type=jnp.float32)
    s = s * (HEAD_DIM ** -0.5)
    sink = sinks.astype(jnp.float32).reshape(N_KV_HEADS, GQA_GROUP)[None, :, :, None, None]
    p = sink_probs(s, mask[None, None, None], sink)
    o = jnp.einsum('bkgqs,bskd->bqkgd', p.astype(vv.dtype), vv)
    return o.reshape(n, t, ATTN_WIDTH)


def conv_branch(u_hist, w_dw, b_dw, ln_g, ln_b):
    y = lax.conv_general_dilated(u_hist, w_dw[:, None, :].astype(u_hist.dtype), window_strides=(1,),
                                 padding='VALID', dimension_numbers=('NWC', 'WIO', 'NWC'),
                                 feature_group_count=CONV_CH)
    yf = (y + b_dw).astype(jnp.float32)
    mu = jnp.mean(yf, axis=-1, keepdims=True)
    var = jnp.mean(jnp.square(yf - mu), axis=-1, keepdims=True)
    yn = (yf - mu) * lax.rsqrt(var + LN_EPS) * ln_g.astype(jnp.float32) + ln_b.astype(jnp.float32)
    return jax.nn.silu(yn).astype(u_hist.dtype)


def merge_groups(attn, conv, w_out):
    return jnp.einsum('btc,cd->btd', jnp.concatenate([attn, conv], axis=-1), w_out)


def sq_relu_mlp(h, g, w_up, w_down):
    a = jax.nn.relu(jnp.einsum('btd,df->btf', rms_norm(h, g), w_up))
    return jnp.einsum('btf,fd->btd', a * a, w_down)


def setup_inputs(seed: int = 0) -> dict:
    key = jax.random.key(seed)
    ks = jax.random.split(key, 20)
    f32 = jnp.float32
    nrm = lambda k, shape, scale: jax.random.normal(k, shape, f32) * scale
    return {
        'x_prompt': nrm(ks[0], (BATCH, SEQ, D_MODEL), 1.0),
        'x_sample': nrm(ks[1], (DEC_BATCH, DEC_SEQ, D_MODEL), 1.0),
        'cache_k': nrm(ks[2], (DEPTH, DEC_BATCH, WINDOW, N_KV_HEADS, HEAD_DIM), 1.0),
        'cache_v': nrm(ks[3], (DEPTH, DEC_BATCH, WINDOW, N_KV_HEADS, HEAD_DIM), 1.0),
        'state_conv': nrm(ks[4], (DEPTH, DEC_BATCH, CONV_K - 1, CONV_CH), 0.5),
        'meta_tokens': nrm(ks[5], (N_META, D_MODEL), 1.0),
        'norm_mix': 1.0 + nrm(ks[6], (DEPTH, D_MODEL), 0.02),
        'w_in': nrm(ks[7], (DEPTH, D_MODEL, IN_COLS), D_MODEL ** -0.5),
        'attn_sinks': nrm(ks[8], (DEPTH, N_HEADS), 0.5),
        'w_dw': nrm(ks[9], (DEPTH, CONV_K, CONV_CH), CONV_K ** -0.5),
        'b_dw': nrm(ks[10], (DEPTH, CONV_CH), 0.02),
        'conv_ln_g': 1.0 + nrm(ks[11], (DEPTH, CONV_CH), 0.02),
        'conv_ln_b': nrm(ks[12], (DEPTH, CONV_CH), 0.02),
        'w_out': nrm(ks[13], (DEPTH, ATTN_WIDTH + CONV_CH, D_MODEL), (ATTN_WIDTH + CONV_CH) ** -0.5),
        'norm_mlp': 1.0 + nrm(ks[14], (DEPTH, D_MODEL), 0.02),
        'w_up': nrm(ks[15], (DEPTH, D_MODEL, D_FF), D_MODEL ** -0.5),
        'w_down': nrm(ks[16], (DEPTH, D_FF, D_MODEL), D_FF ** -0.5),
        'norm_final': 1.0 + nrm(ks[17], (D_MODEL,), 0.02),
    }


def reference(x_prompt, x_sample, cache_k, cache_v, state_conv, meta_tokens, norm_mix, w_in, attn_sinks,
              w_dw, b_dw, conv_ln_g, conv_ln_b, w_out, norm_mlp, w_up, w_down, norm_final):
    n_prompt = x_prompt.shape[0]
    meta = jnp.broadcast_to(meta_tokens.astype(x_prompt.dtype)[None], (n_prompt, N_META, D_MODEL))
    hp = jnp.concatenate([meta, x_prompt], axis=1)
    hs = x_sample
    pos_p = jnp.arange(hp.shape[1], dtype=jnp.int32)
    pos_s = PAST_LEN + jnp.arange(hs.shape[1], dtype=jnp.int32)
    buf_len = cache_k.shape[2]
    kp, vp, cp, ksl, vsl, csl = [], [], [], [], [], []
    for l in range(DEPTH):
        q, k, v, u = mixer_inputs(hp, norm_mix[l], w_in[l])
        q = partial_rope(q, pos_p)
        k = partial_rope(k, pos_p)
        attn = swa_prompt(q, k, v, attn_sinks[l])
        u_hist = jnp.pad(u, ((0, 0), (CONV_K - 1, 0), (0, 0)))
        conv = conv_branch(u_hist, w_dw[l], b_dw[l], conv_ln_g[l], conv_ln_b[l])
        hp = hp + merge_groups(attn, conv, w_out[l])
        hp = hp + sq_relu_mlp(hp, norm_mlp[l], w_up[l], w_down[l])
        kp.append(k[:, -WINDOW:])
        vp.append(v[:, -WINDOW:])
        cp.append(u_hist[:, -(CONV_K - 1):])
        q, k, v, u = mixer_inputs(hs, norm_mix[l], w_in[l])
        q = partial_rope(q, pos_s)
        k = partial_rope(k, pos_s)
        kk = jnp.concatenate([cache_k[l].astype(k.dtype), k], axis=1)
        vv = jnp.concatenate([cache_v[l].astype(v.dtype), v], axis=1)
        attn = swa_sample(q, kk, vv, attn_sinks[l])
        u_hist = jnp.concatenate([state_conv[l].astype(u.dtype), u], axis=1)
        conv = conv_branch(u_hist, w_dw[l], b_dw[l], conv_ln_g[l], conv_ln_b[l])
        hs = hs + merge_groups(attn, conv, w_out[l])
        hs = hs + sq_relu_mlp(hs, norm_mlp[l], w_up[l], w_down[l])
        ksl.append(kk[:, -buf_len:])
        vsl.append(vv[:, -buf_len:])
        csl.append(u_hist[:, -(CONV_K - 1):])
    y_prompt = rms_norm(hp, norm_final)[:, N_META:]
    y_sample = rms_norm(hs, norm_final)
    return (y_prompt, y_sample, jnp.stack(kp), jnp.stack(vp), jnp.stack(cp),
            jnp.stack(ksl), jnp.stack(vsl), jnp.stack(csl))
```

```python
import functools

import jax
import jax.numpy as jnp
from jax import lax
from jax.experimental import pallas as pl
from jax.experimental.pallas import tpu as pltpu

D_MODEL = 2048
N_META = 16
HEAD_DIM = 64
ATTN_WIDTH = D_MODEL // 2
CONV_CH = D_MODEL - ATTN_WIDTH
N_HEADS = ATTN_WIDTH // HEAD_DIM
N_KV_HEADS = 4
GQA_GROUP = N_HEADS // N_KV_HEADS
KV_WIDTH = N_KV_HEADS * HEAD_DIM
IN_COLS = ATTN_WIDTH + 2 * KV_WIDTH + 2 * CONV_CH
WINDOW = 128
BLOCK = 128
ROPE_THETA = 500000.0
ROT_DIM = HEAD_DIM // 4
CONV_K = 31
D_FF = 4 * D_MODEL
RMS_EPS = 1e-6
LN_EPS = 1e-5
MASK_VALUE = -1e30
PAST_LEN = 8192

LANES = 128
SUBLANES = 8
VMEM_LIMIT = 56 * 1024 * 1024

F32 = jnp.float32
BF16 = jnp.bfloat16

K_OFF = ATTN_WIDTH
V_OFF = ATTN_WIDTH + KV_WIDTH
A_OFF = ATTN_WIDTH + 2 * KV_WIDTH
B_OFF = A_OFF + CONV_CH


def _params(*sem):
    return pltpu.CompilerParams(dimension_semantics=sem, vmem_limit_bytes=VMEM_LIMIT)


def _const_spec(shape):
    nd = len(shape)
    return pl.BlockSpec(shape, lambda *_: (0,) * nd, pipeline_mode=pl.Buffered(1))


def _inproj_kernel(x_ref, g_ref, w_ref, rope_ref, q_ref, k_ref, v_ref, u_ref, xn_ref):
    x = x_ref[...]
    ms = jnp.mean(x * x, axis=-1, keepdims=True)
    xn_ref[...] = ((x * lax.rsqrt(ms + RMS_EPS)) * g_ref[...]).astype(BF16)

    cos = rope_ref[0]
    sin_lo = rope_ref[1]
    sin_hi = rope_ref[2]

    def rope(z):
        return (z * cos + pltpu.roll(z, LANES - ROT_DIM // 2, 1) * sin_lo
                + pltpu.roll(z, ROT_DIM // 2, 1) * sin_hi)

    def proj(off, width):
        return jnp.dot(xn_ref[...], w_ref[:, off:off + width], preferred_element_type=F32)

    half = ATTN_WIDTH // 2
    for c in range(2):
        acc = proj(c * half, half)
        for cc in range(half // LANES):
            z = rope(acc[:, cc * LANES:(cc + 1) * LANES]) * (HEAD_DIM ** -0.5)
            q_ref[:, c * half + cc * LANES:c * half + (cc + 1) * LANES] = z.astype(q_ref.dtype)
    acc = proj(K_OFF, 2 * KV_WIDTH)
    for cc in range(KV_WIDTH // LANES):
        k_ref[:, cc * LANES:(cc + 1) * LANES] = rope(acc[:, cc * LANES:(cc + 1) * LANES])
    v_ref[...] = acc[:, KV_WIDTH:]
    for c in range(2):
        a = proj(A_OFF + c * half, half)
        b = proj(B_OFF + c * half, half)
        u_ref[:, c * half:(c + 1) * half] = a * jax.nn.sigmoid(b)


def _inproj(x, g, w_bf, rope_tab, tm, q_dtype):
    m = x.shape[0]
    nrt = rope_tab.shape[1] // tm
    return pl.pallas_call(
        _inproj_kernel,
        grid=(m // tm,),
        in_specs=[
            pl.BlockSpec((tm, D_MODEL), lambda i: (i, 0)),
            _const_spec((1, D_MODEL)),
            _const_spec((D_MODEL, IN_COLS)),
            pl.BlockSpec((3, tm, LANES), lambda i: (0, i % nrt, 0)),
        ],
        out_specs=[
            pl.BlockSpec((tm, ATTN_WIDTH), lambda i: (i, 0)),
            pl.BlockSpec((tm, KV_WIDTH), lambda i: (i, 0)),
            pl.BlockSpec((tm, KV_WIDTH), lambda i: (i, 0)),
            pl.BlockSpec((tm, CONV_CH), lambda i: (i, 0)),
        ],
        out_shape=[
            jax.ShapeDtypeStruct((m, ATTN_WIDTH), q_dtype),
            jax.ShapeDtypeStruct((m, KV_WIDTH), F32),
            jax.ShapeDtypeStruct((m, KV_WIDTH), F32),
            jax.ShapeDtypeStruct((m, CONV_CH), F32),
        ],
        scratch_shapes=[pltpu.VMEM((tm, D_MODEL), BF16)],
        compiler_params=_params("parallel"),
        name="inproj",
    )(x, g, w_bf, rope_tab)


def _rope_table(pos):
    half = ROT_DIM // 2
    inv = jnp.power(jnp.float32(ROPE_THETA), -jnp.arange(half, dtype=F32) * 2.0 / ROT_DIM)
    ang = pos.astype(F32)[:, None] * inv[None, :]
    cos, sin = jnp.cos(ang), jnp.sin(ang)
    t = pos.shape[0]
    rest = HEAD_DIM - ROT_DIM
    one = jnp.ones((t, rest), F32)
    zero = jnp.zeros((t, rest), F32)
    zh = jnp.zeros((t, half), F32)
    c = jnp.concatenate([cos, cos, one], axis=1)
    s_lo = jnp.concatenate([-sin, zh, zero], axis=1)
    s_hi = jnp.concatenate([zh, sin, zero], axis=1)
    tab = jnp.stack([c, s_lo, s_hi])
    return jnp.tile(tab, (1, 1, LANES // HEAD_DIM))


def _attn_prompt_kernel(sink_ref, q_ref, kc_ref, kp_ref, vc_ref, vp_ref, km_ref, vm_ref, o_ref,
                        kk_ref, vv_ref):
    n = pl.program_id(1)

    @pl.when(n == 0)
    def _():
        kk_ref[0:BLOCK] = km_ref[...]
        vv_ref[0:BLOCK] = vm_ref[...]

    @pl.when(n > 0)
    def _():
        kk_ref[0:BLOCK] = kp_ref[...]
        vv_ref[0:BLOCK] = vp_ref[...]

    kk_ref[BLOCK:2 * BLOCK] = kc_ref[...]
    vv_ref[BLOCK:2 * BLOCK] = vc_ref[...]

    lo = jnp.where(n == 0, BLOCK - N_META, 0)
    r = lax.broadcasted_iota(jnp.int32, (BLOCK, 2 * BLOCK), 0)
    j = lax.broadcasted_iota(jnp.int32, (BLOCK, 2 * BLOCK), 1)
    mask = (j > r) & (j <= r + BLOCK) & (j >= lo)
    lane = lax.broadcasted_iota(jnp.int32, (BLOCK, LANES), 1)
    low_half = lane < HEAD_DIM
    lane2 = lax.broadcasted_iota(jnp.int32, (2 * BLOCK, LANES), 1)
    low_half2 = lane2 < HEAD_DIM

    for c in range(KV_WIDTH // LANES):
        kf = kk_ref[:, c * LANES:(c + 1) * LANES]
        vf = vv_ref[:, c * LANES:(c + 1) * LANES]
        k_sw = pltpu.roll(kf, HEAD_DIM, 1)
        v_sw = pltpu.roll(vf, HEAD_DIM, 1)
        for hh in range(2):
            h = 2 * c + hh
            if hh == 0:
                k2 = jnp.where(low_half2, kf, k_sw).astype(BF16)
                v2 = jnp.where(low_half2, vf, v_sw).astype(BF16)
            else:
                k2 = jnp.where(low_half2, k_sw, kf).astype(BF16)
                v2 = jnp.where(low_half2, v_sw, vf).astype(BF16)
            outs = []
            for pair in range(2):
                q2 = q_ref[:, (2 * h + pair) * LANES:(2 * h + pair + 1) * LANES]
                zero = jnp.zeros_like(q2)
                o_pair = []
                for side in range(2):
                    head = 4 * h + 2 * pair + side
                    qm = jnp.where(low_half, q2, zero) if side == 0 else jnp.where(low_half, zero, q2)
                    s = lax.dot_general(qm, k2, (((1,), (1,)), ((), ())), preferred_element_type=F32)
                    s = jnp.where(mask, s, MASK_VALUE)
                    sink = sink_ref[head]
                    mx = jnp.maximum(jnp.max(s, axis=-1, keepdims=True), sink)
                    e = jnp.exp(s - mx)
                    den = jnp.sum(e, axis=-1, keepdims=True) + jnp.exp(sink - mx)
                    o = jnp.dot(e.astype(BF16), v2, preferred_element_type=F32)
                    o_pair.append(o / den)
                outs.append(jnp.where(low_half, o_pair[0], o_pair[1]))
            for pair in range(2):
                col = (2 * h + pair) * LANES
                o_ref[:, col:col + LANES] = outs[pair].astype(o_ref.dtype)


def _attn_prompt(sinks, q, k, v, k_meta_blk, v_meta_blk, n_batch, n_blk):
    kv_cur = pl.BlockSpec((BLOCK, KV_WIDTH), lambda b, n: (b * n_blk + n, 0))
    kv_prev = pl.BlockSpec((BLOCK, KV_WIDTH), lambda b, n: (b * n_blk + jnp.maximum(n - 1, 0), 0))
    return pl.pallas_call(
        _attn_prompt_kernel,
        grid=(n_batch, n_blk),
        in_specs=[
            pl.BlockSpec(memory_space=pltpu.SMEM),
            pl.BlockSpec((BLOCK, ATTN_WIDTH), lambda b, n: (b * n_blk + n, 0)),
            kv_cur, kv_prev, kv_cur, kv_prev,
            _const_spec((BLOCK, KV_WIDTH)),
            _const_spec((BLOCK, KV_WIDTH)),
        ],
        out_specs=pl.BlockSpec((BLOCK, ATTN_WIDTH), lambda b, n: (b * n_blk + n, 0)),
        out_shape=jax.ShapeDtypeStruct((n_batch * n_blk * BLOCK, ATTN_WIDTH), BF16),
        scratch_shapes=[pltpu.VMEM((2 * BLOCK, KV_WIDTH), F32), pltpu.VMEM((2 * BLOCK, KV_WIDTH), F32)],
        compiler_params=_params("parallel", "arbitrary"),
        name="attn_prompt",
    )(sinks, q, k, k, v, v, k_meta_blk, v_meta_blk)


def _attn_sample_kernel(t_new, bt, sink_ref, q_ref, kn_ref, vn_ref, ck_ref, cv_ref,
                        o_ref, ko_ref, vo_ref, qrow_ref, kk_ref, vv_ref):
    nrow = N_HEADS * t_new
    w = ck_ref.shape[1]
    kk_ref[...] = jnp.zeros_like(kk_ref)
    vv_ref[...] = jnp.zeros_like(vv_ref)
    qrow_ref[...] = jnp.zeros_like(qrow_ref)

    r = lax.broadcasted_iota(jnp.int32, (nrow, 2 * w), 0) % t_new
    j = lax.broadcasted_iota(jnp.int32, (nrow, 2 * w), 1)
    mask = (j > r) & (j <= r + w)
    lane = lax.broadcasted_iota(jnp.int32, (t_new, LANES), 1)
    low_half = lane < HEAD_DIM
    sink = sink_ref[...][:, 0:1]

    def body(i, carry):
        row0 = pl.multiple_of(i * t_new, t_new)
        ck = ck_ref[i]
        cv = cv_ref[i]
        kn = kn_ref[pl.ds(row0, t_new), :]
        vn = vn_ref[pl.ds(row0, t_new), :]
        kk_ref[0:w] = ck
        vv_ref[0:w] = cv
        kk_ref[w:w + t_new] = kn
        vv_ref[w:w + t_new] = vn
        ko_ref[i, 0:w - t_new] = ck[t_new:]
        ko_ref[i, w - t_new:w] = kn
        vo_ref[i, 0:w - t_new] = cv[t_new:]
        vo_ref[i, w - t_new:w] = vn

        for t in range(N_HEADS):
            h = t // GQA_GROUP
            qv = q_ref[pl.ds(row0, t_new), (t // 2) * LANES:(t // 2 + 1) * LANES]
            if t % 2 != h % 2:
                qv = pltpu.roll(qv, HEAD_DIM, 1)
            keep = low_half if h % 2 == 0 else jnp.logical_not(low_half)
            qrow_ref[t * t_new:(t + 1) * t_new, (h // 2) * LANES:(h // 2 + 1) * LANES] = (
                jnp.where(keep, qv, 0.0))

        s = lax.dot_general(qrow_ref[...].astype(BF16), kk_ref[...].astype(BF16),
                            (((1,), (1,)), ((), ())), preferred_element_type=F32)
        s = jnp.where(mask, s, MASK_VALUE)
        mx = jnp.maximum(jnp.max(s, axis=-1, keepdims=True), sink)
        e = jnp.exp(s - mx)
        den = jnp.sum(e, axis=-1, keepdims=True) + jnp.exp(sink - mx)
        o_all = jnp.dot(e.astype(BF16), vv_ref[...].astype(BF16), preferred_element_type=F32) / den

        for t2 in range(N_HEADS // 2):
            parts = []
            for side in range(2):
                t = 2 * t2 + side
                h = t // GQA_GROUP
                blk = o_all[t * t_new:(t + 1) * t_new, (h // 2) * LANES:(h // 2 + 1) * LANES]
                if t % 2 != h % 2:
                    blk = pltpu.roll(blk, HEAD_DIM, 1)
                parts.append(blk)
            o_ref[pl.ds(row0, t_new), t2 * LANES:(t2 + 1) * LANES] = jnp.where(low_half, parts[0], parts[1])
        return carry

    lax.fori_loop(0, bt, body, 0)


def _attn_sample(sink_rows, q, k_new, v_new, cache_k, cache_v, t_new, bt):
    nb, w, _ = cache_k.shape
    rows = bt * t_new
    row_spec = lambda width: pl.BlockSpec((rows, width), lambda s: (s, 0))
    cache_spec = pl.BlockSpec((bt, w, KV_WIDTH), lambda s: (s, 0, 0))
    return pl.pallas_call(
        functools.partial(_attn_sample_kernel, t_new, bt),
        grid=(nb // bt,),
        in_specs=[
            _const_spec(sink_rows.shape),
            row_spec(ATTN_WIDTH), row_spec(KV_WIDTH), row_spec(KV_WIDTH),
            cache_spec, cache_spec,
        ],
        out_specs=[row_spec(ATTN_WIDTH), cache_spec, cache_spec],
        out_shape=[
            jax.ShapeDtypeStruct((nb * t_new, ATTN_WIDTH), F32),
            jax.ShapeDtypeStruct(cache_k.shape, F32),
            jax.ShapeDtypeStruct(cache_v.shape, F32),
        ],
        scratch_shapes=[
            pltpu.VMEM((N_HEADS * t_new, KV_WIDTH), F32),
            pltpu.VMEM((2 * w, KV_WIDTH), F32),
            pltpu.VMEM((2 * w, KV_WIDTH), F32),
        ],
        compiler_params=_params("parallel"),
        name="attn_sample",
    )(sink_rows, q, k_new, v_new, cache_k, cache_v)


def _ln_swish(y, g, b):
    mu = jnp.mean(y, axis=-1, keepdims=True)
    d = y - mu
    var = jnp.mean(d * d, axis=-1, keepdims=True)
    yn = d * lax.rsqrt(var + LN_EPS) * g + b
    return yn * jax.nn.sigmoid(yn)


CONV_HALO = 32
CONV_ROWS = 32
CONV_COLS = 256


def _conv_prompt_kernel(tm, u_ref, halo_ref, meta_ref, w_ref, bdw_ref, g_ref, b_ref, o_ref, win_ref, y_ref):
    t = pl.program_id(1)

    @pl.when(t == 0)
    def _():
        win_ref[0:CONV_HALO] = meta_ref[...]

    @pl.when(t > 0)
    def _():
        win_ref[0:CONV_HALO] = halo_ref[...]

    win_ref[CONV_HALO:CONV_HALO + tm] = u_ref[...]
    first = CONV_HALO - (CONV_K - 1)

    for rc in range(tm // CONV_ROWS):
        r0 = rc * CONV_ROWS
        for cc in range(CONV_CH // CONV_COLS):
            cols = slice(cc * CONV_COLS, (cc + 1) * CONV_COLS)
            acc = jnp.zeros((CONV_ROWS, CONV_COLS), F32)
            for tap in range(CONV_K):
                row = r0 + first + tap
                acc = acc + win_ref[row:row + CONV_ROWS, cols] * w_ref[tap:tap + 1, cols]
            y_ref[r0:r0 + CONV_ROWS, cols] = acc
    o_ref[...] = _ln_swish(y_ref[...] + bdw_ref[...], g_ref[...], b_ref[...]).astype(o_ref.dtype)


def _conv_prompt(u, meta_halo, w_dw, b_dw, ln_g, ln_b, n_batch, seq, tm):
    nt = seq // tm
    per = tm // CONV_HALO
    return pl.pallas_call(
        functools.partial(_conv_prompt_kernel, tm),
        grid=(n_batch, nt),
        in_specs=[
            pl.BlockSpec((tm, CONV_CH), lambda b, t: (b * nt + t, 0)),
            pl.BlockSpec((CONV_HALO, CONV_CH), lambda b, t: (jnp.maximum((b * nt + t) * per - 1, 0), 0)),
            _const_spec((CONV_HALO, CONV_CH)),
            _const_spec((CONV_K, CONV_CH)),
            _const_spec((1, CONV_CH)),
            _const_spec((1, CONV_CH)),
            _const_spec((1, CONV_CH)),
        ],
        out_specs=pl.BlockSpec((tm, CONV_CH), lambda b, t: (b * nt + t, 0)),
        out_shape=jax.ShapeDtypeStruct((n_batch * seq, CONV_CH), BF16),
        scratch_shapes=[pltpu.VMEM((CONV_HALO + tm, CONV_CH), F32), pltpu.VMEM((tm, CONV_CH), F32)],
        compiler_params=_params("parallel", "arbitrary"),
        name="conv_prompt",
    )(u, u, meta_halo, w_dw, b_dw, ln_g, ln_b)


def _conv_sample_kernel(t_new, bt, st_ref, u_ref, w_ref, bdw_ref, g_ref, b_ref, o_ref, so_ref, hist_ref):
    hist = CONV_K - 1

    def body(i, carry):
        row0 = pl.multiple_of(i * t_new, t_new)
        un = u_ref[pl.ds(row0, t_new), :]
        hist_ref[0:hist] = st_ref[i]
        hist_ref[hist:hist + t_new] = un
        so_ref[i] = hist_ref[t_new:t_new + hist]
        acc = jnp.zeros((t_new, CONV_CH), F32)
        for tap in range(CONV_K):
            acc = acc + hist_ref[tap:tap + t_new, :] * w_ref[tap:tap + 1, :]
        o_ref[pl.ds(row0, t_new), :] = _ln_swish(acc + bdw_ref[...], g_ref[...], b_ref[...])
        return carry

    lax.fori_loop(0, bt, body, 0)


def _conv_sample(state, u_new, w_dw, b_dw, ln_g, ln_b, t_new, bt):
    nb, hist, _ = state.shape
    rows = bt * t_new
    return pl.pallas_call(
        functools.partial(_conv_sample_kernel, t_new, bt),
        grid=(nb // bt,),
        in_specs=[
            pl.BlockSpec((bt, hist, CONV_CH), lambda s: (s, 0, 0)),
            pl.BlockSpec((rows, CONV_CH), lambda s: (s, 0)),
            _const_spec((CONV_K, CONV_CH)),
            _const_spec((1, CONV_CH)),
            _const_spec((1, CONV_CH)),
            _const_spec((1, CONV_CH)),
        ],
        out_specs=[
            pl.BlockSpec((rows, CONV_CH), lambda s: (s, 0)),
            pl.BlockSpec((bt, hist, CONV_CH), lambda s: (s, 0, 0)),
        ],
        out_shape=[
            jax.ShapeDtypeStruct((nb * t_new, CONV_CH), F32),
            jax.ShapeDtypeStruct(state.shape, F32),
        ],
        scratch_shapes=[pltpu.VMEM((hist + t_new + 2, CONV_CH), F32)],
        compiler_params=_params("parallel"),
        name="conv_sample",
    )(state, u_new, w_dw, b_dw, ln_g, ln_b)


def _rms(x, g):
    ms = jnp.mean(x * x, axis=-1, keepdims=True)
    return (x * lax.rsqrt(ms + RMS_EPS)) * g


def _mlp_kernel(x_ref, a_ref, c_ref, wo_ref, g2_ref, wu_ref, wd_ref, gf_ref, y_ref, acc_ref, xn_ref):
    f = pl.program_id(1)

    @pl.when(f == 0)
    def _():
        mix = jnp.dot(a_ref[...].astype(BF16), wo_ref[0:ATTN_WIDTH], preferred_element_type=F32)
        mix = mix + jnp.dot(c_ref[...].astype(BF16), wo_ref[ATTN_WIDTH:], preferred_element_type=F32)
        h1 = x_ref[...] + mix
        acc_ref[...] = h1
        xn_ref[...] = _rms(h1, g2_ref[...]).astype(BF16)

    up = jnp.dot(xn_ref[...], wu_ref[...], preferred_element_type=F32)
    act = jnp.maximum(up, 0.0)
    act = (act * act).astype(BF16)
    acc_ref[...] += jnp.dot(act, wd_ref[...], preferred_element_type=F32)

    @pl.when(f == pl.num_programs(1) - 1)
    def _():
        y_ref[...] = _rms(acc_ref[...], gf_ref[...])


def _outproj_mlp(x, attn, conv, wo_bf, g2, wu_bf, wd_bf, gf, tm, tf):
    m = x.shape[0]
    return pl.pallas_call(
        _mlp_kernel,
        grid=(m // tm, D_FF // tf),
        in_specs=[
            pl.BlockSpec((tm, D_MODEL), lambda i, f: (i, 0)),
            pl.BlockSpec((tm, ATTN_WIDTH), lambda i, f: (i, 0)),
            pl.BlockSpec((tm, CONV_CH), lambda i, f: (i, 0)),
            _const_spec((D_MODEL, D_MODEL)),
            _const_spec((1, D_MODEL)),
            pl.BlockSpec((D_MODEL, tf), lambda i, f: (0, f)),
            pl.BlockSpec((tf, D_MODEL), lambda i, f: (f, 0)),
            _const_spec((1, D_MODEL)),
        ],
        out_specs=pl.BlockSpec((tm, D_MODEL), lambda i, f: (i, 0)),
        out_shape=jax.ShapeDtypeStruct((m, D_MODEL), F32),
        scratch_shapes=[pltpu.VMEM((tm, D_MODEL), F32), pltpu.VMEM((tm, D_MODEL), BF16)],
        compiler_params=_params("parallel", "arbitrary"),
        name="outproj_mlp",
    )(x, attn, conv, wo_bf, g2, wu_bf, wd_bf, gf)


def kernel(x_prompt, x_sample, cache_k, cache_v, state_conv, meta_tokens, norm_mix, w_in, attn_sinks,
           w_dw, b_dw, conv_ln_g, conv_ln_b, w_out, norm_mlp, w_up, w_down, norm_final):
    n_batch, seq, _ = x_prompt.shape
    n_dec, t_new, _ = x_sample.shape
    depth = w_in.shape[0]
    assert depth == 1 and seq % BLOCK == 0 and t_new == SUBLANES
    l = 0
    n_blk = seq // BLOCK

    w_in_bf = w_in[l].astype(BF16)
    w_out_bf = w_out[l].astype(BF16)
    w_up_bf = w_up[l].astype(BF16)
    w_down_bf = w_down[l].astype(BF16)
    g_mix = norm_mix[l][None, :]
    g_mlp = norm_mlp[l][None, :]
    g_fin = norm_final[None, :]
    sinks = attn_sinks[l].astype(F32)

    tm = 512
    tm_s = 512
    rope_p = _rope_table(N_META + jnp.arange(seq, dtype=jnp.int32))
    rope_s = _rope_table(PAST_LEN + (jnp.arange(tm_s, dtype=jnp.int32) % t_new))
    rope_m = _rope_table(jnp.arange(N_META, dtype=jnp.int32))

    xp = x_prompt.reshape(n_batch * seq, D_MODEL)
    xs = x_sample.reshape(n_dec * t_new, D_MODEL)

    q_p, k_p, v_p, u_p = _inproj(xp, g_mix, w_in_bf, rope_p, tm, BF16)
    q_s, k_s, v_s, u_s = _inproj(xs, g_mix, w_in_bf, rope_s, tm_s, F32)
    _, k_m, v_m, u_m = _inproj(meta_tokens.astype(F32), g_mix, w_in_bf, rope_m, N_META, F32)

    pad_rows = jnp.zeros((BLOCK - N_META, KV_WIDTH), F32)
    k_meta_blk = jnp.concatenate([pad_rows, k_m], axis=0)
    v_meta_blk = jnp.concatenate([pad_rows, v_m], axis=0)
    attn_p = _attn_prompt(sinks, q_p, k_p, v_p, k_meta_blk, v_meta_blk, n_batch, n_blk)
    meta_halo = jnp.concatenate([jnp.zeros((CONV_HALO - N_META, CONV_CH), F32), u_m], axis=0)
    conv_p = _conv_prompt(u_p, meta_halo, w_dw[l], b_dw[l][None, :], conv_ln_g[l][None, :],
                          conv_ln_b[l][None, :], n_batch, seq, 256)

    window = cache_k.shape[2]
    ck = cache_k[l].reshape(n_dec, window, KV_WIDTH)
    cv = cache_v[l].reshape(n_dec, window, KV_WIDTH)
    sink_rows = jnp.broadcast_to(jnp.repeat(sinks, t_new)[:, None], (N_HEADS * t_new, LANES))
    attn_s, k_new_s, v_new_s = _attn_sample(sink_rows, q_s, k_s, v_s, ck, cv, t_new, 8)
    conv_s, conv_state_s = _conv_sample(state_conv[l], u_s, w_dw[l], b_dw[l][None, :],
                                        conv_ln_g[l][None, :], conv_ln_b[l][None, :], t_new, 16)

    y_p = _outproj_mlp(xp, attn_p, conv_p, w_out_bf, g_mlp, w_up_bf, w_down_bf, g_fin, 512, 512)
    y_s = _outproj_mlp(xs, attn_s, conv_s, w_out_bf, g_mlp, w_up_bf, w_down_bf, g_fin, 512, 512)

    y_prompt = y_p.reshape(n_batch, seq, D_MODEL)
    y_sample = y_s.reshape(n_dec, t_new, D_MODEL)
    k3 = k_p.reshape(n_batch, seq, N_KV_HEADS, HEAD_DIM)
    v3 = v_p.reshape(n_batch, seq, N_KV_HEADS, HEAD_DIM)
    new_k_prompt = k3[:, seq - WINDOW:][None]
    new_v_prompt = v3[:, seq - WINDOW:][None]
    new_conv_prompt = u_p.reshape(n_batch, seq, CONV_CH)[:, seq - (CONV_K - 1):][None]
    new_k_sample = k_new_s.reshape(n_dec, window, N_KV_HEADS, HEAD_DIM)[None]
    new_v_sample = v_new_s.reshape(n_dec, window, N_KV_HEADS, HEAD_DIM)[None]
    new_conv_sample = conv_state_s[None]
    return (y_prompt, y_sample, new_k_prompt, new_v_prompt, new_conv_prompt,
            new_k_sample, new_v_sample, new_conv_sample)
```

```python
import functools

import jax
import jax.numpy as jnp
from jax import lax
from jax.experimental import pallas as pl
from jax.experimental.pallas import tpu as pltpu

D_MODEL = 2048
N_META = 16
HEAD_DIM = 64
ATTN_WIDTH = D_MODEL // 2
CONV_CH = D_MODEL - ATTN_WIDTH
N_HEADS = ATTN_WIDTH // HEAD_DIM
N_KV_HEADS = 4
GQA_GROUP = N_HEADS // N_KV_HEADS
KV_WIDTH = N_KV_HEADS * HEAD_DIM
IN_COLS = ATTN_WIDTH + 2 * KV_WIDTH + 2 * CONV_CH
WINDOW = 128
BLOCK = 128
ROPE_THETA = 500000.0
ROT_DIM = HEAD_DIM // 4
CONV_K = 31
D_FF = 4 * D_MODEL
RMS_EPS = 1e-6
LN_EPS = 1e-5
MASK_VALUE = -1e30
PAST_LEN = 8192

LANES = 128
SUBLANES = 8
VMEM_LIMIT = 56 * 1024 * 1024
N_SLAB = CONV_CH // LANES

F32 = jnp.float32
BF16 = jnp.bfloat16

K_OFF = ATTN_WIDTH
V_OFF = ATTN_WIDTH + KV_WIDTH
A_OFF = ATTN_WIDTH + 2 * KV_WIDTH
B_OFF = A_OFF + CONV_CH
NT_DIMS = (((1,), (1,)), ((), ()))


def _params(*sem):
    return pltpu.CompilerParams(dimension_semantics=sem, vmem_limit_bytes=VMEM_LIMIT)


def _const_spec(shape):
    nd = len(shape)
    return pl.BlockSpec(shape, lambda *_: (0,) * nd, pipeline_mode=pl.Buffered(1))


def _rms(x, g):
    ms = jnp.mean(x * x, axis=-1, keepdims=True)
    return (x * lax.rsqrt(ms + RMS_EPS)) * g


def _rope_rows(z, rope_ref):
    return (z * rope_ref[0] + pltpu.roll(z, LANES - ROT_DIM // 2, 1) * rope_ref[1]
            + pltpu.roll(z, ROT_DIM // 2, 1) * rope_ref[2])


def _proj(xn_ref, w_ref, off, width):
    return jnp.dot(xn_ref[...], w_ref[:, off:off + width], preferred_element_type=F32)


def _write_q(xn_ref, w_ref, rope_ref, q_ref):
    half = ATTN_WIDTH // 2
    for c in range(2):
        acc = _proj(xn_ref, w_ref, c * half, half)
        for cc in range(half // LANES):
            z = _rope_rows(acc[:, cc * LANES:(cc + 1) * LANES], rope_ref) * (HEAD_DIM ** -0.5)
            q_ref[:, c * half + cc * LANES:c * half + (cc + 1) * LANES] = z.astype(q_ref.dtype)


def _glu(xn_ref, w_ref, c):
    half = CONV_CH // 2
    a = _proj(xn_ref, w_ref, A_OFF + c * half, half)
    b = _proj(xn_ref, w_ref, B_OFF + c * half, half)
    return a * jax.nn.sigmoid(b)


def _inproj_prompt_kernel(x_ref, g_ref, w_ref, rope_ref, q_ref, k_ref, v_ref, u_ref, xn_ref):
    xn_ref[...] = _rms(x_ref[...], g_ref[...]).astype(BF16)
    _write_q(xn_ref, w_ref, rope_ref, q_ref)
    acc = _proj(xn_ref, w_ref, K_OFF, 2 * KV_WIDTH)
    for cc in range(KV_WIDTH // LANES):
        k_ref[:, cc * LANES:(cc + 1) * LANES] = _rope_rows(acc[:, cc * LANES:(cc + 1) * LANES], rope_ref)
    v_ref[...] = acc[:, KV_WIDTH:]
    per = N_SLAB // 2
    for c in range(2):
        u = _glu(xn_ref, w_ref, c)
        for cc in range(per):
            u_ref[c * per + cc] = u[:, cc * LANES:(cc + 1) * LANES]


def _inproj_prompt(x, g, w_bf, rope_tab, tm):
    m = x.shape[0]
    nrt = rope_tab.shape[1] // tm
    return pl.pallas_call(
        _inproj_prompt_kernel,
        grid=(m // tm,),
        in_specs=[
            pl.BlockSpec((tm, D_MODEL), lambda i: (i, 0)),
            _const_spec((1, D_MODEL)),
            _const_spec((D_MODEL, IN_COLS)),
            pl.BlockSpec((3, tm, LANES), lambda i: (0, i % nrt, 0)),
        ],
        out_specs=[
            pl.BlockSpec((tm, ATTN_WIDTH), lambda i: (i, 0)),
            pl.BlockSpec((tm, KV_WIDTH), lambda i: (i, 0)),
            pl.BlockSpec((tm, KV_WIDTH), lambda i: (i, 0)),
            pl.BlockSpec((N_SLAB, tm, LANES), lambda i: (0, i, 0)),
        ],
        out_shape=[
            jax.ShapeDtypeStruct((m, ATTN_WIDTH), BF16),
            jax.ShapeDtypeStruct((m, KV_WIDTH), F32),
            jax.ShapeDtypeStruct((m, KV_WIDTH), F32),
            jax.ShapeDtypeStruct((N_SLAB, m, LANES), F32),
        ],
        scratch_shapes=[pltpu.VMEM((tm, D_MODEL), BF16)],
        compiler_params=_params("parallel"),
        name="inproj_prompt",
    )(x, g, w_bf, rope_tab)


def _inproj_sample_kernel(x_ref, g_ref, w_ref, wkvt_ref, rope_ref, ropet_ref, q_ref, kt_ref, vt_ref, u_ref,
                          xn_ref):
    xn_ref[...] = _rms(x_ref[...], g_ref[...]).astype(BF16)
    _write_q(xn_ref, w_ref, rope_ref, q_ref)
    kvt = lax.dot_general(wkvt_ref[...], xn_ref[...], NT_DIMS, preferred_element_type=F32)
    kt = kvt[0:KV_WIDTH]
    sh = ROT_DIM // 2
    up = jnp.concatenate([kt[sh:], kt[:sh]], axis=0)
    dn = jnp.concatenate([kt[-sh:], kt[:-sh]], axis=0)
    kt_ref[...] = kt * ropet_ref[0] + up * ropet_ref[1] + dn * ropet_ref[2]
    vt_ref[...] = kvt[KV_WIDTH:]
    half = CONV_CH // 2
    for c in range(2):
        u_ref[:, c * half:(c + 1) * half] = _glu(xn_ref, w_ref, c)


def _inproj_sample(x, g, w_bf, wkvt_bf, rope_tab, ropet_tab, tm):
    m = x.shape[0]
    return pl.pallas_call(
        _inproj_sample_kernel,
        grid=(m // tm,),
        in_specs=[
            pl.BlockSpec((tm, D_MODEL), lambda i: (i, 0)),
            _const_spec((1, D_MODEL)),
            _const_spec((D_MODEL, IN_COLS)),
            _const_spec((2 * KV_WIDTH, D_MODEL)),
            _const_spec((3, tm, LANES)),
            _const_spec((3, KV_WIDTH, tm)),
        ],
        out_specs=[
            pl.BlockSpec((tm, ATTN_WIDTH), lambda i: (i, 0)),
            pl.BlockSpec((KV_WIDTH, tm), lambda i: (0, i)),
            pl.BlockSpec((KV_WIDTH, tm), lambda i: (0, i)),
            pl.BlockSpec((tm, CONV_CH), lambda i: (i, 0)),
        ],
        out_shape=[
            jax.ShapeDtypeStruct((m, ATTN_WIDTH), F32),
            jax.ShapeDtypeStruct((KV_WIDTH, m), F32),
            jax.ShapeDtypeStruct((KV_WIDTH, m), F32),
            jax.ShapeDtypeStruct((m, CONV_CH), F32),
        ],
        scratch_shapes=[pltpu.VMEM((tm, D_MODEL), BF16)],
        compiler_params=_params("parallel"),
        name="inproj_sample",
    )(x, g, w_bf, wkvt_bf, rope_tab, ropet_tab)


def _rope_table(pos):
    half = ROT_DIM // 2
    inv = jnp.power(jnp.float32(ROPE_THETA), -jnp.arange(half, dtype=F32) * 2.0 / ROT_DIM)
    ang = pos.astype(F32)[:, None] * inv[None, :]
    cos, sin = jnp.cos(ang), jnp.sin(ang)
    t = pos.shape[0]
    rest = HEAD_DIM - ROT_DIM
    one = jnp.ones((t, rest), F32)
    zero = jnp.zeros((t, rest), F32)
    zh = jnp.zeros((t, half), F32)
    c = jnp.concatenate([cos, cos, one], axis=1)
    s_lo = jnp.concatenate([-sin, zh, zero], axis=1)
    s_hi = jnp.concatenate([zh, sin, zero], axis=1)
    tab = jnp.stack([c, s_lo, s_hi])
    return jnp.tile(tab, (1, 1, LANES // HEAD_DIM))


def _attn_prompt_kernel(sink_ref, q_ref, kc_ref, kp_ref, vc_ref, vp_ref, km_ref, vm_ref, o_ref,
                        kk_ref, vv_ref):
    n = pl.program_id(1)

    @pl.when(n == 0)
    def _():
        kk_ref[0:BLOCK] = km_ref[...]
        vv_ref[0:BLOCK] = vm_ref[...]

    @pl.when(n > 0)
    def _():
        kk_ref[0:BLOCK] = kp_ref[...]
        vv_ref[0:BLOCK] = vp_ref[...]

    kk_ref[BLOCK:2 * BLOCK] = kc_ref[...]
    vv_ref[BLOCK:2 * BLOCK] = vc_ref[...]

    lo = jnp.where(n == 0, BLOCK - N_META, 0)
    r = lax.broadcasted_iota(jnp.int32, (BLOCK, 2 * BLOCK), 0)
    j = lax.broadcasted_iota(jnp.int32, (BLOCK, 2 * BLOCK), 1)
    mask = (j > r) & (j <= r + BLOCK) & (j >= lo)
    low_half = lax.broadcasted_iota(jnp.int32, (BLOCK, LANES), 1) < HEAD_DIM
    low_half2 = lax.broadcasted_iota(jnp.int32, (2 * BLOCK, LANES), 1) < HEAD_DIM

    for c in range(KV_WIDTH // LANES):
        kf = kk_ref[:, c * LANES:(c + 1) * LANES]
        vf = vv_ref[:, c * LANES:(c + 1) * LANES]
        k_sw = pltpu.roll(kf, HEAD_DIM, 1)
        v_sw = pltpu.roll(vf, HEAD_DIM, 1)
        for hh in range(2):
            h = 2 * c + hh
            if hh == 0:
                k2 = jnp.where(low_half2, kf, k_sw).astype(BF16)
                v2 = jnp.where(low_half2, vf, v_sw).astype(BF16)
            else:
                k2 = jnp.where(low_half2, k_sw, kf).astype(BF16)
                v2 = jnp.where(low_half2, v_sw, vf).astype(BF16)
            qa = q_ref[:, (2 * h) * LANES:(2 * h + 1) * LANES]
            qb = q_ref[:, (2 * h + 1) * LANES:(2 * h + 2) * LANES]
            zero = jnp.zeros_like(qa)
            lhs = jnp.concatenate([jnp.where(low_half, qa, zero), jnp.where(low_half, qb, zero),
                                   jnp.where(low_half, zero, qa), jnp.where(low_half, zero, qb)], axis=0)
            s_all = lax.dot_general(lhs, k2, NT_DIMS, preferred_element_type=F32)
            heads = (4 * h, 4 * h + 2, 4 * h + 1, 4 * h + 3)
            es, dens = [], []
            for g in range(GQA_GROUP):
                s = jnp.where(mask, s_all[g * BLOCK:(g + 1) * BLOCK], MASK_VALUE)
                sink = sink_ref[heads[g]]
                mx = jnp.maximum(jnp.max(s, axis=-1, keepdims=True), sink)
                e = jnp.exp(s - mx)
                dens.append(jnp.sum(e, axis=-1, keepdims=True) + jnp.exp(sink - mx))
                es.append(e.astype(BF16))
            o_all = jnp.dot(jnp.concatenate(es, axis=0), v2, preferred_element_type=F32)
            o = [o_all[g * BLOCK:(g + 1) * BLOCK] / dens[g] for g in range(GQA_GROUP)]
            o_ref[:, (2 * h) * LANES:(2 * h + 1) * LANES] = jnp.where(low_half, o[0], o[2]).astype(o_ref.dtype)
            o_ref[:, (2 * h + 1) * LANES:(2 * h + 2) * LANES] = (
                jnp.where(low_half, o[1], o[3]).astype(o_ref.dtype))


def _attn_prompt(sinks, q, k, v, k_meta_blk, v_meta_blk, n_batch, n_blk):
    kv_cur = pl.BlockSpec((BLOCK, KV_WIDTH), lambda b, n: (b * n_blk + n, 0))
    kv_prev = pl.BlockSpec((BLOCK, KV_WIDTH), lambda b, n: (b * n_blk + jnp.maximum(n - 1, 0), 0))
    return pl.pallas_call(
        _attn_prompt_kernel,
        grid=(n_batch, n_blk),
        in_specs=[
            pl.BlockSpec(memory_space=pltpu.SMEM),
            pl.BlockSpec((BLOCK, ATTN_WIDTH), lambda b, n: (b * n_blk + n, 0)),
            kv_cur, kv_prev, kv_cur, kv_prev,
            _const_spec((BLOCK, KV_WIDTH)),
            _const_spec((BLOCK, KV_WIDTH)),
        ],
        out_specs=pl.BlockSpec((BLOCK, ATTN_WIDTH), lambda b, n: (b * n_blk + n, 0)),
        out_shape=jax.ShapeDtypeStruct((n_batch * n_blk * BLOCK, ATTN_WIDTH), BF16),
        scratch_shapes=[pltpu.VMEM((2 * BLOCK, KV_WIDTH), F32), pltpu.VMEM((2 * BLOCK, KV_WIDTH), F32)],
        compiler_params=_params("parallel", "arbitrary"),
        name="attn_prompt",
    )(sinks, q, k, k, v, v, k_meta_blk, v_meta_blk)


def _attn_sample_kernel(t_new, bt, sink_ref, q_ref, kn_ref, vn_ref, ck_ref, cv_ref,
                        o_ref, ko_ref, vo_ref, qrow_ref, osc_ref):
    nrow = N_HEADS * t_new
    w = ck_ref.shape[2]
    assert bt * t_new == LANES and w == LANES and nrow == LANES
    qrow_ref[...] = jnp.zeros_like(qrow_ref)

    r = lax.broadcasted_iota(jnp.int32, (nrow, 2 * w), 0) % t_new
    j = lax.broadcasted_iota(jnp.int32, (nrow, 2 * w), 1)
    lane = lax.broadcasted_iota(jnp.int32, (t_new, LANES), 1)
    low_half = lane < HEAD_DIM
    lane_kv = lax.broadcasted_iota(jnp.int32, (KV_WIDTH, LANES), 1)
    tail = lane_kv >= w - t_new
    sink = sink_ref[...][:, 0:1]
    kn = kn_ref[...]
    vn = vn_ref[...]
    kn_bf = kn.astype(BF16)
    vn_bf = vn.astype(BF16)

    for i in range(bt):
        ck = ck_ref[i]
        cv = cv_ref[i]
        new_shift = (w - t_new - i * t_new) % LANES
        ko_ref[i] = jnp.where(tail, pltpu.roll(kn, new_shift, 1), pltpu.roll(ck, w - t_new, 1))
        vo_ref[i] = jnp.where(tail, pltpu.roll(vn, new_shift, 1), pltpu.roll(cv, w - t_new, 1))

        for t in range(N_HEADS):
            h = t // GQA_GROUP
            qv = q_ref[i * t_new:(i + 1) * t_new, (t // 2) * LANES:(t // 2 + 1) * LANES]
            if t % 2 != h % 2:
                qv = pltpu.roll(qv, HEAD_DIM, 1)
            keep = low_half if h % 2 == 0 else jnp.logical_not(low_half)
            qrow_ref[t * t_new:(t + 1) * t_new, (h // 2) * LANES:(h // 2 + 1) * LANES] = (
                jnp.where(keep, qv, 0.0))

        rhs_k = jnp.concatenate([ck.astype(BF16), kn_bf], axis=1)
        s = jnp.dot(qrow_ref[...].astype(BF16), rhs_k, preferred_element_type=F32)
        own = w + i * t_new
        mask = ((j > r) & (j < w)) | ((j >= own) & (j <= own + r))
        s = jnp.where(mask, s, MASK_VALUE)
        mx = jnp.maximum(jnp.max(s, axis=-1, keepdims=True), sink)
        e = jnp.exp(s - mx)
        den = jnp.sum(e, axis=-1, keepdims=True) + jnp.exp(sink - mx)
        rhs_v = jnp.concatenate([cv.astype(BF16), vn_bf], axis=1)
        o_all = lax.dot_general(e.astype(BF16), rhs_v, NT_DIMS, preferred_element_type=F32) / den

        for t2 in range(N_HEADS // 2):
            parts = []
            for side in range(2):
                t = 2 * t2 + side
                h = t // GQA_GROUP
                blk = o_all[t * t_new:(t + 1) * t_new, (h // 2) * LANES:(h // 2 + 1) * LANES]
                if t % 2 != h % 2:
                    blk = pltpu.roll(blk, HEAD_DIM, 1)
                parts.append(blk)
            osc_ref[i * t_new:(i + 1) * t_new, t2 * LANES:(t2 + 1) * LANES] = (
                jnp.where(low_half, parts[0], parts[1]))

    o_ref[...] = osc_ref[...].astype(o_ref.dtype)


def _attn_sample(sink_rows, q, kt_new, vt_new, cache_kt, cache_vt, t_new, bt):
    nb, _, w = cache_kt.shape
    rows = bt * t_new
    cache_spec = pl.BlockSpec((bt, KV_WIDTH, w), lambda s: (s, 0, 0))
    new_spec = pl.BlockSpec((KV_WIDTH, rows), lambda s: (0, s))
    return pl.pallas_call(
        functools.partial(_attn_sample_kernel, t_new, bt),
        grid=(nb // bt,),
        in_specs=[
            _const_spec(sink_rows.shape),
            pl.BlockSpec((rows, ATTN_WIDTH), lambda s: (s, 0)),
            new_spec, new_spec, cache_spec, cache_spec,
        ],
        out_specs=[pl.BlockSpec((rows, ATTN_WIDTH), lambda s: (s, 0)), cache_spec, cache_spec],
        out_shape=[
            jax.ShapeDtypeStruct((nb * t_new, ATTN_WIDTH), BF16),
            jax.ShapeDtypeStruct(cache_kt.shape, F32),
            jax.ShapeDtypeStruct(cache_vt.shape, F32),
        ],
        scratch_shapes=[
            pltpu.VMEM((N_HEADS * t_new, KV_WIDTH), F32),
            pltpu.VMEM((rows, ATTN_WIDTH), F32),
        ],
        compiler_params=_params("parallel"),
        name="attn_sample",
    )(sink_rows, q, kt_new, vt_new, cache_kt, cache_vt)


def _ln_swish(y, g, b):
    mu = jnp.mean(y, axis=-1, keepdims=True)
    d = y - mu
    var = jnp.mean(d * d, axis=-1, keepdims=True)
    yn = d * lax.rsqrt(var + LN_EPS) * g + b
    return yn * jax.nn.sigmoid(yn)


CONV_HALO = 32
CONV_SEG = 252
CONV_SEG_STEP = 12
CONV_FIN_STEP = 36


def _conv_prompt_kernel(seq, u_ref, meta_ref, wb_ref, bdw_ref, g_ref, b_ref, o_ref, win_ref, y_ref):
    for c in range(N_SLAB):
        win_ref[c, 0:CONV_HALO] = meta_ref[c]
        win_ref[c, CONV_HALO:CONV_HALO + seq] = u_ref[c]
    first = CONV_HALO - (CONV_K - 1)

    rem_base = SUBLANES * CONV_SEG
    rem_seg = (seq - rem_base) // SUBLANES
    n_main = CONV_SEG // CONV_SEG_STEP

    def taps(c, wts, base, seg, n_g, i0, y0):
        accs = [None] * n_g
        for m in range(n_g + CONV_K - 1):
            v = win_ref[c, pl.ds(base + first + i0 + m, SUBLANES, stride=seg), :]
            for g in range(max(0, m - CONV_K + 1), min(n_g, m + 1)):
                term = v * wts[m - g]
                accs[g] = term if accs[g] is None else accs[g] + term
        for g in range(n_g):
            y_ref[y0 + g, :, c * LANES:(c + 1) * LANES] = accs[g]

    for c in range(N_SLAB):
        wts = [wb_ref[tap, c] for tap in range(CONV_K)]

        def body(ci, carry, c=c, wts=wts):
            i0 = ci * CONV_SEG_STEP
            taps(c, wts, 0, CONV_SEG, CONV_SEG_STEP, i0, i0)
            return carry

        lax.fori_loop(0, n_main, body, 0)
        taps(c, wts, rem_base, rem_seg, rem_seg, 0, CONV_SEG)

    def finish(base, seg, n_g, i0, y0):
        z = _ln_swish(y_ref[pl.ds(y0, n_g)] + bdw_ref[...], g_ref[...], b_ref[...])
        for c in range(N_SLAB):
            for g in range(n_g):
                o_ref[c, pl.ds(base + i0 + g, SUBLANES, stride=seg), :] = z[g, :, c * LANES:(c + 1) * LANES]

    def fin_body(ci, carry):
        i0 = ci * CONV_FIN_STEP
        finish(0, CONV_SEG, CONV_FIN_STEP, i0, i0)
        return carry

    lax.fori_loop(0, CONV_SEG // CONV_FIN_STEP, fin_body, 0)
    finish(rem_base, rem_seg, rem_seg, 0, CONV_SEG)


def _conv_prompt(u_slabs, meta_halo, wb, b_dw, ln_g, ln_b, n_batch, seq):
    assert CONV_SEG % CONV_SEG_STEP == 0 and CONV_SEG % CONV_FIN_STEP == 0 and seq > SUBLANES * CONV_SEG
    assert (seq - SUBLANES * CONV_SEG) % SUBLANES == 0
    blk = pl.BlockSpec((N_SLAB, seq, LANES), lambda b: (0, b, 0))
    return pl.pallas_call(
        functools.partial(_conv_prompt_kernel, seq),
        grid=(n_batch,),
        in_specs=[
            blk,
            _const_spec((N_SLAB, CONV_HALO, LANES)),
            _const_spec((CONV_K, N_SLAB, SUBLANES, LANES)),
            _const_spec((1, CONV_CH)),
            _const_spec((1, CONV_CH)),
            _const_spec((1, CONV_CH)),
        ],
        out_specs=blk,
        out_shape=jax.ShapeDtypeStruct((N_SLAB, n_batch * seq, LANES), F32),
        scratch_shapes=[
            pltpu.VMEM((N_SLAB, CONV_HALO + seq, LANES), F32),
            pltpu.VMEM((seq // SUBLANES, SUBLANES, CONV_CH), F32),
        ],
        compiler_params=_params("parallel"),
        name="conv_prompt",
    )(u_slabs, meta_halo, wb, b_dw, ln_g, ln_b)


def _conv_sample_kernel(t_new, st_ref, u_ref, wb_ref, bdw_ref, g_ref, b_ref, o_ref, so_ref):
    hist = st_ref.shape[0]
    bb = st_ref.shape[1]
    so_ref[0:hist - t_new] = st_ref[t_new:hist]
    so_ref[hist - t_new:hist] = u_ref[...]

    def src(k, rows):
        return st_ref[k, rows, :] if k < hist else u_ref[k - hist, rows, :]

    def body(rc, carry):
        rows = pl.ds(pl.multiple_of(rc * SUBLANES, SUBLANES), SUBLANES)
        for t in range(t_new):
            acc = jnp.zeros((SUBLANES, CONV_CH), F32)
            for tap in range(CONV_K):
                acc = acc + src(t + tap, rows) * wb_ref[tap]
            o_ref[t, rows, :] = _ln_swish(acc + bdw_ref[...], g_ref[...], b_ref[...]).astype(o_ref.dtype)
        return carry

    lax.fori_loop(0, bb // SUBLANES, body, 0)


def _conv_sample(state_t, u_t, wb, b_dw, ln_g, ln_b, bb):
    hist, nb, _ = state_t.shape
    t_new = u_t.shape[0]
    return pl.pallas_call(
        functools.partial(_conv_sample_kernel, t_new),
        grid=(nb // bb,),
        in_specs=[
            pl.BlockSpec((hist, bb, CONV_CH), lambda s: (0, s, 0)),
            pl.BlockSpec((t_new, bb, CONV_CH), lambda s: (0, s, 0)),
            _const_spec((CONV_K, SUBLANES, CONV_CH)),
            _const_spec((1, CONV_CH)),
            _const_spec((1, CONV_CH)),
            _const_spec((1, CONV_CH)),
        ],
        out_specs=[
            pl.BlockSpec((t_new, bb, CONV_CH), lambda s: (0, s, 0)),
            pl.BlockSpec((hist, bb, CONV_CH), lambda s: (0, s, 0)),
        ],
        out_shape=[
            jax.ShapeDtypeStruct((t_new, nb, CONV_CH), F32),
            jax.ShapeDtypeStruct(state_t.shape, F32),
        ],
        compiler_params=_params("parallel"),
        name="conv_sample",
    )(state_t, u_t, wb, b_dw, ln_g, ln_b)


def _mlp_kernel(conv_slabs, x_ref, a_ref, c_ref, wo_ref, g2_ref, wu_ref, wd_ref, gf_ref, y_ref,
                acc_ref, xn_ref, cbf_ref):
    f = pl.program_id(1)

    @pl.when(f == 0)
    def _():
        if conv_slabs:
            for c in range(N_SLAB):
                cbf_ref[:, c * LANES:(c + 1) * LANES] = c_ref[c].astype(BF16)
        else:
            cbf_ref[...] = c_ref[...].astype(BF16)
        mix = jnp.dot(a_ref[...], wo_ref[0:ATTN_WIDTH], preferred_element_type=F32)
        mix = mix + jnp.dot(cbf_ref[...], wo_ref[ATTN_WIDTH:], preferred_element_type=F32)
        h1 = x_ref[...] + mix
        acc_ref[...] = h1
        xn_ref[...] = _rms(h1, g2_ref[...]).astype(BF16)

    up = jnp.dot(xn_ref[...], wu_ref[...], preferred_element_type=F32)
    act = jnp.maximum(up, 0.0)
    act = (act * act).astype(BF16)
    acc_ref[...] += jnp.dot(act, wd_ref[...], preferred_element_type=F32)

    @pl.when(f == pl.num_programs(1) - 1)
    def _():
        y_ref[...] = _rms(acc_ref[...], gf_ref[...])


def _outproj_mlp(x, attn, conv, wo_bf, g2, wu_bf, wd_bf, gf, tm, tf, conv_slabs):
    m = x.shape[0]
    if conv_slabs:
        conv_spec = pl.BlockSpec((N_SLAB, tm, LANES), lambda i, f: (0, i, 0))
    else:
        conv_spec = pl.BlockSpec((tm, CONV_CH), lambda i, f: (i, 0))
    return pl.pallas_call(
        functools.partial(_mlp_kernel, conv_slabs),
        grid=(m // tm, D_FF // tf),
        in_specs=[
            pl.BlockSpec((tm, D_MODEL), lambda i, f: (i, 0)),
            pl.BlockSpec((tm, ATTN_WIDTH), lambda i, f: (i, 0)),
            conv_spec,
            _const_spec((D_MODEL, D_MODEL)),
            _const_spec((1, D_MODEL)),
            pl.BlockSpec((D_MODEL, tf), lambda i, f: (0, f)),
            pl.BlockSpec((tf, D_MODEL), lambda i, f: (f, 0)),
            _const_spec((1, D_MODEL)),
        ],
        out_specs=pl.BlockSpec((tm, D_MODEL), lambda i, f: (i, 0)),
        out_shape=jax.ShapeDtypeStruct((m, D_MODEL), F32),
        scratch_shapes=[
            pltpu.VMEM((tm, D_MODEL), F32),
            pltpu.VMEM((tm, D_MODEL), BF16),
            pltpu.VMEM((tm, CONV_CH), BF16),
        ],
        compiler_params=_params("parallel", "arbitrary"),
        name="outproj_mlp",
    )(x, attn, conv, wo_bf, g2, wu_bf, wd_bf, gf)


def kernel(x_prompt, x_sample, cache_k, cache_v, state_conv, meta_tokens, norm_mix, w_in, attn_sinks,
           w_dw, b_dw, conv_ln_g, conv_ln_b, w_out, norm_mlp, w_up, w_down, norm_final):
    n_batch, seq, _ = x_prompt.shape
    n_dec, t_new, _ = x_sample.shape
    depth = w_in.shape[0]
    window = cache_k.shape[2]
    assert depth == 1 and seq % BLOCK == 0 and t_new == SUBLANES and window == WINDOW
    l = 0
    n_blk = seq // BLOCK
    tm = 512
    tf = 512
    bt = LANES // t_new

    w_in_bf = w_in[l].astype(BF16)
    w_kvt_bf = w_in[l, :, K_OFF:A_OFF].T.astype(BF16)
    w_out_bf = w_out[l].astype(BF16)
    w_up_bf = w_up[l].astype(BF16)
    w_down_bf = w_down[l].astype(BF16)
    g_mix = norm_mix[l][None, :]
    g_mlp = norm_mlp[l][None, :]
    g_fin = norm_final[None, :]
    sinks = attn_sinks[l].astype(F32)
    bdw = b_dw[l][None, :]
    ln_g = conv_ln_g[l][None, :]
    ln_b = conv_ln_b[l][None, :]
    wb_rows = jnp.broadcast_to(w_dw[l][:, None, :], (CONV_K, SUBLANES, CONV_CH))
    wb_slabs = wb_rows.reshape(CONV_K, SUBLANES, N_SLAB, LANES).transpose(0, 2, 1, 3)

    rope_p = _rope_table(N_META + jnp.arange(seq, dtype=jnp.int32))
    rope_s = _rope_table(PAST_LEN + (jnp.arange(tm, dtype=jnp.int32) % t_new))
    ropet_s = jnp.tile(rope_s[:, :, :HEAD_DIM].transpose(0, 2, 1), (1, N_KV_HEADS, 1))
    rope_m = _rope_table(jnp.arange(N_META, dtype=jnp.int32))

    xp = x_prompt.reshape(n_batch * seq, D_MODEL)
    xs = x_sample.reshape(n_dec * t_new, D_MODEL)

    q_p, k_p, v_p, u_p = _inproj_prompt(xp, g_mix, w_in_bf, rope_p, tm)
    q_s, kt_s, vt_s, u_s = _inproj_sample(xs, g_mix, w_in_bf, w_kvt_bf, rope_s, ropet_s, tm)
    _, k_m, v_m, u_m = _inproj_prompt(meta_tokens.astype(F32), g_mix, w_in_bf, rope_m, N_META)

    pad_rows = jnp.zeros((BLOCK - N_META, KV_WIDTH), F32)
    k_meta_blk = jnp.concatenate([pad_rows, k_m], axis=0)
    v_meta_blk = jnp.concatenate([pad_rows, v_m], axis=0)
    attn_p = _attn_prompt(sinks, q_p, k_p, v_p, k_meta_blk, v_meta_blk, n_batch, n_blk)
    meta_halo = jnp.concatenate([jnp.zeros((N_SLAB, CONV_HALO - N_META, LANES), F32), u_m], axis=1)
    conv_p = _conv_prompt(u_p, meta_halo, wb_slabs, bdw, ln_g, ln_b, n_batch, seq)

    ckt = cache_k[l].transpose(0, 2, 3, 1).reshape(n_dec, KV_WIDTH, window)
    cvt = cache_v[l].transpose(0, 2, 3, 1).reshape(n_dec, KV_WIDTH, window)
    sink_rows = jnp.broadcast_to(jnp.repeat(sinks, t_new)[:, None], (N_HEADS * t_new, LANES))
    attn_s, kt_out, vt_out = _attn_sample(sink_rows, q_s, kt_s, vt_s, ckt, cvt, t_new, bt)
    state_t = state_conv[l].transpose(1, 0, 2)
    u_t = u_s.reshape(n_dec, t_new, CONV_CH).transpose(1, 0, 2)
    conv_t, state_out_t = _conv_sample(state_t, u_t, wb_rows, bdw, ln_g, ln_b, 32)
    conv_s = conv_t.transpose(1, 0, 2).reshape(n_dec * t_new, CONV_CH)

    y_p = _outproj_mlp(xp, attn_p, conv_p, w_out_bf, g_mlp, w_up_bf, w_down_bf, g_fin, tm, tf, True)
    y_s = _outproj_mlp(xs, attn_s, conv_s, w_out_bf, g_mlp, w_up_bf, w_down_bf, g_fin, tm, tf, False)

    y_prompt = y_p.reshape(n_batch, seq, D_MODEL)
    y_sample = y_s.reshape(n_dec, t_new, D_MODEL)
    k_last = k_p.reshape(n_batch, seq, KV_WIDTH)[:, seq - WINDOW:]
    v_last = v_p.reshape(n_batch, seq, KV_WIDTH)[:, seq - WINDOW:]
    new_k_prompt = k_last.reshape(n_batch, WINDOW, N_KV_HEADS, HEAD_DIM)[None]
    new_v_prompt = v_last.reshape(n_batch, WINDOW, N_KV_HEADS, HEAD_DIM)[None]
    u_last = u_p.reshape(N_SLAB, n_batch, seq, LANES)[:, :, seq - (CONV_K - 1):]
    new_conv_prompt = u_last.transpose(1, 2, 0, 3).reshape(n_batch, CONV_K - 1, CONV_CH)[None]
    new_k_sample = kt_out.reshape(n_dec, N_KV_HEADS, HEAD_DIM, window).transpose(0, 3, 1, 2)[None]
    new_v_sample = vt_out.reshape(n_dec, N_KV_HEADS, HEAD_DIM, window).transpose(0, 3, 1, 2)[None]
    new_conv_sample = state_out_t.transpose(1, 0, 2)[None]
    return (y_prompt, y_sample, new_k_prompt, new_v_prompt, new_conv_prompt,
            new_k_sample, new_v_sample, new_conv_sample)
```

```python
import functools

import jax
import jax.numpy as jnp
from jax import lax
from jax.experimental import pallas as pl
from jax.experimental.pallas import tpu as pltpu

D_MODEL = 2048
N_META = 16
HEAD_DIM = 64
ATTN_WIDTH = D_MODEL // 2
CONV_CH = D_MODEL - ATTN_WIDTH
N_HEADS = ATTN_WIDTH // HEAD_DIM
N_KV_HEADS = 4
GQA_GROUP = N_HEADS // N_KV_HEADS
KV_WIDTH = N_KV_HEADS * HEAD_DIM
IN_COLS = ATTN_WIDTH + 2 * KV_WIDTH + 2 * CONV_CH
WINDOW = 128
BLOCK = 128
ROPE_THETA = 500000.0
ROT_DIM = HEAD_DIM // 4
CONV_K = 31
D_FF = 4 * D_MODEL
RMS_EPS = 1e-6
LN_EPS = 1e-5
MASK_VALUE = -1e30
PAST_LEN = 8192

LANES = 128
SUBLANES = 8
VMEM_LIMIT = 56 * 1024 * 1024
N_SLAB = CONV_CH // LANES

F32 = jnp.float32
BF16 = jnp.bfloat16

K_OFF = ATTN_WIDTH
V_OFF = ATTN_WIDTH + KV_WIDTH
A_OFF = ATTN_WIDTH + 2 * KV_WIDTH
B_OFF = A_OFF + CONV_CH
NT_DIMS = (((1,), (1,)), ((), ()))
LOG2_E = 1.4426950408889634
Q_SCALE = HEAD_DIM ** -0.5 * LOG2_E


def _params(*sem):
    return pltpu.CompilerParams(dimension_semantics=sem, vmem_limit_bytes=VMEM_LIMIT)


def _const_spec(shape):
    nd = len(shape)
    return pl.BlockSpec(shape, lambda *_: (0,) * nd, pipeline_mode=pl.Buffered(1))


def _rms(x, g):
    ms = jnp.mean(x * x, axis=-1, keepdims=True)
    return (x * lax.rsqrt(ms + RMS_EPS)) * g


def _rope_rows(z, rope_ref):
    return (z * rope_ref[0] + pltpu.roll(z, LANES - ROT_DIM // 2, 1) * rope_ref[1]
            + pltpu.roll(z, ROT_DIM // 2, 1) * rope_ref[2])


def _proj(xn_ref, w_ref, off, width):
    return jnp.dot(xn_ref[...], w_ref[:, off:off + width], preferred_element_type=F32)


def _write_q(xn_ref, w_ref, rope_ref, q_ref):
    half = ATTN_WIDTH // 2
    for c in range(2):
        acc = _proj(xn_ref, w_ref, c * half, half)
        for cc in range(half // LANES):
            z = _rope_rows(acc[:, cc * LANES:(cc + 1) * LANES], rope_ref) * Q_SCALE
            q_ref[:, c * half + cc * LANES:c * half + (cc + 1) * LANES] = z.astype(q_ref.dtype)


def _glu(xn_ref, w_ref, c):
    half = CONV_CH // 2
    a = _proj(xn_ref, w_ref, A_OFF + c * half, half)
    b = _proj(xn_ref, w_ref, B_OFF + c * half, half)
    return a * jax.nn.sigmoid(b)


def _inproj_prompt_kernel(x_ref, g_ref, w_ref, rope_ref, q_ref, k_ref, v_ref, u_ref, xn_ref):
    xn_ref[...] = _rms(x_ref[...], g_ref[...]).astype(BF16)
    _write_q(xn_ref, w_ref, rope_ref, q_ref)
    acc = _proj(xn_ref, w_ref, K_OFF, 2 * KV_WIDTH)
    for cc in range(KV_WIDTH // LANES):
        k_ref[:, cc * LANES:(cc + 1) * LANES] = _rope_rows(acc[:, cc * LANES:(cc + 1) * LANES], rope_ref)
    v_ref[...] = acc[:, KV_WIDTH:]
    per = N_SLAB // 2
    for c in range(2):
        u = _glu(xn_ref, w_ref, c)
        for cc in range(per):
            u_ref[c * per + cc] = u[:, cc * LANES:(cc + 1) * LANES]


def _inproj_prompt(x, g, w_bf, rope_tab, tm):
    m = x.shape[0]
    nrt = rope_tab.shape[1] // tm
    return pl.pallas_call(
        _inproj_prompt_kernel,
        grid=(m // tm,),
        in_specs=[
            pl.BlockSpec((tm, D_MODEL), lambda i: (i, 0)),
            _const_spec((1, D_MODEL)),
            _const_spec((D_MODEL, IN_COLS)),
            pl.BlockSpec((3, tm, LANES), lambda i: (0, i % nrt, 0)),
        ],
        out_specs=[
            pl.BlockSpec((tm, ATTN_WIDTH), lambda i: (i, 0)),
            pl.BlockSpec((tm, KV_WIDTH), lambda i: (i, 0)),
            pl.BlockSpec((tm, KV_WIDTH), lambda i: (i, 0)),
            pl.BlockSpec((N_SLAB, tm, LANES), lambda i: (0, i, 0)),
        ],
        out_shape=[
            jax.ShapeDtypeStruct((m, ATTN_WIDTH), BF16),
            jax.ShapeDtypeStruct((m, KV_WIDTH), F32),
            jax.ShapeDtypeStruct((m, KV_WIDTH), F32),
            jax.ShapeDtypeStruct((N_SLAB, m, LANES), F32),
        ],
        scratch_shapes=[pltpu.VMEM((tm, D_MODEL), BF16)],
        compiler_params=_params("parallel"),
        name="inproj_prompt",
    )(x, g, w_bf, rope_tab)


def _inproj_sample_kernel(x_ref, g_ref, w_ref, rope_ref, ropet_ref, q_ref, kt_ref, vt_ref, u_ref, xn_ref):
    xn_ref[...] = _rms(x_ref[...], g_ref[...]).astype(BF16)
    _write_q(xn_ref, w_ref, rope_ref, q_ref)
    kvt = _proj(xn_ref, w_ref, K_OFF, 2 * KV_WIDTH).T
    kt = kvt[0:KV_WIDTH]
    sh = ROT_DIM // 2
    up = jnp.concatenate([kt[sh:], kt[:sh]], axis=0)
    dn = jnp.concatenate([kt[-sh:], kt[:-sh]], axis=0)
    kt_ref[...] = kt * ropet_ref[0] + up * ropet_ref[1] + dn * ropet_ref[2]
    vt_ref[...] = kvt[KV_WIDTH:]
    half = CONV_CH // 2
    for c in range(2):
        u_ref[:, c * half:(c + 1) * half] = _glu(xn_ref, w_ref, c)


def _inproj_sample(x, g, w_bf, rope_tab, ropet_tab, tm):
    m = x.shape[0]
    return pl.pallas_call(
        _inproj_sample_kernel,
        grid=(m // tm,),
        in_specs=[
            pl.BlockSpec((tm, D_MODEL), lambda i: (i, 0)),
            _const_spec((1, D_MODEL)),
            _const_spec((D_MODEL, IN_COLS)),
            _const_spec((3, tm, LANES)),
            _const_spec((3, KV_WIDTH, tm)),
        ],
        out_specs=[
            pl.BlockSpec((tm, ATTN_WIDTH), lambda i: (i, 0)),
            pl.BlockSpec((KV_WIDTH, tm), lambda i: (0, i)),
            pl.BlockSpec((KV_WIDTH, tm), lambda i: (0, i)),
            pl.BlockSpec((tm, CONV_CH), lambda i: (i, 0)),
        ],
        out_shape=[
            jax.ShapeDtypeStruct((m, ATTN_WIDTH), F32),
            jax.ShapeDtypeStruct((KV_WIDTH, m), F32),
            jax.ShapeDtypeStruct((KV_WIDTH, m), F32),
            jax.ShapeDtypeStruct((m, CONV_CH), F32),
        ],
        scratch_shapes=[pltpu.VMEM((tm, D_MODEL), BF16)],
        compiler_params=_params("parallel"),
        name="inproj_sample",
    )(x, g, w_bf, rope_tab, ropet_tab)


def _rope_table(pos):
    half = ROT_DIM // 2
    inv = jnp.power(jnp.float32(ROPE_THETA), -jnp.arange(half, dtype=F32) * 2.0 / ROT_DIM)
    ang = pos.astype(F32)[:, None] * inv[None, :]
    cos, sin = jnp.cos(ang), jnp.sin(ang)
    t = pos.shape[0]
    rest = HEAD_DIM - ROT_DIM
    one = jnp.ones((t, rest), F32)
    zero = jnp.zeros((t, rest), F32)
    zh = jnp.zeros((t, half), F32)
    c = jnp.concatenate([cos, cos, one], axis=1)
    s_lo = jnp.concatenate([-sin, zh, zero], axis=1)
    s_hi = jnp.concatenate([zh, sin, zero], axis=1)
    tab = jnp.stack([c, s_lo, s_hi])
    return jnp.tile(tab, (1, 1, LANES // HEAD_DIM))


def _attn_prompt_kernel(sink_ref, q_ref, kc_ref, kp_ref, vc_ref, vp_ref, km_ref, vm_ref, o_ref,
                        kk_ref, vv_ref):
    n = pl.program_id(1)

    @pl.when(n == 0)
    def _():
        kk_ref[0:BLOCK] = km_ref[...]
        vv_ref[0:BLOCK] = vm_ref[...]

    @pl.when(n > 0)
    def _():
        kk_ref[0:BLOCK] = kp_ref[...]
        vv_ref[0:BLOCK] = vp_ref[...]

    kk_ref[BLOCK:2 * BLOCK] = kc_ref[...]
    vv_ref[BLOCK:2 * BLOCK] = vc_ref[...]

    lo = jnp.where(n == 0, BLOCK - N_META, 0)
    r = lax.broadcasted_iota(jnp.int32, (BLOCK, 2 * BLOCK), 0)
    j = lax.broadcasted_iota(jnp.int32, (BLOCK, 2 * BLOCK), 1)
    mask = (j > r) & (j <= r + BLOCK) & (j >= lo)
    low_half = lax.broadcasted_iota(jnp.int32, (BLOCK, LANES), 1) < HEAD_DIM
    low_half2 = lax.broadcasted_iota(jnp.int32, (2 * BLOCK, LANES), 1) < HEAD_DIM

    for c in range(KV_WIDTH // LANES):
        kf = kk_ref[:, c * LANES:(c + 1) * LANES]
        vf = vv_ref[:, c * LANES:(c + 1) * LANES]
        k_sw = pltpu.roll(kf, HEAD_DIM, 1)
        v_sw = pltpu.roll(vf, HEAD_DIM, 1)
        for hh in range(2):
            h = 2 * c + hh
            if hh == 0:
                k2 = jnp.where(low_half2, kf, k_sw).astype(BF16)
                v2 = jnp.where(low_half2, vf, v_sw).astype(BF16)
            else:
                k2 = jnp.where(low_half2, k_sw, kf).astype(BF16)
                v2 = jnp.where(low_half2, v_sw, vf).astype(BF16)
            qa = q_ref[:, (2 * h) * LANES:(2 * h + 1) * LANES]
            qb = q_ref[:, (2 * h + 1) * LANES:(2 * h + 2) * LANES]
            zero = jnp.zeros_like(qa)
            lhs = jnp.concatenate([jnp.where(low_half, qa, zero), jnp.where(low_half, qb, zero),
                                   jnp.where(low_half, zero, qa), jnp.where(low_half, zero, qb)], axis=0)
            s_all = lax.dot_general(lhs, k2, NT_DIMS, preferred_element_type=F32)
            heads = (4 * h, 4 * h + 2, 4 * h + 1, 4 * h + 3)
            es, dens = [], []
            for g in range(GQA_GROUP):
                s = jnp.where(mask, s_all[g * BLOCK:(g + 1) * BLOCK], MASK_VALUE)
                sink = sink_ref[heads[g]]
                mx = jnp.maximum(jnp.max(s, axis=-1, keepdims=True), sink)
                e = jnp.exp2(s - mx)
                dens.append(jnp.sum(e, axis=-1, keepdims=True) + jnp.exp2(sink - mx))
                es.append(e.astype(BF16))
            o_all = jnp.dot(jnp.concatenate(es, axis=0), v2, preferred_element_type=F32)
            o = [o_all[g * BLOCK:(g + 1) * BLOCK] / dens[g] for g in range(GQA_GROUP)]
            o_ref[:, (2 * h) * LANES:(2 * h + 1) * LANES] = jnp.where(low_half, o[0], o[2]).astype(o_ref.dtype)
            o_ref[:, (2 * h + 1) * LANES:(2 * h + 2) * LANES] = (
                jnp.where(low_half, o[1], o[3]).astype(o_ref.dtype))


def _attn_prompt(sinks, q, k, v, k_meta_blk, v_meta_blk, n_batch, n_blk):
    kv_cur = pl.BlockSpec((BLOCK, KV_WIDTH), lambda b, n: (b * n_blk + n, 0))
    kv_prev = pl.BlockSpec((BLOCK, KV_WIDTH), lambda b, n: (b * n_blk + jnp.maximum(n - 1, 0), 0))
    return pl.pallas_call(
        _attn_prompt_kernel,
        grid=(n_batch, n_blk),
        in_specs=[
            pl.BlockSpec(memory_space=pltpu.SMEM),
            pl.BlockSpec((BLOCK, ATTN_WIDTH), lambda b, n: (b * n_blk + n, 0)),
            kv_cur, kv_prev, kv_cur, kv_prev,
            _const_spec((BLOCK, KV_WIDTH)),
            _const_spec((BLOCK, KV_WIDTH)),
        ],
        out_specs=pl.BlockSpec((BLOCK, ATTN_WIDTH), lambda b, n: (b * n_blk + n, 0)),
        out_shape=jax.ShapeDtypeStruct((n_batch * n_blk * BLOCK, ATTN_WIDTH), BF16),
        scratch_shapes=[pltpu.VMEM((2 * BLOCK, KV_WIDTH), F32), pltpu.VMEM((2 * BLOCK, KV_WIDTH), F32)],
        compiler_params=_params("parallel", "arbitrary"),
        name="attn_prompt",
    )(sinks, q, k, k, v, v, k_meta_blk, v_meta_blk)


def _attn_sample_kernel(t_new, bt, sink_ref, q_ref, kn_ref, vn_ref, ck_ref, cv_ref,
                        o_ref, ko_ref, vo_ref, s_ref, p_ref, inv_ref, osc_ref):
    nrow = N_HEADS * t_new
    w = ck_ref.shape[2]
    assert bt * t_new == LANES and w == LANES and nrow == LANES
    zero_blk = jnp.zeros((t_new, LANES), F32)

    r = lax.broadcasted_iota(jnp.int32, (nrow, 2 * w), 0) % t_new
    j = lax.broadcasted_iota(jnp.int32, (nrow, 2 * w), 1)
    cache_ok = (j > r) & (j < w)
    low_half = lax.broadcasted_iota(jnp.int32, (t_new, LANES), 1) < HEAD_DIM
    tail = lax.broadcasted_iota(jnp.int32, (KV_WIDTH, LANES), 1) >= w - t_new
    kn = kn_ref[...]
    vn = vn_ref[...]
    kn_bf = kn.astype(BF16)
    vn_bf = vn.astype(BF16)

    for i in range(bt):
        new_shift = (w - t_new - i * t_new) % LANES
        ko_ref[i] = jnp.where(tail, pltpu.roll(kn, new_shift, 1), pltpu.roll(ck_ref[i], w - t_new, 1))
        vo_ref[i] = jnp.where(tail, pltpu.roll(vn, new_shift, 1), pltpu.roll(cv_ref[i], w - t_new, 1))

    for i in range(bt):
        qrows = []
        for t in range(N_HEADS):
            h = t // GQA_GROUP
            qv = q_ref[i * t_new:(i + 1) * t_new, (t // 2) * LANES:(t // 2 + 1) * LANES]
            if t % 2 != h % 2:
                qv = pltpu.roll(qv, HEAD_DIM, 1)
            keep = low_half if h % 2 == 0 else jnp.logical_not(low_half)
            cols = [zero_blk] * (KV_WIDTH // LANES)
            cols[h // 2] = jnp.where(keep, qv, 0.0)
            qrows.append(jnp.concatenate(cols, axis=1))
        qrow = jnp.concatenate(qrows, axis=0).astype(BF16)

        rhs_k = jnp.concatenate([ck_ref[i].astype(BF16), kn_bf], axis=1)
        s = jnp.dot(qrow, rhs_k, preferred_element_type=F32)
        own = w + i * t_new
        mask = cache_ok | ((j >= own) & (j <= own + r))
        s_ref[i * nrow:(i + 1) * nrow] = jnp.where(mask, s, MASK_VALUE)

    s = s_ref[...]
    sink = sink_ref[...][:, 0:1]
    mx = jnp.maximum(jnp.max(s, axis=-1, keepdims=True), sink)
    e = jnp.exp2(s - mx)
    den = jnp.sum(e, axis=-1, keepdims=True) + jnp.exp2(sink - mx)
    p_ref[...] = e.astype(BF16)
    inv_ref[...] = jnp.broadcast_to(1.0 / den, inv_ref.shape)

    for i in range(bt):
        rhs_v = jnp.concatenate([cv_ref[i].astype(BF16), vn_bf], axis=1)
        o_all = lax.dot_general(p_ref[i * nrow:(i + 1) * nrow], rhs_v, NT_DIMS, preferred_element_type=F32)
        for t2 in range(N_HEADS // 2):
            parts = []
            for side in range(2):
                t = 2 * t2 + side
                h = t // GQA_GROUP
                blk = o_all[t * t_new:(t + 1) * t_new, (h // 2) * LANES:(h // 2 + 1) * LANES]
                blk = blk * inv_ref[i * nrow + t * t_new:i * nrow + (t + 1) * t_new, :]
                if t % 2 != h % 2:
                    blk = pltpu.roll(blk, HEAD_DIM, 1)
                parts.append(blk)
            osc_ref[i * t_new:(i + 1) * t_new, t2 * LANES:(t2 + 1) * LANES] = (
                jnp.where(low_half, parts[0], parts[1]))

    o_ref[...] = osc_ref[...].astype(o_ref.dtype)


def _attn_sample(sink_rows, q, kt_new, vt_new, cache_kt, cache_vt, t_new, bt):
    nb, _, w = cache_kt.shape
    rows = bt * t_new
    cache_spec = pl.BlockSpec((bt, KV_WIDTH, w), lambda s: (s, 0, 0))
    new_spec = pl.BlockSpec((KV_WIDTH, rows), lambda s: (0, s))
    return pl.pallas_call(
        functools.partial(_attn_sample_kernel, t_new, bt),
        grid=(nb // bt,),
        in_specs=[
            _const_spec(sink_rows.shape),
            pl.BlockSpec((rows, ATTN_WIDTH), lambda s: (s, 0)),
            new_spec, new_spec, cache_spec, cache_spec,
        ],
        out_specs=[pl.BlockSpec((rows, ATTN_WIDTH), lambda s: (s, 0)), cache_spec, cache_spec],
        out_shape=[
            jax.ShapeDtypeStruct((nb * t_new, ATTN_WIDTH), BF16),
            jax.ShapeDtypeStruct(cache_kt.shape, F32),
            jax.ShapeDtypeStruct(cache_vt.shape, F32),
        ],
        scratch_shapes=[
            pltpu.VMEM((bt * N_HEADS * t_new, 2 * w), F32),
            pltpu.VMEM((bt * N_HEADS * t_new, 2 * w), BF16),
            pltpu.VMEM((bt * N_HEADS * t_new, LANES), F32),
            pltpu.VMEM((rows, ATTN_WIDTH), F32),
        ],
        compiler_params=_params("parallel"),
        name="attn_sample",
    )(sink_rows, q, kt_new, vt_new, cache_kt, cache_vt)


def _ln_swish(y, g, b):
    mu = jnp.mean(y, axis=-1, keepdims=True)
    d = y - mu
    var = jnp.mean(d * d, axis=-1, keepdims=True)
    yn = d * lax.rsqrt(var + LN_EPS) * g + b
    return yn * jax.nn.sigmoid(yn)


CONV_HALO = 32
CONV_SEG = 252
CONV_SEG_STEP = 12
CONV_FIN_STEP = 36


def _conv_prompt_kernel(seq, u_ref, meta_ref, wb_ref, bdw_ref, g_ref, b_ref, o_ref, win_ref, y_ref):
    for c in range(N_SLAB):
        win_ref[c, 0:CONV_HALO] = meta_ref[c]
        win_ref[c, CONV_HALO:CONV_HALO + seq] = u_ref[c]
    first = CONV_HALO - (CONV_K - 1)

    rem_base = SUBLANES * CONV_SEG
    rem_seg = (seq - rem_base) // SUBLANES
    n_main = CONV_SEG // CONV_SEG_STEP

    def taps(c, wts, base, seg, n_g, i0, y0):
        accs = [None] * n_g
        for m in range(n_g + CONV_K - 1):
            v = win_ref[c, pl.ds(base + first + i0 + m, SUBLANES, stride=seg), :]
            for g in range(max(0, m - CONV_K + 1), min(n_g, m + 1)):
                term = v * wts[m - g]
                accs[g] = term if accs[g] is None else accs[g] + term
        for g in range(n_g):
            y_ref[y0 + g, :, c * LANES:(c + 1) * LANES] = accs[g]

    for c in range(N_SLAB):
        wts = [wb_ref[tap, c] for tap in range(CONV_K)]

        def body(ci, carry, c=c, wts=wts):
            i0 = ci * CONV_SEG_STEP
            taps(c, wts, 0, CONV_SEG, CONV_SEG_STEP, i0, i0)
            return carry

        lax.fori_loop(0, n_main, body, 0)
        taps(c, wts, rem_base, rem_seg, rem_seg, 0, CONV_SEG)

    def finish(base, seg, n_g, i0, y0):
        z = _ln_swish(y_ref[pl.ds(y0, n_g)] + bdw_ref[...], g_ref[...], b_ref[...])
        for c in range(N_SLAB):
            for g in range(n_g):
                o_ref[c, pl.ds(base + i0 + g, SUBLANES, stride=seg), :] = z[g, :, c * LANES:(c + 1) * LANES]

    def fin_body(ci, carry):
        i0 = ci * CONV_FIN_STEP
        finish(0, CONV_SEG, CONV_FIN_STEP, i0, i0)
        return carry

    lax.fori_loop(0, CONV_SEG // CONV_FIN_STEP, fin_body, 0)
    finish(rem_base, rem_seg, rem_seg, 0, CONV_SEG)


def _conv_prompt(u_slabs, meta_halo, wb, b_dw, ln_g, ln_b, n_batch, seq):
    assert CONV_SEG % CONV_SEG_STEP == 0 and CONV_SEG % CONV_FIN_STEP == 0 and seq > SUBLANES * CONV_SEG
    assert (seq - SUBLANES * CONV_SEG) % SUBLANES == 0
    blk = pl.BlockSpec((N_SLAB, seq, LANES), lambda b: (0, b, 0))
    return pl.pallas_call(
        functools.partial(_conv_prompt_kernel, seq),
        grid=(n_batch,),
        in_specs=[
            blk,
            _const_spec((N_SLAB, CONV_HALO, LANES)),
            _const_spec((CONV_K, N_SLAB, SUBLANES, LANES)),
            _const_spec((1, CONV_CH)),
            _const_spec((1, CONV_CH)),
            _const_spec((1, CONV_CH)),
        ],
        out_specs=blk,
        out_shape=jax.ShapeDtypeStruct((N_SLAB, n_batch * seq, LANES), F32),
        scratch_shapes=[
            pltpu.VMEM((N_SLAB, CONV_HALO + seq, LANES), F32),
            pltpu.VMEM((seq // SUBLANES, SUBLANES, CONV_CH), F32),
        ],
        compiler_params=_params("parallel"),
        name="conv_prompt",
    )(u_slabs, meta_halo, wb, b_dw, ln_g, ln_b)


def _conv_sample_kernel(t_new, st_ref, u_ref, wb_ref, bdw_ref, g_ref, b_ref, o_ref, so_ref):
    hist = st_ref.shape[0]
    bb = st_ref.shape[1]
    so_ref[0:hist - t_new] = st_ref[t_new:hist]
    so_ref[hist - t_new:hist] = u_ref[...]

    def src(k, rows):
        return st_ref[k, rows, :] if k < hist else u_ref[k - hist, rows, :]

    def body(rc, carry):
        rows = pl.ds(pl.multiple_of(rc * SUBLANES, SUBLANES), SUBLANES)
        for t in range(t_new):
            acc = jnp.zeros((SUBLANES, CONV_CH), F32)
            for tap in range(CONV_K):
                acc = acc + src(t + tap, rows) * wb_ref[tap]
            o_ref[t, rows, :] = _ln_swish(acc + bdw_ref[...], g_ref[...], b_ref[...]).astype(o_ref.dtype)
        return carry

    lax.fori_loop(0, bb // SUBLANES, body, 0)


def _conv_sample(state_t, u_t, wb, b_dw, ln_g, ln_b, bb):
    hist, nb, _ = state_t.shape
    t_new = u_t.shape[0]
    return pl.pallas_call(
        functools.partial(_conv_sample_kernel, t_new),
        grid=(nb // bb,),
        in_specs=[
            pl.BlockSpec((hist, bb, CONV_CH), lambda s: (0, s, 0)),
            pl.BlockSpec((t_new, bb, CONV_CH), lambda s: (0, s, 0)),
            _const_spec((CONV_K, SUBLANES, CONV_CH)),
            _const_spec((1, CONV_CH)),
            _const_spec((1, CONV_CH)),
            _const_spec((1, CONV_CH)),
        ],
        out_specs=[
            pl.BlockSpec((t_new, bb, CONV_CH), lambda s: (0, s, 0)),
            pl.BlockSpec((hist, bb, CONV_CH), lambda s: (0, s, 0)),
        ],
        out_shape=[
            jax.ShapeDtypeStruct((t_new, nb, CONV_CH), F32),
            jax.ShapeDtypeStruct(state_t.shape, F32),
        ],
        compiler_params=_params("parallel"),
        name="conv_sample",
    )(state_t, u_t, wb, b_dw, ln_g, ln_b)


def _outproj_kernel(conv_slabs, x_ref, a_ref, c_ref, wo_ref, h_ref, cbf_ref):
    if conv_slabs:
        for c in range(N_SLAB):
            cbf_ref[:, c * LANES:(c + 1) * LANES] = c_ref[c].astype(BF16)
    else:
        cbf_ref[...] = c_ref[...].astype(BF16)
    mix = jnp.dot(a_ref[...], wo_ref[0:ATTN_WIDTH], preferred_element_type=F32)
    mix = mix + jnp.dot(cbf_ref[...], wo_ref[ATTN_WIDTH:], preferred_element_type=F32)
    h_ref[...] = x_ref[...] + mix


def _outproj(x, attn, conv, wo_bf, tm, conv_slabs):
    m = x.shape[0]
    if conv_slabs:
        conv_spec = pl.BlockSpec((N_SLAB, tm, LANES), lambda i: (0, i, 0))
    else:
        conv_spec = pl.BlockSpec((tm, CONV_CH), lambda i: (i, 0))
    return pl.pallas_call(
        functools.partial(_outproj_kernel, conv_slabs),
        grid=(m // tm,),
        in_specs=[
            pl.BlockSpec((tm, D_MODEL), lambda i: (i, 0)),
            pl.BlockSpec((tm, ATTN_WIDTH), lambda i: (i, 0)),
            conv_spec,
            _const_spec((D_MODEL, D_MODEL)),
        ],
        out_specs=pl.BlockSpec((tm, D_MODEL), lambda i: (i, 0)),
        out_shape=jax.ShapeDtypeStruct((m, D_MODEL), F32),
        scratch_shapes=[pltpu.VMEM((tm, CONV_CH), BF16)],
        compiler_params=_params("parallel"),
        name="outproj",
    )(x, attn, conv, wo_bf)


def _mlp_kernel(h_ref, g2_ref, wu_ref, wd_ref, gf_ref, y_ref, xn_ref):
    f = pl.program_id(1)

    @pl.when(f == 0)
    def _():
        h1 = h_ref[...]
        y_ref[...] = h1
        xn_ref[...] = _rms(h1, g2_ref[...]).astype(BF16)

    up = jnp.dot(xn_ref[...], wu_ref[...], preferred_element_type=F32)
    act = jnp.maximum(up, 0.0)
    act = (act * act).astype(BF16)
    y_ref[...] += jnp.dot(act, wd_ref[...], preferred_element_type=F32)

    @pl.when(f == pl.num_programs(1) - 1)
    def _():
        y_ref[...] = _rms(y_ref[...], gf_ref[...])


def _mlp(h1, g2, wu_bf, wd_bf, gf, tm, tf):
    m = h1.shape[0]
    return pl.pallas_call(
        _mlp_kernel,
        grid=(m // tm, D_FF // tf),
        in_specs=[
            pl.BlockSpec((tm, D_MODEL), lambda i, f: (i, 0)),
            _const_spec((1, D_MODEL)),
            pl.BlockSpec((D_MODEL, tf), lambda i, f: (0, f)),
            pl.BlockSpec((tf, D_MODEL), lambda i, f: (f, 0)),
            _const_spec((1, D_MODEL)),
        ],
        out_specs=pl.BlockSpec((tm, D_MODEL), lambda i, f: (i, 0)),
        out_shape=jax.ShapeDtypeStruct((m, D_MODEL), F32),
        scratch_shapes=[pltpu.VMEM((tm, D_MODEL), BF16)],
        compiler_params=_params("parallel", "arbitrary"),
        name="mlp",
    )(h1, g2, wu_bf, wd_bf, gf)


def kernel(x_prompt, x_sample, cache_k, cache_v, state_conv, meta_tokens, norm_mix, w_in, attn_sinks,
           w_dw, b_dw, conv_ln_g, conv_ln_b, w_out, norm_mlp, w_up, w_down, norm_final):
    n_batch, seq, _ = x_prompt.shape
    n_dec, t_new, _ = x_sample.shape
    depth = w_in.shape[0]
    window = cache_k.shape[2]
    assert depth == 1 and seq % BLOCK == 0 and t_new == SUBLANES and window == WINDOW
    l = 0
    n_blk = seq // BLOCK
    tm = 512
    tm_mlp = 1024
    tf = 512
    bt = LANES // t_new

    w_in_bf = w_in[l].astype(BF16)
    w_out_bf = w_out[l].astype(BF16)
    w_up_bf = w_up[l].astype(BF16)
    w_down_bf = w_down[l].astype(BF16)
    g_mix = norm_mix[l][None, :]
    g_mlp = norm_mlp[l][None, :]
    g_fin = norm_final[None, :]
    sinks = attn_sinks[l].astype(F32) * LOG2_E
    bdw = b_dw[l][None, :]
    ln_g = conv_ln_g[l][None, :]
    ln_b = conv_ln_b[l][None, :]
    wb_rows = jnp.broadcast_to(w_dw[l][:, None, :], (CONV_K, SUBLANES, CONV_CH))
    wb_slabs = wb_rows.reshape(CONV_K, SUBLANES, N_SLAB, LANES).transpose(0, 2, 1, 3)

    rope_p = _rope_table(N_META + jnp.arange(seq, dtype=jnp.int32))
    rope_s = _rope_table(PAST_LEN + (jnp.arange(tm, dtype=jnp.int32) % t_new))
    ropet_s = jnp.tile(rope_s[:, :, :HEAD_DIM].transpose(0, 2, 1), (1, N_KV_HEADS, 1))
    rope_m = _rope_table(jnp.arange(N_META, dtype=jnp.int32))

    xp = x_prompt.reshape(n_batch * seq, D_MODEL)
    xs = x_sample.reshape(n_dec * t_new, D_MODEL)

    q_p, k_p, v_p, u_p = _inproj_prompt(xp, g_mix, w_in_bf, rope_p, tm_mlp)
    q_s, kt_s, vt_s, u_s = _inproj_sample(xs, g_mix, w_in_bf, rope_s, ropet_s, tm)
    _, k_m, v_m, u_m = _inproj_prompt(meta_tokens.astype(F32), g_mix, w_in_bf, rope_m, N_META)

    pad_rows = jnp.zeros((BLOCK - N_META, KV_WIDTH), F32)
    k_meta_blk = jnp.concatenate([pad_rows, k_m], axis=0)
    v_meta_blk = jnp.concatenate([pad_rows, v_m], axis=0)
    attn_p = _attn_prompt(sinks, q_p, k_p, v_p, k_meta_blk, v_meta_blk, n_batch, n_blk)
    meta_halo = jnp.concatenate([jnp.zeros((N_SLAB, CONV_HALO - N_META, LANES), F32), u_m], axis=1)
    conv_p = _conv_prompt(u_p, meta_halo, wb_slabs, bdw, ln_g, ln_b, n_batch, seq)

    ckt = cache_k[l].transpose(0, 2, 3, 1).reshape(n_dec, KV_WIDTH, window)
    cvt = cache_v[l].transpose(0, 2, 3, 1).reshape(n_dec, KV_WIDTH, window)
    sink_rows = jnp.broadcast_to(jnp.tile(jnp.repeat(sinks, t_new), bt)[:, None], (bt * N_HEADS * t_new, LANES))
    attn_s, kt_out, vt_out = _attn_sample(sink_rows, q_s, kt_s, vt_s, ckt, cvt, t_new, bt)
    state_t = state_conv[l].transpose(1, 0, 2)
    u_t = u_s.reshape(n_dec, t_new, CONV_CH).transpose(1, 0, 2)
    conv_t, state_out_t = _conv_sample(state_t, u_t, wb_rows, bdw, ln_g, ln_b, 32)
    conv_s = conv_t.transpose(1, 0, 2).reshape(n_dec * t_new, CONV_CH)

    h_p = _outproj(xp, attn_p, conv_p, w_out_bf, tm, True)
    h_s = _outproj(xs, attn_s, conv_s, w_out_bf, tm, False)
    y_p = _mlp(h_p, g_mlp, w_up_bf, w_down_bf, g_fin, tm_mlp, tf)
    y_s = _mlp(h_s, g_mlp, w_up_bf, w_down_bf, g_fin, tm_mlp, tf)

    y_prompt = y_p.reshape(n_batch, seq, D_MODEL)
    y_sample = y_s.reshape(n_dec, t_new, D_MODEL)
    k_last = k_p.reshape(n_batch, seq, KV_WIDTH)[:, seq - WINDOW:]
    v_last = v_p.reshape(n_batch, seq, KV_WIDTH)[:, seq - WINDOW:]
    new_k_prompt = k_last.reshape(n_batch, WINDOW, N_KV_HEADS, HEAD_DIM)[None]
    new_v_prompt = v_last.reshape(n_batch, WINDOW, N_KV_HEADS, HEAD_DIM)[None]
    u_last = u_p.reshape(N_SLAB, n_batch, seq, LANES)[:, :, seq - (CONV_K - 1):]
    new_conv_prompt = u_last.transpose(1, 2, 0, 3).reshape(n_batch, CONV_K - 1, CONV_CH)[None]
    new_k_sample = kt_out.reshape(n_dec, N_KV_HEADS, HEAD_DIM, window).transpose(0, 3, 1, 2)[None]
    new_v_sample = vt_out.reshape(n_dec, N_KV_HEADS, HEAD_DIM, window).transpose(0, 3, 1, 2)[None]
    new_conv_sample = state_out_t.transpose(1, 0, 2)[None]
    return (y_prompt, y_sample, new_k_prompt, new_v_prompt, new_conv_prompt,
            new_k_sample, new_v_sample, new_conv_sample)
```

```python
import functools

import jax
import jax.numpy as jnp
from jax import lax
from jax.experimental import pallas as pl
from jax.experimental.pallas import tpu as pltpu

D_MODEL = 2048
N_META = 16
HEAD_DIM = 64
ATTN_WIDTH = D_MODEL // 2
CONV_CH = D_MODEL - ATTN_WIDTH
N_HEADS = ATTN_WIDTH // HEAD_DIM
N_KV_HEADS = 4
GQA_GROUP = N_HEADS // N_KV_HEADS
KV_WIDTH = N_KV_HEADS * HEAD_DIM
IN_COLS = ATTN_WIDTH + 2 * KV_WIDTH + 2 * CONV_CH
WINDOW = 128
BLOCK = 128
ROPE_THETA = 500000.0
ROT_DIM = HEAD_DIM // 4
CONV_K = 31
D_FF = 4 * D_MODEL
RMS_EPS = 1e-6
LN_EPS = 1e-5
MASK_VALUE = -1e30
PAST_LEN = 8192

LANES = 128
SUBLANES = 8
VMEM_LIMIT = 56 * 1024 * 1024
VMEM_LIMIT_MLP = 62 * 1024 * 1024
N_SLAB = CONV_CH // LANES

F32 = jnp.float32
BF16 = jnp.bfloat16

K_OFF = ATTN_WIDTH
V_OFF = ATTN_WIDTH + KV_WIDTH
A_OFF = ATTN_WIDTH + 2 * KV_WIDTH
B_OFF = A_OFF + CONV_CH
NT_DIMS = (((1,), (1,)), ((), ()))
LOG2_E = 1.4426950408889634
Q_SCALE = HEAD_DIM ** -0.5 * LOG2_E


def _params(*sem):
    return pltpu.CompilerParams(dimension_semantics=sem, vmem_limit_bytes=VMEM_LIMIT)


def _const_spec(shape):
    nd = len(shape)
    return pl.BlockSpec(shape, lambda *_: (0,) * nd, pipeline_mode=pl.Buffered(1))


def _rms(x, g):
    ms = jnp.mean(x * x, axis=-1, keepdims=True)
    return (x * lax.rsqrt(ms + RMS_EPS)) * g


def _rope_rows(z, rope_ref):
    return (z * rope_ref[0] + pltpu.roll(z, LANES - ROT_DIM // 2, 1) * rope_ref[1]
            + pltpu.roll(z, ROT_DIM // 2, 1) * rope_ref[2])


def _proj(xn_ref, w_ref, off, width):
    return jnp.dot(xn_ref[...], w_ref[:, off:off + width], preferred_element_type=F32)


def _write_q(xn_ref, w_ref, rope_ref, q_ref):
    half = ATTN_WIDTH // 2
    for c in range(2):
        acc = _proj(xn_ref, w_ref, c * half, half)
        for cc in range(half // LANES):
            z = _rope_rows(acc[:, cc * LANES:(cc + 1) * LANES], rope_ref) * Q_SCALE
            q_ref[:, c * half + cc * LANES:c * half + (cc + 1) * LANES] = z.astype(q_ref.dtype)


def _glu(xn_ref, w_ref, c):
    half = CONV_CH // 2
    a = _proj(xn_ref, w_ref, A_OFF + c * half, half)
    b = _proj(xn_ref, w_ref, B_OFF + c * half, half)
    return a * jax.nn.sigmoid(b)


def _inproj_prompt_kernel(x_ref, g_ref, w_ref, rope_ref, q_ref, k_ref, v_ref, u_ref, xn_ref):
    xn_ref[...] = _rms(x_ref[...], g_ref[...]).astype(BF16)
    _write_q(xn_ref, w_ref, rope_ref, q_ref)
    acc = _proj(xn_ref, w_ref, K_OFF, 2 * KV_WIDTH)
    for cc in range(KV_WIDTH // LANES):
        k_ref[:, cc * LANES:(cc + 1) * LANES] = _rope_rows(acc[:, cc * LANES:(cc + 1) * LANES], rope_ref)
    v_ref[...] = acc[:, KV_WIDTH:]
    per = N_SLAB // 2
    for c in range(2):
        u = _glu(xn_ref, w_ref, c)
        for cc in range(per):
            u_ref[c * per + cc] = u[:, cc * LANES:(cc + 1) * LANES]


def _inproj_prompt(x, g, w_bf, rope_tab, tm):
    m = x.shape[0]
    nrt = rope_tab.shape[1] // tm
    return pl.pallas_call(
        _inproj_prompt_kernel,
        grid=(m // tm,),
        in_specs=[
            pl.BlockSpec((tm, D_MODEL), lambda i: (i, 0)),
            _const_spec((1, D_MODEL)),
            _const_spec((D_MODEL, IN_COLS)),
            pl.BlockSpec((3, tm, LANES), lambda i: (0, i % nrt, 0)),
        ],
        out_specs=[
            pl.BlockSpec((tm, ATTN_WIDTH), lambda i: (i, 0)),
            pl.BlockSpec((tm, KV_WIDTH), lambda i: (i, 0)),
            pl.BlockSpec((tm, KV_WIDTH), lambda i: (i, 0)),
            pl.BlockSpec((N_SLAB, tm, LANES), lambda i: (0, i, 0)),
        ],
        out_shape=[
            jax.ShapeDtypeStruct((m, ATTN_WIDTH), BF16),
            jax.ShapeDtypeStruct((m, KV_WIDTH), F32),
            jax.ShapeDtypeStruct((m, KV_WIDTH), F32),
            jax.ShapeDtypeStruct((N_SLAB, m, LANES), F32),
        ],
        scratch_shapes=[pltpu.VMEM((tm, D_MODEL), BF16)],
        compiler_params=_params("parallel"),
        name="inproj_prompt",
    )(x, g, w_bf, rope_tab)


def _inproj_sample_kernel(x_ref, g_ref, w_ref, rope_ref, ropet_ref, q_ref, kt_ref, vt_ref, u_ref, xn_ref):
    xn_ref[...] = _rms(x_ref[...], g_ref[...]).astype(BF16)
    _write_q(xn_ref, w_ref, rope_ref, q_ref)
    kvt = _proj(xn_ref, w_ref, K_OFF, 2 * KV_WIDTH).T
    kt = kvt[0:KV_WIDTH]
    sh = ROT_DIM // 2
    up = jnp.concatenate([kt[sh:], kt[:sh]], axis=0)
    dn = jnp.concatenate([kt[-sh:], kt[:-sh]], axis=0)
    kt_ref[...] = kt * ropet_ref[0] + up * ropet_ref[1] + dn * ropet_ref[2]
    vt_ref[...] = kvt[KV_WIDTH:]
    half = CONV_CH // 2
    for c in range(2):
        u_ref[:, c * half:(c + 1) * half] = _glu(xn_ref, w_ref, c)


def _inproj_sample(x, g, w_bf, rope_tab, ropet_tab, tm):
    m = x.shape[0]
    return pl.pallas_call(
        _inproj_sample_kernel,
        grid=(m // tm,),
        in_specs=[
            pl.BlockSpec((tm, D_MODEL), lambda i: (i, 0)),
            _const_spec((1, D_MODEL)),
            _const_spec((D_MODEL, IN_COLS)),
            _const_spec((3, tm, LANES)),
            _const_spec((3, KV_WIDTH, tm)),
        ],
        out_specs=[
            pl.BlockSpec((tm, ATTN_WIDTH), lambda i: (i, 0)),
            pl.BlockSpec((KV_WIDTH, tm), lambda i: (0, i)),
            pl.BlockSpec((KV_WIDTH, tm), lambda i: (0, i)),
            pl.BlockSpec((tm, CONV_CH), lambda i: (i, 0)),
        ],
        out_shape=[
            jax.ShapeDtypeStruct((m, ATTN_WIDTH), F32),
            jax.ShapeDtypeStruct((KV_WIDTH, m), F32),
            jax.ShapeDtypeStruct((KV_WIDTH, m), F32),
            jax.ShapeDtypeStruct((m, CONV_CH), F32),
        ],
        scratch_shapes=[pltpu.VMEM((tm, D_MODEL), BF16)],
        compiler_params=_params("parallel"),
        name="inproj_sample",
    )(x, g, w_bf, rope_tab, ropet_tab)


def _rope_table(pos):
    half = ROT_DIM // 2
    inv = jnp.power(jnp.float32(ROPE_THETA), -jnp.arange(half, dtype=F32) * 2.0 / ROT_DIM)
    ang = pos.astype(F32)[:, None] * inv[None, :]
    cos, sin = jnp.cos(ang), jnp.sin(ang)
    t = pos.shape[0]
    rest = HEAD_DIM - ROT_DIM
    one = jnp.ones((t, rest), F32)
    zero = jnp.zeros((t, rest), F32)
    zh = jnp.zeros((t, half), F32)
    c = jnp.concatenate([cos, cos, one], axis=1)
    s_lo = jnp.concatenate([-sin, zh, zero], axis=1)
    s_hi = jnp.concatenate([zh, sin, zero], axis=1)
    tab = jnp.stack([c, s_lo, s_hi])
    return jnp.tile(tab, (1, 1, LANES // HEAD_DIM))


def _attn_prompt_kernel(sink_ref, q_ref, kc_ref, kp_ref, vc_ref, vp_ref, km_ref, vm_ref, o_ref,
                        kk_ref, vv_ref, s_ref, p_ref, inv_ref):
    n = pl.program_id(1)

    @pl.when(n == 0)
    def _():
        kk_ref[0:BLOCK] = km_ref[...]
        vv_ref[0:BLOCK] = vm_ref[...]

    @pl.when(n > 0)
    def _():
        kk_ref[0:BLOCK] = kp_ref[...]
        vv_ref[0:BLOCK] = vp_ref[...]

    kk_ref[BLOCK:2 * BLOCK] = kc_ref[...]
    vv_ref[BLOCK:2 * BLOCK] = vc_ref[...]
    vv_ref[0:1, :] = jnp.zeros((1, KV_WIDTH), F32)

    lo = jnp.where(n == 0, BLOCK - N_META, 0)
    r = lax.broadcasted_iota(jnp.int32, (BLOCK, 2 * BLOCK), 0)
    j = lax.broadcasted_iota(jnp.int32, (BLOCK, 2 * BLOCK), 1)
    mask3 = ((j > r) & (j <= r + BLOCK) & (j >= lo)).reshape(BLOCK // SUBLANES, SUBLANES, 2 * BLOCK)
    col0 = lax.broadcasted_iota(jnp.int32, (1, SUBLANES, 2 * BLOCK), 2) == 0
    low_half = lax.broadcasted_iota(jnp.int32, (BLOCK, LANES), 1) < HEAD_DIM
    low_half2 = lax.broadcasted_iota(jnp.int32, (2 * BLOCK, LANES), 1) < HEAD_DIM

    def dup_halves(ref, h):
        xf = ref[:, (h // 2) * LANES:(h // 2 + 1) * LANES]
        x_sw = pltpu.roll(xf, HEAD_DIM, 1)
        both = jnp.where(low_half2, xf, x_sw) if h % 2 == 0 else jnp.where(low_half2, x_sw, xf)
        return both.astype(BF16)

    rows_kv = GQA_GROUP * BLOCK
    for h in range(N_KV_HEADS):
        qa = q_ref[:, (2 * h) * LANES:(2 * h + 1) * LANES]
        qb = q_ref[:, (2 * h + 1) * LANES:(2 * h + 2) * LANES]
        zero = jnp.zeros_like(qa)
        lhs = jnp.concatenate([jnp.where(low_half, qa, zero), jnp.where(low_half, qb, zero),
                               jnp.where(low_half, zero, qa), jnp.where(low_half, zero, qb)], axis=0)
        s_all = lax.dot_general(lhs, dup_halves(kk_ref, h), NT_DIMS, preferred_element_type=F32)
        heads = (4 * h, 4 * h + 2, 4 * h + 1, 4 * h + 3)
        for g in range(GQA_GROUP):
            fill = jnp.where(col0, sink_ref[heads[g]], MASK_VALUE)
            s = jnp.where(mask3, s_all[g * BLOCK:(g + 1) * BLOCK].reshape(mask3.shape), fill)
            s_ref[h * rows_kv + g * BLOCK:h * rows_kv + (g + 1) * BLOCK] = s.reshape(BLOCK, 2 * BLOCK)

    s = s_ref[...]
    mx = jnp.max(s, axis=-1, keepdims=True)
    e = jnp.exp2(s - mx)
    p_ref[...] = e.astype(BF16)
    inv_ref[...] = jnp.broadcast_to(1.0 / jnp.sum(e, axis=-1, keepdims=True), inv_ref.shape)

    for h in range(N_KV_HEADS):
        base = h * rows_kv
        o_all = jnp.dot(p_ref[base:base + rows_kv], dup_halves(vv_ref, h), preferred_element_type=F32)
        o = [o_all[g * BLOCK:(g + 1) * BLOCK] * inv_ref[base + g * BLOCK:base + (g + 1) * BLOCK]
             for g in range(GQA_GROUP)]
        o_ref[:, (2 * h) * LANES:(2 * h + 1) * LANES] = jnp.where(low_half, o[0], o[2]).astype(o_ref.dtype)
        o_ref[:, (2 * h + 1) * LANES:(2 * h + 2) * LANES] = jnp.where(low_half, o[1], o[3]).astype(o_ref.dtype)


def _attn_prompt(sinks, q, k, v, k_meta_blk, v_meta_blk, n_batch, n_blk):
    kv_cur = pl.BlockSpec((BLOCK, KV_WIDTH), lambda b, n: (b * n_blk + n, 0))
    kv_prev = pl.BlockSpec((BLOCK, KV_WIDTH), lambda b, n: (b * n_blk + jnp.maximum(n - 1, 0), 0))
    return pl.pallas_call(
        _attn_prompt_kernel,
        grid=(n_batch, n_blk),
        in_specs=[
            pl.BlockSpec(memory_space=pltpu.SMEM),
            pl.BlockSpec((BLOCK, ATTN_WIDTH), lambda b, n: (b * n_blk + n, 0)),
            kv_cur, kv_prev, kv_cur, kv_prev,
            _const_spec((BLOCK, KV_WIDTH)),
            _const_spec((BLOCK, KV_WIDTH)),
        ],
        out_specs=pl.BlockSpec((BLOCK, ATTN_WIDTH), lambda b, n: (b * n_blk + n, 0)),
        out_shape=jax.ShapeDtypeStruct((n_batch * n_blk * BLOCK, ATTN_WIDTH), BF16),
        scratch_shapes=[
            pltpu.VMEM((2 * BLOCK, KV_WIDTH), F32),
            pltpu.VMEM((2 * BLOCK, KV_WIDTH), F32),
            pltpu.VMEM((N_HEADS * BLOCK, 2 * BLOCK), F32),
            pltpu.VMEM((N_HEADS * BLOCK, 2 * BLOCK), BF16),
            pltpu.VMEM((N_HEADS * BLOCK, LANES), F32),
        ],
        compiler_params=_params("parallel", "arbitrary"),
        name="attn_prompt",
    )(sinks, q, k, k, v, v, k_meta_blk, v_meta_blk)


def _attn_sample_kernel(t_new, bt, sink_ref, q_ref, kn_ref, vn_ref, ck_ref, cv_ref,
                        o_ref, ko_ref, vo_ref, s_ref, p_ref, inv_ref, osc_ref):
    nrow = N_HEADS * t_new
    w = ck_ref.shape[2]
    assert bt * t_new == LANES and w == LANES and nrow == LANES
    zero_blk = jnp.zeros((t_new, LANES), F32)

    r = lax.broadcasted_iota(jnp.int32, (nrow, 2 * w), 0) % t_new
    j = lax.broadcasted_iota(jnp.int32, (nrow, 2 * w), 1)
    cache_ok = (j > r) & (j < w)
    fill = jnp.where(j == 0, sink_ref[...][:, 0:1], MASK_VALUE)
    key0 = lax.broadcasted_iota(jnp.int32, (bt * nrow, w), 1) == 0
    low_half =lax.broadcasted_iota(jnp.int32, (t_new, LANES), 1) < HEAD_DIM
    tail = lax.broadcasted_iota(jnp.int32, (KV_WIDTH, LANES), 1) >= w - t_new
    kn = kn_ref[...]
    vn = vn_ref[...]
    kn_bf = kn.astype(BF16)
    vn_bf = vn.astype(BF16)

    for i in range(bt):
        new_shift = (w - t_new - i * t_new) % LANES
        ko_ref[i] = jnp.where(tail, pltpu.roll(kn, new_shift, 1), pltpu.roll(ck_ref[i], w - t_new, 1))
        vo_ref[i] = jnp.where(tail, pltpu.roll(vn, new_shift, 1), pltpu.roll(cv_ref[i], w - t_new, 1))

    for i in range(bt):
        qrows = []
        for t in range(N_HEADS):
            h = t // GQA_GROUP
            qv = q_ref[i * t_new:(i + 1) * t_new, (t // 2) * LANES:(t // 2 + 1) * LANES]
            if t % 2 != h % 2:
                qv = pltpu.roll(qv, HEAD_DIM, 1)
            keep = low_half if h % 2 == 0 else jnp.logical_not(low_half)
            cols = [zero_blk] * (KV_WIDTH // LANES)
            cols[h // 2] = jnp.where(keep, qv, 0.0)
            qrows.append(jnp.concatenate(cols, axis=1))
        qrow = jnp.concatenate(qrows, axis=0).astype(BF16)

        rhs_k = jnp.concatenate([ck_ref[i].astype(BF16), kn_bf], axis=1)
        s = jnp.dot(qrow, rhs_k, preferred_element_type=F32)
        own = w + i * t_new
        mask = cache_ok | ((j >= own) & (j <= own + r))
        s_ref[i * nrow:(i + 1) * nrow] = jnp.where(mask, s, fill)

    s = s_ref[...]
    mx = jnp.max(s, axis=-1, keepdims=True)
    e = jnp.exp2(s - mx)
    den = jnp.sum(e, axis=-1, keepdims=True)
    p_ref[:, 0:w] = jnp.where(key0, 0.0, e[:, 0:w]).astype(BF16)
    p_ref[:, w:2 * w] = e[:, w:2 * w].astype(BF16)
    inv_ref[...] = jnp.broadcast_to(1.0 / den, inv_ref.shape)

    for i in range(bt):
        rhs_v = jnp.concatenate([cv_ref[i].astype(BF16), vn_bf], axis=1)
        o_all = lax.dot_general(p_ref[i * nrow:(i + 1) * nrow], rhs_v, NT_DIMS, preferred_element_type=F32)
        for t2 in range(N_HEADS // 2):
            parts = []
            for side in range(2):
                t = 2 * t2 + side
                h = t // GQA_GROUP
                blk = o_all[t * t_new:(t + 1) * t_new, (h // 2) * LANES:(h // 2 + 1) * LANES]
                blk = blk * inv_ref[i * nrow + t * t_new:i * nrow + (t + 1) * t_new, :]
                if t % 2 != h % 2:
                    blk = pltpu.roll(blk, HEAD_DIM, 1)
                parts.append(blk)
            osc_ref[i * t_new:(i + 1) * t_new, t2 * LANES:(t2 + 1) * LANES] = (
                jnp.where(low_half, parts[0], parts[1]))

    o_ref[...] = osc_ref[...].astype(o_ref.dtype)


def _attn_sample(sink_rows, q, kt_new, vt_new, cache_kt, cache_vt, t_new, bt):
    nb, _, w = cache_kt.shape
    rows = bt * t_new
    cache_spec = pl.BlockSpec((bt, KV_WIDTH, w), lambda s: (s, 0, 0))
    new_spec = pl.BlockSpec((KV_WIDTH, rows), lambda s: (0, s))
    return pl.pallas_call(
        functools.partial(_attn_sample_kernel, t_new, bt),
        grid=(nb // bt,),
        in_specs=[
            _const_spec(sink_rows.shape),
            pl.BlockSpec((rows, ATTN_WIDTH), lambda s: (s, 0)),
            new_spec, new_spec, cache_spec, cache_spec,
        ],
        out_specs=[pl.BlockSpec((rows, ATTN_WIDTH), lambda s: (s, 0)), cache_spec, cache_spec],
        out_shape=[
            jax.ShapeDtypeStruct((nb * t_new, ATTN_WIDTH), BF16),
            jax.ShapeDtypeStruct(cache_kt.shape, F32),
            jax.ShapeDtypeStruct(cache_vt.shape, F32),
        ],
        scratch_shapes=[
            pltpu.VMEM((bt * N_HEADS * t_new, 2 * w), F32),
            pltpu.VMEM((bt * N_HEADS * t_new, 2 * w), BF16),
            pltpu.VMEM((bt * N_HEADS * t_new, LANES), F32),
            pltpu.VMEM((rows, ATTN_WIDTH), F32),
        ],
        compiler_params=_params("parallel"),
        name="attn_sample",
    )(sink_rows, q, kt_new, vt_new, cache_kt, cache_vt)


def _ln_swish(y, g, b):
    mu = jnp.mean(y, axis=-1, keepdims=True)
    d = y - mu
    var = jnp.mean(d * d, axis=-1, keepdims=True)
    yn = d * lax.rsqrt(var + LN_EPS) * g + b
    return yn * jax.nn.sigmoid(yn)


CONV_HALO = 32
CONV_SEG = 252
CONV_SEG_STEP = 12
CONV_FIN_STEP = 36


def _conv_prompt_kernel(seq, u_ref, meta_ref, wb_ref, bdw_ref, g_ref, b_ref, o_ref, win_ref, y_ref):
    for c in range(N_SLAB):
        win_ref[c, 0:CONV_HALO] = meta_ref[c]
        win_ref[c, CONV_HALO:CONV_HALO + seq] = u_ref[c]
    first = CONV_HALO - (CONV_K - 1)

    rem_base = SUBLANES * CONV_SEG
    rem_seg = (seq - rem_base) // SUBLANES
    n_main = CONV_SEG // CONV_SEG_STEP

    def taps(c, wts, base, seg, n_g, i0, y0):
        accs = [None] * n_g
        for m in range(n_g + CONV_K - 1):
            v = win_ref[c, pl.ds(base + first + i0 + m, SUBLANES, stride=seg), :]
            for g in range(max(0, m - CONV_K + 1), min(n_g, m + 1)):
                term = v * wts[m - g]
                accs[g] = term if accs[g] is None else accs[g] + term
        for g in range(n_g):
            y_ref[y0 + g, :, c * LANES:(c + 1) * LANES] = accs[g]

    for c in range(N_SLAB):
        wts = [wb_ref[tap, c] for tap in range(CONV_K)]

        def body(ci, carry, c=c, wts=wts):
            i0 = ci * CONV_SEG_STEP
            taps(c, wts, 0, CONV_SEG, CONV_SEG_STEP, i0, i0)
            return carry

        lax.fori_loop(0, n_main, body, 0)
        taps(c, wts, rem_base, rem_seg, rem_seg, 0, CONV_SEG)

    def finish(base, seg, n_g, i0, y0):
        z = _ln_swish(y_ref[pl.ds(y0, n_g)] + bdw_ref[...], g_ref[...], b_ref[...])
        for c in range(N_SLAB):
            for g in range(n_g):
                o_ref[c, pl.ds(base + i0 + g, SUBLANES, stride=seg), :] = z[g, :, c * LANES:(c + 1) * LANES]

    def fin_body(ci, carry):
        i0 = ci * CONV_FIN_STEP
        finish(0, CONV_SEG, CONV_FIN_STEP, i0, i0)
        return carry

    lax.fori_loop(0, CONV_SEG // CONV_FIN_STEP, fin_body, 0)
    finish(rem_base, rem_seg, rem_seg, 0, CONV_SEG)


def _conv_prompt(u_slabs, meta_halo, wb, b_dw, ln_g, ln_b, n_batch, seq):
    assert CONV_SEG % CONV_SEG_STEP == 0 and CONV_SEG % CONV_FIN_STEP == 0 and seq > SUBLANES * CONV_SEG
    assert (seq - SUBLANES * CONV_SEG) % SUBLANES == 0
    blk = pl.BlockSpec((N_SLAB, seq, LANES), lambda b: (0, b, 0))
    return pl.pallas_call(
        functools.partial(_conv_prompt_kernel, seq),
        grid=(n_batch,),
        in_specs=[
            blk,
            _const_spec((N_SLAB, CONV_HALO, LANES)),
            _const_spec((CONV_K, N_SLAB, SUBLANES, LANES)),
            _const_spec((1, CONV_CH)),
            _const_spec((1, CONV_CH)),
            _const_spec((1, CONV_CH)),
        ],
        out_specs=blk,
        out_shape=jax.ShapeDtypeStruct((N_SLAB, n_batch * seq, LANES), F32),
        scratch_shapes=[
            pltpu.VMEM((N_SLAB, CONV_HALO + seq, LANES), F32),
            pltpu.VMEM((seq // SUBLANES, SUBLANES, CONV_CH), F32),
        ],
        compiler_params=_params("parallel"),
        name="conv_prompt",
    )(u_slabs, meta_halo, wb, b_dw, ln_g, ln_b)


def _conv_sample_kernel(t_new, st_ref, u_ref, wb_ref, bdw_ref, g_ref, b_ref, o_ref, so_ref):
    hist = st_ref.shape[0]
    bb = st_ref.shape[1]
    so_ref[0:hist - t_new] = st_ref[t_new:hist]
    so_ref[hist - t_new:hist] = u_ref[...]

    def src(k, rows):
        return st_ref[k, rows, :] if k < hist else u_ref[k - hist, rows, :]

    def body(rc, carry):
        rows = pl.ds(pl.multiple_of(rc * SUBLANES, SUBLANES), SUBLANES)
        for t in range(t_new):
            acc = jnp.zeros((SUBLANES, CONV_CH), F32)
            for tap in range(CONV_K):
                acc = acc + src(t + tap, rows) * wb_ref[tap]
            o_ref[t, rows, :] = _ln_swish(acc + bdw_ref[...], g_ref[...], b_ref[...]).astype(o_ref.dtype)
        return carry

    lax.fori_loop(0, bb // SUBLANES, body, 0)


def _conv_sample(state_t, u_t, wb, b_dw, ln_g, ln_b, bb):
    hist, nb, _ = state_t.shape
    t_new = u_t.shape[0]
    return pl.pallas_call(
        functools.partial(_conv_sample_kernel, t_new),
        grid=(nb // bb,),
        in_specs=[
            pl.BlockSpec((hist, bb, CONV_CH), lambda s: (0, s, 0)),
            pl.BlockSpec((t_new, bb, CONV_CH), lambda s: (0, s, 0)),
            _const_spec((CONV_K, SUBLANES, CONV_CH)),
            _const_spec((1, CONV_CH)),
            _const_spec((1, CONV_CH)),
            _const_spec((1, CONV_CH)),
        ],
        out_specs=[
            pl.BlockSpec((t_new, bb, CONV_CH), lambda s: (0, s, 0)),
            pl.BlockSpec((hist, bb, CONV_CH), lambda s: (0, s, 0)),
        ],
        out_shape=[
            jax.ShapeDtypeStruct((t_new, nb, CONV_CH), F32),
            jax.ShapeDtypeStruct(state_t.shape, F32),
        ],
        compiler_params=_params("parallel"),
        name="conv_sample",
    )(state_t, u_t, wb, b_dw, ln_g, ln_b)


def _outproj_kernel(conv_slabs, x_ref, a_ref, c_ref, wo_ref, h_ref, cbf_ref):
    if conv_slabs:
        for c in range(N_SLAB):
            cbf_ref[:, c * LANES:(c + 1) * LANES] = c_ref[c].astype(BF16)
    else:
        cbf_ref[...] = c_ref[...].astype(BF16)
    mix = jnp.dot(a_ref[...], wo_ref[0:ATTN_WIDTH], preferred_element_type=F32)
    mix = mix + jnp.dot(cbf_ref[...], wo_ref[ATTN_WIDTH:], preferred_element_type=F32)
    h_ref[...] = x_ref[...] + mix


def _outproj(x, attn, conv, wo_bf, tm, conv_slabs):
    m = x.shape[0]
    if conv_slabs:
        conv_spec = pl.BlockSpec((N_SLAB, tm, LANES), lambda i: (0, i, 0))
    else:
        conv_spec = pl.BlockSpec((tm, CONV_CH), lambda i: (i, 0))
    return pl.pallas_call(
        functools.partial(_outproj_kernel, conv_slabs),
        grid=(m // tm,),
        in_specs=[
            pl.BlockSpec((tm, D_MODEL), lambda i: (i, 0)),
            pl.BlockSpec((tm, ATTN_WIDTH), lambda i: (i, 0)),
            conv_spec,
            _const_spec((D_MODEL, D_MODEL)),
        ],
        out_specs=pl.BlockSpec((tm, D_MODEL), lambda i: (i, 0)),
        out_shape=jax.ShapeDtypeStruct((m, D_MODEL), F32),
        scratch_shapes=[pltpu.VMEM((tm, CONV_CH), BF16)],
        compiler_params=_params("parallel"),
        name="outproj",
    )(x, attn, conv, wo_bf)


def _mlp_kernel(h_ref, g2_ref, wu_ref, wd_ref, gf_ref, y_ref, xn_ref):
    f = pl.program_id(1)

    @pl.when(f == 0)
    def _():
        h1 = h_ref[...]
        y_ref[...] = h1
        xn_ref[...] = _rms(h1, g2_ref[...]).astype(BF16)

    up = jnp.dot(xn_ref[...], wu_ref[...].astype(BF16), preferred_element_type=F32)
    act = jnp.maximum(up, 0.0)
    act = (act * act).astype(BF16)
    y_ref[...] += jnp.dot(act, wd_ref[...].astype(BF16), preferred_element_type=F32)

    @pl.when(f == pl.num_programs(1) - 1)
    def _():
        y_ref[...] = _rms(y_ref[...], gf_ref[...])


def _mlp(h1, g2, wu_bf, wd_bf, gf, tm, tf):
    m = h1.shape[0]
    return pl.pallas_call(
        _mlp_kernel,
        grid=(m // tm, D_FF // tf),
        in_specs=[
            pl.BlockSpec((tm, D_MODEL), lambda i, f: (i, 0)),
            _const_spec((1, D_MODEL)),
            pl.BlockSpec((D_MODEL, tf), lambda i, f: (0, f)),
            pl.BlockSpec((tf, D_MODEL), lambda i, f: (f, 0)),
            _const_spec((1, D_MODEL)),
        ],
        out_specs=pl.BlockSpec((tm, D_MODEL), lambda i, f: (i, 0)),
        out_shape=jax.ShapeDtypeStruct((m, D_MODEL), F32),
        scratch_shapes=[pltpu.VMEM((tm, D_MODEL), BF16)],
        compiler_params=pltpu.CompilerParams(dimension_semantics=("parallel", "arbitrary"),
                                             vmem_limit_bytes=VMEM_LIMIT_MLP),
        name="mlp",
    )(h1, g2, wu_bf, wd_bf, gf)


def kernel(x_prompt, x_sample, cache_k, cache_v, state_conv, meta_tokens, norm_mix, w_in, attn_sinks,
           w_dw, b_dw, conv_ln_g, conv_ln_b, w_out, norm_mlp, w_up, w_down, norm_final):
    n_batch, seq, _ = x_prompt.shape
    n_dec, t_new, _ = x_sample.shape
    depth = w_in.shape[0]
    window = cache_k.shape[2]
    assert depth == 1 and seq % BLOCK == 0 and t_new == SUBLANES and window == WINDOW
    l = 0
    n_blk = seq // BLOCK
    tm = 512
    tm_mlp = 1024
    tf = 512
    bt = LANES // t_new

    w_in_bf = w_in[l].astype(BF16)
    w_out_bf = w_out[l].astype(BF16)
    g_mix = norm_mix[l][None, :]
    g_mlp = norm_mlp[l][None, :]
    g_fin = norm_final[None, :]
    sinks = attn_sinks[l].astype(F32) * LOG2_E
    bdw = b_dw[l][None, :]
    ln_g = conv_ln_g[l][None, :]
    ln_b = conv_ln_b[l][None, :]
    wb_rows = jnp.broadcast_to(w_dw[l][:, None, :], (CONV_K, SUBLANES, CONV_CH))
    wb_slabs = wb_rows.reshape(CONV_K, SUBLANES, N_SLAB, LANES).transpose(0, 2, 1, 3)

    rope_p = _rope_table(N_META + jnp.arange(seq, dtype=jnp.int32))
    rope_s = _rope_table(PAST_LEN + (jnp.arange(tm, dtype=jnp.int32) % t_new))
    ropet_s = jnp.tile(rope_s[:, :, :HEAD_DIM].transpose(0, 2, 1), (1, N_KV_HEADS, 1))
    rope_m = _rope_table(jnp.arange(N_META, dtype=jnp.int32))

    xp = x_prompt.reshape(n_batch * seq, D_MODEL)
    xs = x_sample.reshape(n_dec * t_new, D_MODEL)

    q_p, k_p, v_p, u_p = _inproj_prompt(xp, g_mix, w_in_bf, rope_p, tm_mlp)
    q_s, kt_s, vt_s, u_s = _inproj_sample(xs, g_mix, w_in_bf, rope_s, ropet_s, tm)
    _, k_m, v_m, u_m = _inproj_prompt(meta_tokens.astype(F32), g_mix, w_in_bf, rope_m, N_META)

    pad_rows = jnp.zeros((BLOCK - N_META, KV_WIDTH), F32)
    k_meta_blk = jnp.concatenate([pad_rows, k_m], axis=0)
    v_meta_blk = jnp.concatenate([pad_rows, v_m], axis=0)
    attn_p = _attn_prompt(sinks, q_p, k_p, v_p, k_meta_blk, v_meta_blk, n_batch, n_blk)
    meta_halo = jnp.concatenate([jnp.zeros((N_SLAB, CONV_HALO - N_META, LANES), F32), u_m], axis=1)
    conv_p = _conv_prompt(u_p, meta_halo, wb_slabs, bdw, ln_g, ln_b, n_batch, seq)

    ckt = cache_k[l].transpose(0, 2, 3, 1).reshape(n_dec, KV_WIDTH, window)
    cvt = cache_v[l].transpose(0, 2, 3, 1).reshape(n_dec, KV_WIDTH, window)
    sink_rows = jnp.broadcast_to(jnp.repeat(sinks, t_new)[:, None], (N_HEADS * t_new, LANES))
    attn_s, kt_out, vt_out = _attn_sample(sink_rows, q_s, kt_s, vt_s, ckt, cvt, t_new, bt)
    state_t = state_conv[l].transpose(1, 0, 2)
    u_t = u_s.reshape(n_dec, t_new, CONV_CH).transpose(1, 0, 2)
    conv_t, state_out_t = _conv_sample(state_t, u_t, wb_rows, bdw, ln_g, ln_b, 32)
    conv_s = conv_t.transpose(1, 0, 2).reshape(n_dec * t_new, CONV_CH)

    h_p = _outproj(xp, attn_p, conv_p, w_out_bf, tm, True)
    h_s = _outproj(xs, attn_s, conv_s, w_out_bf, tm, False)
    y_p = _mlp(h_p, g_mlp, w_up[l], w_down[l], g_fin, tm_mlp, tf)
    y_s = _mlp(h_s, g_mlp, w_up[l], w_down[l], g_fin, tm_mlp, tf)

    y_prompt = y_p.reshape(n_batch, seq, D_MODEL)
    y_sample = y_s.reshape(n_dec, t_new, D_MODEL)
    k_last = k_p.reshape(n_batch, seq, KV_WIDTH)[:, seq - WINDOW:]
    v_last = v_p.reshape(n_batch, seq, KV_WIDTH)[:, seq - WINDOW:]
    new_k_prompt = k_last.reshape(n_batch, WINDOW, N_KV_HEADS, HEAD_DIM)[None]
    new_v_prompt = v_last.reshape(n_batch, WINDOW, N_KV_HEADS, HEAD_DIM)[None]
    u_last = u_p.reshape(N_SLAB, n_batch, seq, LANES)[:, :, seq - (CONV_K - 1):]
    new_conv_prompt = u_last.transpose(1, 2, 0, 3).reshape(n_batch, CONV_K - 1, CONV_CH)[None]
    new_k_sample = kt_out.reshape(n_dec, N_KV_HEADS, HEAD_DIM, window).transpose(0, 3, 1, 2)[None]
    new_v_sample = vt_out.reshape(n_dec, N_KV_HEADS, HEAD_DIM, window).transpose(0, 3, 1, 2)[None]
    new_conv_sample = state_out_t.transpose(1, 0, 2)[None]
    return (y_prompt, y_sample, new_k_prompt, new_v_prompt, new_conv_prompt,
            new_k_sample, new_v_sample, new_conv_sample)
```

```python
import functools

import jax
import jax.numpy as jnp
from jax import lax
from jax.experimental import pallas as pl
from jax.experimental.pallas import tpu as pltpu

D_MODEL = 2048
N_META = 16
HEAD_DIM = 64
ATTN_WIDTH = D_MODEL // 2
CONV_CH = D_MODEL - ATTN_WIDTH
N_HEADS = ATTN_WIDTH // HEAD_DIM
N_KV_HEADS = 4
GQA_GROUP = N_HEADS // N_KV_HEADS
KV_WIDTH = N_KV_HEADS * HEAD_DIM
IN_COLS = ATTN_WIDTH + 2 * KV_WIDTH + 2 * CONV_CH
WINDOW = 128
BLOCK = 128
ROPE_THETA = 500000.0
ROT_DIM = HEAD_DIM // 4
CONV_K = 31
D_FF = 4 * D_MODEL
RMS_EPS = 1e-6
LN_EPS = 1e-5
MASK_VALUE = -1e30
PAST_LEN = 8192

LANES = 128
SUBLANES = 8
VMEM_LIMIT = 56 * 1024 * 1024
VMEM_LIMIT_MLP = 62 * 1024 * 1024
N_SLAB = CONV_CH // LANES

F32 = jnp.float32
BF16 = jnp.bfloat16

K_OFF = ATTN_WIDTH
V_OFF = ATTN_WIDTH + KV_WIDTH
A_OFF = ATTN_WIDTH + 2 * KV_WIDTH
B_OFF = A_OFF + CONV_CH
NT_DIMS = (((1,), (1,)), ((), ()))
LOG2_E = 1.4426950408889634
Q_SCALE = HEAD_DIM ** -0.5 * LOG2_E


def _params(*sem):
    return pltpu.CompilerParams(dimension_semantics=sem, vmem_limit_bytes=VMEM_LIMIT)


def _const_spec(shape):
    nd = len(shape)
    return pl.BlockSpec(shape, lambda *_: (0,) * nd, pipeline_mode=pl.Buffered(1))


def _rms(x, g):
    ms = jnp.mean(x * x, axis=-1, keepdims=True)
    return (x * lax.rsqrt(ms + RMS_EPS)) * g


def _rope_rows(z, rope_ref):
    return (z * rope_ref[0] + pltpu.roll(z, LANES - ROT_DIM // 2, 1) * rope_ref[1]
            + pltpu.roll(z, ROT_DIM // 2, 1) * rope_ref[2])


def _proj(xn_ref, w_ref, off, width):
    return jnp.dot(xn_ref[...], w_ref[:, off:off + width], preferred_element_type=F32)


def _write_q(xn_ref, w_ref, rope_ref, q_ref):
    half = ATTN_WIDTH // 2
    for c in range(2):
        acc = _proj(xn_ref, w_ref, c * half, half)
        for cc in range(half // LANES):
            z = _rope_rows(acc[:, cc * LANES:(cc + 1) * LANES], rope_ref) * Q_SCALE
            q_ref[:, c * half + cc * LANES:c * half + (cc + 1) * LANES] = z.astype(q_ref.dtype)


def _glu(xn_ref, w_ref, c):
    half = CONV_CH // 2
    a = _proj(xn_ref, w_ref, A_OFF + c * half, half)
    b = _proj(xn_ref, w_ref, B_OFF + c * half, half)
    return a * jax.nn.sigmoid(b)


def _inproj_prompt_kernel(x_ref, g_ref, w_ref, rope_ref, q_ref, k_ref, v_ref, u_ref, xn_ref):
    xn_ref[...] = _rms(x_ref[...], g_ref[...]).astype(BF16)
    _write_q(xn_ref, w_ref, rope_ref, q_ref)
    acc = _proj(xn_ref, w_ref, K_OFF, 2 * KV_WIDTH)
    for cc in range(KV_WIDTH // LANES):
        k_ref[:, cc * LANES:(cc + 1) * LANES] = _rope_rows(acc[:, cc * LANES:(cc + 1) * LANES], rope_ref)
    v_ref[...] = acc[:, KV_WIDTH:]
    per = N_SLAB // 2
    for c in range(2):
        u = _glu(xn_ref, w_ref, c)
        for cc in range(per):
            u_ref[c * per + cc] = u[:, cc * LANES:(cc + 1) * LANES]


def _inproj_prompt(x, g, w_bf, rope_tab, tm):
    m = x.shape[0]
    nrt = rope_tab.shape[1] // tm
    return pl.pallas_call(
        _inproj_prompt_kernel,
        grid=(m // tm,),
        in_specs=[
            pl.BlockSpec((tm, D_MODEL), lambda i: (i, 0)),
            _const_spec((1, D_MODEL)),
            _const_spec((D_MODEL, IN_COLS)),
            pl.BlockSpec((3, tm, LANES), lambda i: (0, i % nrt, 0)),
        ],
        out_specs=[
            pl.BlockSpec((tm, ATTN_WIDTH), lambda i: (i, 0)),
            pl.BlockSpec((tm, KV_WIDTH), lambda i: (i, 0)),
            pl.BlockSpec((tm, KV_WIDTH), lambda i: (i, 0)),
            pl.BlockSpec((N_SLAB, tm, LANES), lambda i: (0, i, 0)),
        ],
        out_shape=[
            jax.ShapeDtypeStruct((m, ATTN_WIDTH), BF16),
            jax.ShapeDtypeStruct((m, KV_WIDTH), F32),
            jax.ShapeDtypeStruct((m, KV_WIDTH), F32),
            jax.ShapeDtypeStruct((N_SLAB, m, LANES), F32),
        ],
        scratch_shapes=[pltpu.VMEM((tm, D_MODEL), BF16)],
        compiler_params=_params("parallel"),
        name="inproj_prompt",
    )(x, g, w_bf, rope_tab)


def _inproj_sample_kernel(x_ref, g_ref, w_ref, rope_ref, ropet_ref, q_ref, kt_ref, vt_ref, u_ref, xn_ref):
    xn_ref[...] = _rms(x_ref[...], g_ref[...]).astype(BF16)
    _write_q(xn_ref, w_ref, rope_ref, q_ref)
    kvt = _proj(xn_ref, w_ref, K_OFF, 2 * KV_WIDTH).T
    kt = kvt[0:KV_WIDTH]
    sh = ROT_DIM // 2
    up = jnp.concatenate([kt[sh:], kt[:sh]], axis=0)
    dn = jnp.concatenate([kt[-sh:], kt[:-sh]], axis=0)
    kt_ref[...] = kt * ropet_ref[0] + up * ropet_ref[1] + dn * ropet_ref[2]
    vt_ref[...] = kvt[KV_WIDTH:]
    half = CONV_CH // 2
    for c in range(2):
        u_ref[:, c * half:(c + 1) * half] = _glu(xn_ref, w_ref, c)


def _inproj_sample(x, g, w_bf, rope_tab, ropet_tab, tm):
    m = x.shape[0]
    return pl.pallas_call(
        _inproj_sample_kernel,
        grid=(m // tm,),
        in_specs=[
            pl.BlockSpec((tm, D_MODEL), lambda i: (i, 0)),
            _const_spec((1, D_MODEL)),
            _const_spec((D_MODEL, IN_COLS)),
            _const_spec((3, tm, LANES)),
            _const_spec((3, KV_WIDTH, tm)),
        ],
        out_specs=[
            pl.BlockSpec((tm, ATTN_WIDTH), lambda i: (i, 0)),
            pl.BlockSpec((KV_WIDTH, tm), lambda i: (0, i)),
            pl.BlockSpec((KV_WIDTH, tm), lambda i: (0, i)),
            pl.BlockSpec((tm, CONV_CH), lambda i: (i, 0)),
        ],
        out_shape=[
            jax.ShapeDtypeStruct((m, ATTN_WIDTH), F32),
            jax.ShapeDtypeStruct((KV_WIDTH, m), F32),
            jax.ShapeDtypeStruct((KV_WIDTH, m), F32),
            jax.ShapeDtypeStruct((m, CONV_CH), F32),
        ],
        scratch_shapes=[pltpu.VMEM((tm, D_MODEL), BF16)],
        compiler_params=_params("parallel"),
        name="inproj_sample",
    )(x, g, w_bf, rope_tab, ropet_tab)


def _rope_table(pos):
    half = ROT_DIM // 2
    inv = jnp.power(jnp.float32(ROPE_THETA), -jnp.arange(half, dtype=F32) * 2.0 / ROT_DIM)
    ang = pos.astype(F32)[:, None] * inv[None, :]
    cos, sin = jnp.cos(ang), jnp.sin(ang)
    t = pos.shape[0]
    rest = HEAD_DIM - ROT_DIM
    one = jnp.ones((t, rest), F32)
    zero = jnp.zeros((t, rest), F32)
    zh = jnp.zeros((t, half), F32)
    c = jnp.concatenate([cos, cos, one], axis=1)
    s_lo = jnp.concatenate([-sin, zh, zero], axis=1)
    s_hi = jnp.concatenate([zh, sin, zero], axis=1)
    tab = jnp.stack([c, s_lo, s_hi])
    return jnp.tile(tab, (1, 1, LANES // HEAD_DIM))


def _attn_prompt_kernel(sink_ref, q_ref, kc_ref, kp_ref, vc_ref, vp_ref, km_ref, vm_ref, o_ref,
                        kk_ref, vv_ref, s_ref, p_ref, inv_ref):
    n = pl.program_id(1)

    @pl.when(n == 0)
    def _():
        pad = jnp.zeros((BLOCK - N_META, KV_WIDTH), F32)
        kk_ref[0:BLOCK - N_META] = pad
        vv_ref[0:BLOCK - N_META] = pad
        kk_ref[BLOCK - N_META:BLOCK] = km_ref[...]
        vv_ref[BLOCK - N_META:BLOCK] = vm_ref[...]

    @pl.when(n > 0)
    def _():
        kk_ref[0:BLOCK] = kp_ref[...]
        vv_ref[0:BLOCK] = vp_ref[...]

    kk_ref[BLOCK:2 * BLOCK] = kc_ref[...]
    vv_ref[BLOCK:2 * BLOCK] = vc_ref[...]
    vv_ref[0:1, :] = jnp.zeros((1, KV_WIDTH), F32)

    lo = jnp.where(n == 0, BLOCK - N_META, 0)
    r = lax.broadcasted_iota(jnp.int32, (BLOCK, 2 * BLOCK), 0)
    j = lax.broadcasted_iota(jnp.int32, (BLOCK, 2 * BLOCK), 1)
    mask3 = ((j > r) & (j <= r + BLOCK) & (j >= lo)).reshape(BLOCK // SUBLANES, SUBLANES, 2 * BLOCK)
    col0 = lax.broadcasted_iota(jnp.int32, (1, SUBLANES, 2 * BLOCK), 2) == 0
    low_half = lax.broadcasted_iota(jnp.int32, (BLOCK, LANES), 1) < HEAD_DIM
    low_half2 = lax.broadcasted_iota(jnp.int32, (2 * BLOCK, LANES), 1) < HEAD_DIM

    def dup_halves(ref, h):
        xf = ref[:, (h // 2) * LANES:(h // 2 + 1) * LANES]
        x_sw = pltpu.roll(xf, HEAD_DIM, 1)
        both = jnp.where(low_half2, xf, x_sw) if h % 2 == 0 else jnp.where(low_half2, x_sw, xf)
        return both.astype(BF16)

    rows_kv = GQA_GROUP * BLOCK
    for h in range(N_KV_HEADS):
        qa = q_ref[:, (2 * h) * LANES:(2 * h + 1) * LANES]
        qb = q_ref[:, (2 * h + 1) * LANES:(2 * h + 2) * LANES]
        zero = jnp.zeros_like(qa)
        lhs = jnp.concatenate([jnp.where(low_half, qa, zero), jnp.where(low_half, qb, zero),
                               jnp.where(low_half, zero, qa), jnp.where(low_half, zero, qb)], axis=0)
        s_all = lax.dot_general(lhs, dup_halves(kk_ref, h), NT_DIMS, preferred_element_type=F32)
        heads = (4 * h, 4 * h + 2, 4 * h + 1, 4 * h + 3)
        for g in range(GQA_GROUP):
            fill = jnp.where(col0, sink_ref[heads[g]], MASK_VALUE)
            s = jnp.where(mask3, s_all[g * BLOCK:(g + 1) * BLOCK].reshape(mask3.shape), fill)
            s_ref[h * rows_kv + g * BLOCK:h * rows_kv + (g + 1) * BLOCK] = s.reshape(BLOCK, 2 * BLOCK)

    s = s_ref[...]
    mx = jnp.max(s, axis=-1, keepdims=True)
    e = jnp.exp2(s - mx)
    p_ref[...] = e.astype(BF16)
    inv_ref[...] = jnp.broadcast_to(1.0 / jnp.sum(e, axis=-1, keepdims=True), inv_ref.shape)

    for h in range(N_KV_HEADS):
        base = h * rows_kv
        o_all = jnp.dot(p_ref[base:base + rows_kv], dup_halves(vv_ref, h), preferred_element_type=F32)
        o = [o_all[g * BLOCK:(g + 1) * BLOCK] * inv_ref[base + g * BLOCK:base + (g + 1) * BLOCK]
             for g in range(GQA_GROUP)]
        o_ref[:, (2 * h) * LANES:(2 * h + 1) * LANES] = jnp.where(low_half, o[0], o[2]).astype(o_ref.dtype)
        o_ref[:, (2 * h + 1) * LANES:(2 * h + 2) * LANES] = jnp.where(low_half, o[1], o[3]).astype(o_ref.dtype)


def _attn_prompt(sinks, q, k, v, k_meta, v_meta, n_batch, n_blk):
    kv_cur = pl.BlockSpec((BLOCK, KV_WIDTH), lambda b, n: (b * n_blk + n, 0))
    kv_prev = pl.BlockSpec((BLOCK, KV_WIDTH), lambda b, n: (b * n_blk + jnp.maximum(n - 1, 0), 0))
    return pl.pallas_call(
        _attn_prompt_kernel,
        grid=(n_batch, n_blk),
        in_specs=[
            pl.BlockSpec(memory_space=pltpu.SMEM),
            pl.BlockSpec((BLOCK, ATTN_WIDTH), lambda b, n: (b * n_blk + n, 0)),
            kv_cur, kv_prev, kv_cur, kv_prev,
            _const_spec((N_META, KV_WIDTH)),
            _const_spec((N_META, KV_WIDTH)),
        ],
        out_specs=pl.BlockSpec((BLOCK, ATTN_WIDTH), lambda b, n: (b * n_blk + n, 0)),
        out_shape=jax.ShapeDtypeStruct((n_batch * n_blk * BLOCK, ATTN_WIDTH), BF16),
        scratch_shapes=[
            pltpu.VMEM((2 * BLOCK, KV_WIDTH), F32),
            pltpu.VMEM((2 * BLOCK, KV_WIDTH), F32),
            pltpu.VMEM((N_HEADS * BLOCK, 2 * BLOCK), F32),
            pltpu.VMEM((N_HEADS * BLOCK, 2 * BLOCK), BF16),
            pltpu.VMEM((N_HEADS * BLOCK, LANES), F32),
        ],
        compiler_params=_params("parallel", "arbitrary"),
        name="attn_prompt",
    )(sinks, q, k, k, v, v, k_meta, v_meta)


def _attn_sample_kernel(t_new, bt, sink_ref, q_ref, kn_ref, vn_ref, ck_ref, cv_ref,
                        o_ref, ko_ref, vo_ref, s_ref, p_ref, inv_ref, osc_ref):
    nrow = N_HEADS * t_new
    w = ck_ref.shape[2]
    assert bt * t_new == LANES and w == LANES and nrow == LANES
    zero_blk = jnp.zeros((t_new, LANES), F32)

    r = lax.broadcasted_iota(jnp.int32, (nrow, 2 * w), 0) % t_new
    j = lax.broadcasted_iota(jnp.int32, (nrow, 2 * w), 1)
    cache_ok = (j > r) & (j < w)
    fill = jnp.where(j == 0, sink_ref[...][:, 0:1], MASK_VALUE)
    key0 = lax.broadcasted_iota(jnp.int32, (bt * nrow, w), 1) == 0
    low_half =lax.broadcasted_iota(jnp.int32, (t_new, LANES), 1) < HEAD_DIM
    tail = lax.broadcasted_iota(jnp.int32, (KV_WIDTH, LANES), 1) >= w - t_new
    kn = kn_ref[...]
    vn = vn_ref[...]
    kn_bf = kn.astype(BF16)
    vn_bf = vn.astype(BF16)

    for i in range(bt):
        new_shift = (w - t_new - i * t_new) % LANES
        ko_ref[i] = jnp.where(tail, pltpu.roll(kn, new_shift, 1), pltpu.roll(ck_ref[i], w - t_new, 1))
        vo_ref[i] = jnp.where(tail, pltpu.roll(vn, new_shift, 1), pltpu.roll(cv_ref[i], w - t_new, 1))

    for i in range(bt):
        qrows = []
        for t in range(N_HEADS):
            h = t // GQA_GROUP
            qv = q_ref[i * t_new:(i + 1) * t_new, (t // 2) * LANES:(t // 2 + 1) * LANES]
            if t % 2 != h % 2:
                qv = pltpu.roll(qv, HEAD_DIM, 1)
            keep = low_half if h % 2 == 0 else jnp.logical_not(low_half)
            cols = [zero_blk] * (KV_WIDTH // LANES)
            cols[h // 2] = jnp.where(keep, qv, 0.0)
            qrows.append(jnp.concatenate(cols, axis=1))
        qrow = jnp.concatenate(qrows, axis=0).astype(BF16)

        rhs_k = jnp.concatenate([ck_ref[i].astype(BF16), kn_bf], axis=1)
        s = jnp.dot(qrow, rhs_k, preferred_element_type=F32)
        own = w + i * t_new
        mask = cache_ok | ((j >= own) & (j <= own + r))
        s_ref[i * nrow:(i + 1) * nrow] = jnp.where(mask, s, fill)

    s = s_ref[...]
    mx = jnp.max(s, axis=-1, keepdims=True)
    e = jnp.exp2(s - mx)
    den = jnp.sum(e, axis=-1, keepdims=True)
    p_ref[:, 0:w] = jnp.where(key0, 0.0, e[:, 0:w]).astype(BF16)
    p_ref[:, w:2 * w] = e[:, w:2 * w].astype(BF16)
    inv_ref[...] = jnp.broadcast_to(1.0 / den, inv_ref.shape)

    for i in range(bt):
        rhs_v = jnp.concatenate([cv_ref[i].astype(BF16), vn_bf], axis=1)
        o_all = lax.dot_general(p_ref[i * nrow:(i + 1) * nrow], rhs_v, NT_DIMS, preferred_element_type=F32)
        for t2 in range(N_HEADS // 2):
            parts = []
            for side in range(2):
                t = 2 * t2 + side
                h = t // GQA_GROUP
                blk = o_all[t * t_new:(t + 1) * t_new, (h // 2) * LANES:(h // 2 + 1) * LANES]
                blk = blk * inv_ref[i * nrow + t * t_new:i * nrow + (t + 1) * t_new, :]
                if t % 2 != h % 2:
                    blk = pltpu.roll(blk, HEAD_DIM, 1)
                parts.append(blk)
            osc_ref[i * t_new:(i + 1) * t_new, t2 * LANES:(t2 + 1) * LANES] = (
                jnp.where(low_half, parts[0], parts[1]))

    o_ref[...] = osc_ref[...].astype(o_ref.dtype)


def _attn_sample(sink_rows, q, kt_new, vt_new, cache_kt, cache_vt, t_new, bt):
    nb, _, w = cache_kt.shape
    rows = bt * t_new
    cache_spec = pl.BlockSpec((bt, KV_WIDTH, w), lambda s: (s, 0, 0))
    new_spec = pl.BlockSpec((KV_WIDTH, rows), lambda s: (0, s))
    return pl.pallas_call(
        functools.partial(_attn_sample_kernel, t_new, bt),
        grid=(nb // bt,),
        in_specs=[
            _const_spec(sink_rows.shape),
            pl.BlockSpec((rows, ATTN_WIDTH), lambda s: (s, 0)),
            new_spec, new_spec, cache_spec, cache_spec,
        ],
        out_specs=[pl.BlockSpec((rows, ATTN_WIDTH), lambda s: (s, 0)), cache_spec, cache_spec],
        out_shape=[
            jax.ShapeDtypeStruct((nb * t_new, ATTN_WIDTH), BF16),
            jax.ShapeDtypeStruct(cache_kt.shape, F32),
            jax.ShapeDtypeStruct(cache_vt.shape, F32),
        ],
        scratch_shapes=[
            pltpu.VMEM((bt * N_HEADS * t_new, 2 * w), F32),
            pltpu.VMEM((bt * N_HEADS * t_new, 2 * w), BF16),
            pltpu.VMEM((bt * N_HEADS * t_new, LANES), F32),
            pltpu.VMEM((rows, ATTN_WIDTH), F32),
        ],
        compiler_params=_params("parallel"),
        name="attn_sample",
    )(sink_rows, q, kt_new, vt_new, cache_kt, cache_vt)


def _ln_swish(y, g, b):
    mu = jnp.mean(y, axis=-1, keepdims=True)
    d = y - mu
    var = jnp.mean(d * d, axis=-1, keepdims=True)
    yn = d * lax.rsqrt(var + LN_EPS) * g + b
    return yn * jax.nn.sigmoid(yn)


CONV_HALO = 32
CONV_SEG = 252
CONV_SEG_STEP = 12


def _conv_prompt_kernel(seq, u_ref, meta_ref, wb_ref, o_ref, win_ref):
    pad = CONV_HALO - N_META
    for c in range(N_SLAB):
        win_ref[c, 0:pad] = jnp.zeros((pad, LANES), F32)
        win_ref[c, pad:CONV_HALO] = meta_ref[c]
        win_ref[c, CONV_HALO:CONV_HALO + seq] = u_ref[c]
    first = CONV_HALO - (CONV_K - 1)

    rem_base = SUBLANES * CONV_SEG
    rem_seg = (seq - rem_base) // SUBLANES
    n_main = CONV_SEG // CONV_SEG_STEP

    def taps(c, wts, base, seg, n_g, i0):
        accs = [None] * n_g
        for m in range(n_g + CONV_K - 1):
            v = win_ref[c, pl.ds(base + first + i0 + m, SUBLANES, stride=seg), :]
            for g in range(max(0, m - CONV_K + 1), min(n_g, m + 1)):
                term = v * wts[m - g]
                accs[g] = term if accs[g] is None else accs[g] + term
        for g in range(n_g):
            o_ref[c, pl.ds(base + i0 + g, SUBLANES, stride=seg), :] = accs[g]

    for c in range(N_SLAB):
        wts = [wb_ref[tap, c] for tap in range(CONV_K)]

        def body(ci, carry, c=c, wts=wts):
            taps(c, wts, 0, CONV_SEG, CONV_SEG_STEP, ci * CONV_SEG_STEP)
            return carry

        lax.fori_loop(0, n_main, body, 0)
        taps(c, wts, rem_base, rem_seg, rem_seg, 0)


def _conv_prompt(u_slabs, u_meta, wb, n_batch, seq):
    assert CONV_SEG % CONV_SEG_STEP == 0 and seq > SUBLANES * CONV_SEG
    assert (seq - SUBLANES * CONV_SEG) % SUBLANES == 0
    blk = pl.BlockSpec((N_SLAB, seq, LANES), lambda b: (0, b, 0))
    return pl.pallas_call(
        functools.partial(_conv_prompt_kernel, seq),
        grid=(n_batch,),
        in_specs=[
            blk,
            _const_spec((N_SLAB, N_META, LANES)),
            _const_spec((CONV_K, N_SLAB, SUBLANES, LANES)),
        ],
        out_specs=blk,
        out_shape=jax.ShapeDtypeStruct((N_SLAB, n_batch * seq, LANES), F32),
        scratch_shapes=[pltpu.VMEM((N_SLAB, CONV_HALO + seq, LANES), F32)],
        compiler_params=_params("parallel"),
        name="conv_prompt",
    )(u_slabs, u_meta, wb)


def _conv_sample_kernel(t_new, st_ref, u_ref, wb_ref, bdw_ref, g_ref, b_ref, o_ref, so_ref):
    hist = st_ref.shape[0]
    bb = st_ref.shape[1]
    so_ref[0:hist - t_new] = st_ref[t_new:hist]
    so_ref[hist - t_new:hist] = u_ref[...]

    def src(k, rows):
        return st_ref[k, rows, :] if k < hist else u_ref[k - hist, rows, :]

    def body(rc, carry):
        rows = pl.ds(pl.multiple_of(rc * SUBLANES, SUBLANES), SUBLANES)
        for t in range(t_new):
            acc = jnp.zeros((SUBLANES, CONV_CH), F32)
            for tap in range(CONV_K):
                acc = acc + src(t + tap, rows) * wb_ref[tap]
            o_ref[t, rows, :] = _ln_swish(acc + bdw_ref[...], g_ref[...], b_ref[...]).astype(o_ref.dtype)
        return carry

    lax.fori_loop(0, bb // SUBLANES, body, 0)


def _conv_sample(state_t, u_t, wb, b_dw, ln_g, ln_b, bb):
    hist, nb, _ = state_t.shape
    t_new = u_t.shape[0]
    return pl.pallas_call(
        functools.partial(_conv_sample_kernel, t_new),
        grid=(nb // bb,),
        in_specs=[
            pl.BlockSpec((hist, bb, CONV_CH), lambda s: (0, s, 0)),
            pl.BlockSpec((t_new, bb, CONV_CH), lambda s: (0, s, 0)),
            _const_spec((CONV_K, SUBLANES, CONV_CH)),
            _const_spec((1, CONV_CH)),
            _const_spec((1, CONV_CH)),
            _const_spec((1, CONV_CH)),
        ],
        out_specs=[
            pl.BlockSpec((t_new, bb, CONV_CH), lambda s: (0, s, 0)),
            pl.BlockSpec((hist, bb, CONV_CH), lambda s: (0, s, 0)),
        ],
        out_shape=[
            jax.ShapeDtypeStruct((t_new, nb, CONV_CH), F32),
            jax.ShapeDtypeStruct(state_t.shape, F32),
        ],
        compiler_params=_params("parallel"),
        name="conv_sample",
    )(state_t, u_t, wb, b_dw, ln_g, ln_b)


def _outproj_kernel(raw_slabs, x_ref, a_ref, c_ref, bdw_ref, g_ref, b_ref, wo_ref, h_ref, cbf_ref):
    if raw_slabs:
        y = jnp.concatenate([c_ref[c] for c in range(N_SLAB)], axis=1)
        cbf_ref[...] = _ln_swish(y + bdw_ref[...], g_ref[...], b_ref[...]).astype(BF16)
    else:
        cbf_ref[...] = c_ref[...].astype(BF16)
    mix = jnp.dot(a_ref[...], wo_ref[0:ATTN_WIDTH], preferred_element_type=F32)
    mix = mix + jnp.dot(cbf_ref[...], wo_ref[ATTN_WIDTH:], preferred_element_type=F32)
    h_ref[...] = x_ref[...] + mix


def _outproj(x, attn, conv, b_dw, ln_g, ln_b, wo_bf, tm, raw_slabs):
    m = x.shape[0]
    if raw_slabs:
        conv_spec = pl.BlockSpec((N_SLAB, tm, LANES), lambda i: (0, i, 0))
    else:
        conv_spec = pl.BlockSpec((tm, CONV_CH), lambda i: (i, 0))
    return pl.pallas_call(
        functools.partial(_outproj_kernel, raw_slabs),
        grid=(m // tm,),
        in_specs=[
            pl.BlockSpec((tm, D_MODEL), lambda i: (i, 0)),
            pl.BlockSpec((tm, ATTN_WIDTH), lambda i: (i, 0)),
            conv_spec,
            _const_spec((1, CONV_CH)),
            _const_spec((1, CONV_CH)),
            _const_spec((1, CONV_CH)),
            _const_spec((D_MODEL, D_MODEL)),
        ],
        out_specs=pl.BlockSpec((tm, D_MODEL), lambda i: (i, 0)),
        out_shape=jax.ShapeDtypeStruct((m, D_MODEL), F32),
        scratch_shapes=[pltpu.VMEM((tm, CONV_CH), BF16)],
        compiler_params=_params("parallel"),
        name="outproj",
    )(x, attn, conv, b_dw, ln_g, ln_b, wo_bf)


def _mlp_kernel(h_ref, g2_ref, wu_ref, wd_ref, gf_ref, y_ref, xn_ref):
    f = pl.program_id(1)

    @pl.when(f == 0)
    def _():
        h1 = h_ref[...]
        y_ref[...] = h1
        xn_ref[...] = _rms(h1, g2_ref[...]).astype(BF16)

    up = jnp.dot(xn_ref[...], wu_ref[...].astype(BF16), preferred_element_type=F32)
    act = jnp.maximum(up, 0.0)
    act = (act * act).astype(BF16)
    y_ref[...] += jnp.dot(act, wd_ref[...].astype(BF16), preferred_element_type=F32)

    @pl.when(f == pl.num_programs(1) - 1)
    def _():
        y_ref[...] = _rms(y_ref[...], gf_ref[...])


def _mlp(h1, g2, wu_bf, wd_bf, gf, tm, tf):
    m = h1.shape[0]
    return pl.pallas_call(
        _mlp_kernel,
        grid=(m // tm, D_FF // tf),
        in_specs=[
            pl.BlockSpec((tm, D_MODEL), lambda i, f: (i, 0)),
            _const_spec((1, D_MODEL)),
            pl.BlockSpec((D_MODEL, tf), lambda i, f: (0, f)),
            pl.BlockSpec((tf, D_MODEL), lambda i, f: (f, 0)),
            _const_spec((1, D_MODEL)),
        ],
        out_specs=pl.BlockSpec((tm, D_MODEL), lambda i, f: (i, 0)),
        out_shape=jax.ShapeDtypeStruct((m, D_MODEL), F32),
        scratch_shapes=[pltpu.VMEM((tm, D_MODEL), BF16)],
        compiler_params=pltpu.CompilerParams(dimension_semantics=("parallel", "arbitrary"),
                                             vmem_limit_bytes=VMEM_LIMIT_MLP),
        name="mlp",
    )(h1, g2, wu_bf, wd_bf, gf)


def kernel(x_prompt, x_sample, cache_k, cache_v, state_conv, meta_tokens, norm_mix, w_in, attn_sinks,
           w_dw, b_dw, conv_ln_g, conv_ln_b, w_out, norm_mlp, w_up, w_down, norm_final):
    n_batch, seq, _ = x_prompt.shape
    n_dec, t_new, _ = x_sample.shape
    depth = w_in.shape[0]
    window = cache_k.shape[2]
    assert depth == 1 and seq % BLOCK == 0 and t_new == SUBLANES and window == WINDOW
    l = 0
    n_blk = seq // BLOCK
    tm = 512
    tm_mlp = 1024
    tf = 512
    bt = LANES // t_new

    w_in_bf = w_in[l].astype(BF16)
    w_out_bf = w_out[l].astype(BF16)
    g_mix = norm_mix[l][None, :]
    g_mlp = norm_mlp[l][None, :]
    g_fin = norm_final[None, :]
    sinks = attn_sinks[l].astype(F32) * LOG2_E
    bdw = b_dw[l][None, :]
    ln_g = conv_ln_g[l][None, :]
    ln_b = conv_ln_b[l][None, :]
    wb_rows = jnp.broadcast_to(w_dw[l][:, None, :], (CONV_K, SUBLANES, CONV_CH))
    wb_slabs = wb_rows.reshape(CONV_K, SUBLANES, N_SLAB, LANES).transpose(0, 2, 1, 3)

    rope_p = _rope_table(N_META + jnp.arange(seq, dtype=jnp.int32))
    rope_s = _rope_table(PAST_LEN + (jnp.arange(tm, dtype=jnp.int32) % t_new))
    ropet_s = jnp.tile(rope_s[:, :, :HEAD_DIM].transpose(0, 2, 1), (1, N_KV_HEADS, 1))
    rope_m = _rope_table(jnp.arange(N_META, dtype=jnp.int32))

    xp = x_prompt.reshape(n_batch * seq, D_MODEL)
    xs = x_sample.reshape(n_dec * t_new, D_MODEL)

    q_p, k_p, v_p, u_p = _inproj_prompt(xp, g_mix, w_in_bf, rope_p, tm_mlp)
    q_s, kt_s, vt_s, u_s = _inproj_sample(xs, g_mix, w_in_bf, rope_s, ropet_s, tm)
    _, k_m, v_m, u_m = _inproj_prompt(meta_tokens.astype(F32), g_mix, w_in_bf, rope_m, N_META)

    attn_p = _attn_prompt(sinks, q_p, k_p, v_p, k_m, v_m, n_batch, n_blk)
    conv_p = _conv_prompt(u_p, u_m, wb_slabs, n_batch, seq)

    ckt = cache_k[l].transpose(0, 2, 3, 1).reshape(n_dec, KV_WIDTH, window)
    cvt = cache_v[l].transpose(0, 2, 3, 1).reshape(n_dec, KV_WIDTH, window)
    sink_rows = jnp.broadcast_to(jnp.repeat(sinks, t_new)[:, None], (N_HEADS * t_new, LANES))
    attn_s, kt_out, vt_out = _attn_sample(sink_rows, q_s, kt_s, vt_s, ckt, cvt, t_new, bt)
    state_t = state_conv[l].transpose(1, 0, 2)
    u_t = u_s.reshape(n_dec, t_new, CONV_CH).transpose(1, 0, 2)
    conv_t, state_out_t = _conv_sample(state_t, u_t, wb_rows, bdw, ln_g, ln_b, 32)
    conv_s = conv_t.transpose(1, 0, 2).reshape(n_dec * t_new, CONV_CH)

    h_p = _outproj(xp, attn_p, conv_p, bdw, ln_g, ln_b, w_out_bf, tm, True)
    h_s = _outproj(xs, attn_s, conv_s, bdw, ln_g, ln_b, w_out_bf, tm, False)
    y_p = _mlp(h_p, g_mlp, w_up[l], w_down[l], g_fin, tm_mlp, tf)
    y_s = _mlp(h_s, g_mlp, w_up[l], w_down[l], g_fin, tm_mlp, tf)

    y_prompt = y_p.reshape(n_batch, seq, D_MODEL)
    y_sample = y_s.reshape(n_dec, t_new, D_MODEL)
    k_last = k_p.reshape(n_batch, seq, KV_WIDTH)[:, seq - WINDOW:]
    v_last = v_p.reshape(n_batch, seq, KV_WIDTH)[:, seq - WINDOW:]
    new_k_prompt = k_last.reshape(n_batch, WINDOW, N_KV_HEADS, HEAD_DIM)[None]
    new_v_prompt = v_last.reshape(n_batch, WINDOW, N_KV_HEADS, HEAD_DIM)[None]
    u_last = u_p.reshape(N_SLAB, n_batch, seq, LANES)[:, :, seq - (CONV_K - 1):]
    new_conv_prompt = u_last.transpose(1, 2, 0, 3).reshape(n_batch, CONV_K - 1, CONV_CH)[None]
    new_k_sample = kt_out.reshape(n_dec, N_KV_HEADS, HEAD_DIM, window).transpose(0, 3, 1, 2)[None]
    new_v_sample = vt_out.reshape(n_dec, N_KV_HEADS, HEAD_DIM, window).transpose(0, 3, 1, 2)[None]
    new_conv_sample = state_out_t.transpose(1, 0, 2)[None]
    return (y_prompt, y_sample, new_k_prompt, new_v_prompt, new_conv_prompt,
            new_k_sample, new_v_sample, new_conv_sample)
```

```python
import functools

import jax
import jax.numpy as jnp
from jax import lax
from jax.experimental import pallas as pl
from jax.experimental.pallas import tpu as pltpu

D_MODEL = 2048
N_META = 16
HEAD_DIM = 64
ATTN_WIDTH = D_MODEL // 2
CONV_CH = D_MODEL - ATTN_WIDTH
N_HEADS = ATTN_WIDTH // HEAD_DIM
N_KV_HEADS = 4
GQA_GROUP = N_HEADS // N_KV_HEADS
KV_WIDTH = N_KV_HEADS * HEAD_DIM
IN_COLS = ATTN_WIDTH + 2 * KV_WIDTH + 2 * CONV_CH
WINDOW = 128
BLOCK = 128
ROPE_THETA = 500000.0
ROT_DIM = HEAD_DIM // 4
CONV_K = 31
D_FF = 4 * D_MODEL
RMS_EPS = 1e-6
LN_EPS = 1e-5
MASK_VALUE = -1e30
PAST_LEN = 8192

LANES = 128
SUBLANES = 8
VMEM_LIMIT = 56 * 1024 * 1024
VMEM_LIMIT_MLP = 62 * 1024 * 1024
N_SLAB = CONV_CH // LANES

F32 = jnp.float32
BF16 = jnp.bfloat16

K_OFF = ATTN_WIDTH
V_OFF = ATTN_WIDTH + KV_WIDTH
A_OFF = ATTN_WIDTH + 2 * KV_WIDTH
B_OFF = A_OFF + CONV_CH
NT_DIMS = (((1,), (1,)), ((), ()))
LOG2_E = 1.4426950408889634
Q_SCALE = HEAD_DIM ** -0.5 * LOG2_E


def _params(*sem):
    return pltpu.CompilerParams(dimension_semantics=sem, vmem_limit_bytes=VMEM_LIMIT)


def _const_spec(shape):
    nd = len(shape)
    return pl.BlockSpec(shape, lambda *_: (0,) * nd, pipeline_mode=pl.Buffered(1))


def _rms(x, g):
    ms = jnp.mean(x * x, axis=-1, keepdims=True)
    return (x * lax.rsqrt(ms + RMS_EPS)) * g


def _rope_rows(z, rope_ref):
    return (z * rope_ref[0] + pltpu.roll(z, LANES - ROT_DIM // 2, 1) * rope_ref[1]
            + pltpu.roll(z, ROT_DIM // 2, 1) * rope_ref[2])


def _proj(xn_ref, w_ref, off, width):
    return jnp.dot(xn_ref[...], w_ref[:, off:off + width], preferred_element_type=F32)


def _write_q(xn_ref, w_ref, rope_ref, q_ref):
    half = ATTN_WIDTH // 2
    for c in range(2):
        acc = _proj(xn_ref, w_ref, c * half, half)
        for cc in range(half // LANES):
            z = _rope_rows(acc[:, cc * LANES:(cc + 1) * LANES], rope_ref) * Q_SCALE
            q_ref[:, c * half + cc * LANES:c * half + (cc + 1) * LANES] = z.astype(q_ref.dtype)


def _glu(xn_ref, w_ref, c):
    half = CONV_CH // 2
    a = _proj(xn_ref, w_ref, A_OFF + c * half, half)
    b = _proj(xn_ref, w_ref, B_OFF + c * half, half)
    return a * jax.nn.sigmoid(b)


def _inproj_prompt_kernel(x_ref, g_ref, w_ref, rope_ref, q_ref, k_ref, v_ref, u_ref, xn_ref):
    xn_ref[...] = _rms(x_ref[...], g_ref[...]).astype(BF16)
    _write_q(xn_ref, w_ref, rope_ref, q_ref)
    _write_kvu(xn_ref, w_ref, rope_ref, k_ref, v_ref, u_ref)


def _write_kvu(xn_ref, w_ref, rope_ref, k_ref, v_ref, u_ref):
    acc = _proj(xn_ref, w_ref, K_OFF, 2 * KV_WIDTH)
    for cc in range(KV_WIDTH // LANES):
        k_ref[:, cc * LANES:(cc + 1) * LANES] = _rope_rows(acc[:, cc * LANES:(cc + 1) * LANES], rope_ref)
    v_ref[...] = acc[:, KV_WIDTH:]
    per = N_SLAB // 2
    for c in range(2):
        u = _glu(xn_ref, w_ref, c)
        for cc in range(per):
            u_ref[c * per + cc] = u[:, cc * LANES:(cc + 1) * LANES]


def _inproj_prompt(x, g, w_bf, rope_tab, tm):
    m = x.shape[0]
    nrt = rope_tab.shape[1] // tm
    return pl.pallas_call(
        _inproj_prompt_kernel,
        grid=(m // tm,),
        in_specs=[
            pl.BlockSpec((tm, D_MODEL), lambda i: (i, 0)),
            _const_spec((1, D_MODEL)),
            _const_spec((D_MODEL, IN_COLS)),
            pl.BlockSpec((3, tm, LANES), lambda i: (0, i % nrt, 0)),
        ],
        out_specs=[
            pl.BlockSpec((tm, ATTN_WIDTH), lambda i: (i, 0)),
            pl.BlockSpec((tm, KV_WIDTH), lambda i: (i, 0)),
            pl.BlockSpec((tm, KV_WIDTH), lambda i: (i, 0)),
            pl.BlockSpec((N_SLAB, tm, LANES), lambda i: (0, i, 0)),
        ],
        out_shape=[
            jax.ShapeDtypeStruct((m, ATTN_WIDTH), BF16),
            jax.ShapeDtypeStruct((m, KV_WIDTH), F32),
            jax.ShapeDtypeStruct((m, KV_WIDTH), F32),
            jax.ShapeDtypeStruct((N_SLAB, m, LANES), F32),
        ],
        scratch_shapes=[pltpu.VMEM((tm, D_MODEL), BF16)],
        compiler_params=_params("parallel"),
        name="inproj_prompt",
    )(x, g, w_bf, rope_tab)


def _inproj_sample_kernel(x_ref, g_ref, w_ref, rope_ref, ropet_ref, meta_ref, ropem_ref,
                          q_ref, kt_ref, vt_ref, u_ref, km_ref, vm_ref, um_ref, xn_ref, xm_ref):
    @pl.when(pl.program_id(0) == 0)
    def _():
        xm_ref[...] = _rms(meta_ref[...], g_ref[...]).astype(BF16)
        _write_kvu(xm_ref, w_ref, ropem_ref, km_ref, vm_ref, um_ref)

    xn_ref[...] = _rms(x_ref[...], g_ref[...]).astype(BF16)
    _write_q(xn_ref, w_ref, rope_ref, q_ref)
    kvt = _proj(xn_ref, w_ref, K_OFF, 2 * KV_WIDTH).T
    kt = kvt[0:KV_WIDTH]
    sh = ROT_DIM // 2
    up = jnp.concatenate([kt[sh:], kt[:sh]], axis=0)
    dn = jnp.concatenate([kt[-sh:], kt[:-sh]], axis=0)
    kt_ref[...] = kt * ropet_ref[0] + up * ropet_ref[1] + dn * ropet_ref[2]
    vt_ref[...] = kvt[KV_WIDTH:]
    half = CONV_CH // 2
    for c in range(2):
        u_ref[:, c * half:(c + 1) * half] = _glu(xn_ref, w_ref, c)


def _inproj_sample(x, g, w_bf, rope_tab, ropet_tab, meta, rope_meta, tm):
    m = x.shape[0]
    n_meta = meta.shape[0]
    return pl.pallas_call(
        _inproj_sample_kernel,
        grid=(m // tm,),
        in_specs=[
            pl.BlockSpec((tm, D_MODEL), lambda i: (i, 0)),
            _const_spec((1, D_MODEL)),
            _const_spec((D_MODEL, IN_COLS)),
            _const_spec((3, tm, LANES)),
            _const_spec((3, KV_WIDTH, tm)),
            _const_spec((n_meta, D_MODEL)),
            _const_spec((3, n_meta, LANES)),
        ],
        out_specs=[
            pl.BlockSpec((tm, ATTN_WIDTH), lambda i: (i, 0)),
            pl.BlockSpec((KV_WIDTH, tm), lambda i: (0, i)),
            pl.BlockSpec((KV_WIDTH, tm), lambda i: (0, i)),
            pl.BlockSpec((tm, CONV_CH), lambda i: (i, 0)),
            pl.BlockSpec((n_meta, KV_WIDTH), lambda i: (0, 0)),
            pl.BlockSpec((n_meta, KV_WIDTH), lambda i: (0, 0)),
            pl.BlockSpec((N_SLAB, n_meta, LANES), lambda i: (0, 0, 0)),
        ],
        out_shape=[
            jax.ShapeDtypeStruct((m, ATTN_WIDTH), F32),
            jax.ShapeDtypeStruct((KV_WIDTH, m), F32),
            jax.ShapeDtypeStruct((KV_WIDTH, m), F32),
            jax.ShapeDtypeStruct((m, CONV_CH), F32),
            jax.ShapeDtypeStruct((n_meta, KV_WIDTH), F32),
            jax.ShapeDtypeStruct((n_meta, KV_WIDTH), F32),
            jax.ShapeDtypeStruct((N_SLAB, n_meta, LANES), F32),
        ],
        scratch_shapes=[pltpu.VMEM((tm, D_MODEL), BF16), pltpu.VMEM((n_meta, D_MODEL), BF16)],
        compiler_params=_params("arbitrary"),
        name="inproj_sample",
    )(x, g, w_bf, rope_tab, ropet_tab, meta, rope_meta)


def _rope_table(pos):
    half = ROT_DIM // 2
    inv = jnp.power(jnp.float32(ROPE_THETA), -jnp.arange(half, dtype=F32) * 2.0 / ROT_DIM)
    ang = pos.astype(F32)[:, None] * inv[None, :]
    cos, sin = jnp.cos(ang), jnp.sin(ang)
    t = pos.shape[0]
    rest = HEAD_DIM - ROT_DIM
    one = jnp.ones((t, rest), F32)
    zero = jnp.zeros((t, rest), F32)
    zh = jnp.zeros((t, half), F32)
    c = jnp.concatenate([cos, cos, one], axis=1)
    s_lo = jnp.concatenate([-sin, zh, zero], axis=1)
    s_hi = jnp.concatenate([zh, sin, zero], axis=1)
    tab = jnp.stack([c, s_lo, s_hi])
    return jnp.tile(tab, (1, 1, LANES // HEAD_DIM))


def _attn_prompt_kernel(sink_ref, q_ref, kc_ref, kp_ref, vc_ref, vp_ref, km_ref, vm_ref, o_ref,
                        kk_ref, vv_ref, s_ref, p_ref, inv_ref):
    n = pl.program_id(1)

    @pl.when(n == 0)
    def _():
        pad = jnp.zeros((BLOCK - N_META, KV_WIDTH), F32)
        kk_ref[0:BLOCK - N_META] = pad
        vv_ref[0:BLOCK - N_META] = pad
        kk_ref[BLOCK - N_META:BLOCK] = km_ref[...]
        vv_ref[BLOCK - N_META:BLOCK] = vm_ref[...]

    @pl.when(n > 0)
    def _():
        kk_ref[0:BLOCK] = kp_ref[...]
        vv_ref[0:BLOCK] = vp_ref[...]

    kk_ref[BLOCK:2 * BLOCK] = kc_ref[...]
    vv_ref[BLOCK:2 * BLOCK] = vc_ref[...]
    vv_ref[0:1, :] = jnp.zeros((1, KV_WIDTH), F32)

    lo = jnp.where(n == 0, BLOCK - N_META, 0)
    r = lax.broadcasted_iota(jnp.int32, (BLOCK, 2 * BLOCK), 0)
    j = lax.broadcasted_iota(jnp.int32, (BLOCK, 2 * BLOCK), 1)
    mask3 = ((j > r) & (j <= r + BLOCK) & (j >= lo)).reshape(BLOCK // SUBLANES, SUBLANES, 2 * BLOCK)
    col0 = lax.broadcasted_iota(jnp.int32, (1, SUBLANES, 2 * BLOCK), 2) == 0
    low_half = lax.broadcasted_iota(jnp.int32, (BLOCK, LANES), 1) < HEAD_DIM
    low_half2 = lax.broadcasted_iota(jnp.int32, (2 * BLOCK, LANES), 1) < HEAD_DIM

    def dup_halves(ref, h):
        xf = ref[:, (h // 2) * LANES:(h // 2 + 1) * LANES]
        x_sw = pltpu.roll(xf, HEAD_DIM, 1)
        both = jnp.where(low_half2, xf, x_sw) if h % 2 == 0 else jnp.where(low_half2, x_sw, xf)
        return both.astype(BF16)

    rows_kv = GQA_GROUP * BLOCK
    for h in range(N_KV_HEADS):
        qa = q_ref[:, (2 * h) * LANES:(2 * h + 1) * LANES]
        qb = q_ref[:, (2 * h + 1) * LANES:(2 * h + 2) * LANES]
        zero = jnp.zeros_like(qa)
        lhs = jnp.concatenate([jnp.where(low_half, qa, zero), jnp.where(low_half, qb, zero),
                               jnp.where(low_half, zero, qa), jnp.where(low_half, zero, qb)], axis=0)
        s_all = lax.dot_general(lhs, dup_halves(kk_ref, h), NT_DIMS, preferred_element_type=F32)
        heads = (4 * h, 4 * h + 2, 4 * h + 1, 4 * h + 3)
        for g in range(GQA_GROUP):
            fill = jnp.where(col0, sink_ref[heads[g]], MASK_VALUE)
            s = jnp.where(mask3, s_all[g * BLOCK:(g + 1) * BLOCK].reshape(mask3.shape), fill)
            s_ref[h * rows_kv + g * BLOCK:h * rows_kv + (g + 1) * BLOCK] = s.reshape(BLOCK, 2 * BLOCK)

    s = s_ref[...]
    mx = jnp.max(s, axis=-1, keepdims=True)
    e = jnp.exp2(s - mx)
    p_ref[...] = e.astype(BF16)
    inv_ref[...] = jnp.broadcast_to(1.0 / jnp.sum(e, axis=-1, keepdims=True), inv_ref.shape)

    for h in range(N_KV_HEADS):
        base = h * rows_kv
        o_all = jnp.dot(p_ref[base:base + rows_kv], dup_halves(vv_ref, h), preferred_element_type=F32)
        o = [o_all[g * BLOCK:(g + 1) * BLOCK] * inv_ref[base + g * BLOCK:base + (g + 1) * BLOCK]
             for g in range(GQA_GROUP)]
        o_ref[:, (2 * h) * LANES:(2 * h + 1) * LANES] = jnp.where(low_half, o[0], o[2]).astype(o_ref.dtype)
        o_ref[:, (2 * h + 1) * LANES:(2 * h + 2) * LANES] = jnp.where(low_half, o[1], o[3]).astype(o_ref.dtype)


def _attn_prompt(sinks, q, k, v, k_meta, v_meta, n_batch, n_blk):
    kv_cur = pl.BlockSpec((BLOCK, KV_WIDTH), lambda b, n: (b * n_blk + n, 0))
    kv_prev = pl.BlockSpec((BLOCK, KV_WIDTH), lambda b, n: (b * n_blk + jnp.maximum(n - 1, 0), 0))
    return pl.pallas_call(
        _attn_prompt_kernel,
        grid=(n_batch, n_blk),
        in_specs=[
            pl.BlockSpec(memory_space=pltpu.SMEM),
            pl.BlockSpec((BLOCK, ATTN_WIDTH), lambda b, n: (b * n_blk + n, 0)),
            kv_cur, kv_prev, kv_cur, kv_prev,
            _const_spec((N_META, KV_WIDTH)),
            _const_spec((N_META, KV_WIDTH)),
        ],
        out_specs=pl.BlockSpec((BLOCK, ATTN_WIDTH), lambda b, n: (b * n_blk + n, 0)),
        out_shape=jax.ShapeDtypeStruct((n_batch * n_blk * BLOCK, ATTN_WIDTH), BF16),
        scratch_shapes=[
            pltpu.VMEM((2 * BLOCK, KV_WIDTH), F32),
            pltpu.VMEM((2 * BLOCK, KV_WIDTH), F32),
            pltpu.VMEM((N_HEADS * BLOCK, 2 * BLOCK), F32),
            pltpu.VMEM((N_HEADS * BLOCK, 2 * BLOCK), BF16),
            pltpu.VMEM((N_HEADS * BLOCK, LANES), F32),
        ],
        compiler_params=_params("parallel", "arbitrary"),
        name="attn_prompt",
    )(sinks, q, k, k, v, v, k_meta, v_meta)


def _attn_sample_kernel(t_new, bt, sink_ref, q_ref, kn_ref, vn_ref, ck_ref, cv_ref,
                        o_ref, ko_ref, vo_ref, s_ref, p_ref, inv_ref, osc_ref):
    nrow = N_HEADS * t_new
    w = ck_ref.shape[2]
    assert bt * t_new == LANES and w == LANES and nrow == LANES
    zero_blk = jnp.zeros((t_new, LANES), F32)

    r = lax.broadcasted_iota(jnp.int32, (nrow, 2 * w), 0) % t_new
    j = lax.broadcasted_iota(jnp.int32, (nrow, 2 * w), 1)
    cache_ok = (j > r) & (j < w)
    fill = jnp.where(j == 0, sink_ref[...][:, 0:1], MASK_VALUE)
    key0 = lax.broadcasted_iota(jnp.int32, (bt * nrow, w), 1) == 0
    low_half =lax.broadcasted_iota(jnp.int32, (t_new, LANES), 1) < HEAD_DIM
    tail = lax.broadcasted_iota(jnp.int32, (KV_WIDTH, LANES), 1) >= w - t_new
    kn = kn_ref[...]
    vn = vn_ref[...]
    kn_bf = kn.astype(BF16)
    vn_bf = vn.astype(BF16)

    for i in range(bt):
        new_shift = (w - t_new - i * t_new) % LANES
        ko_ref[i] = jnp.where(tail, pltpu.roll(kn, new_shift, 1), pltpu.roll(ck_ref[i], w - t_new, 1))
        vo_ref[i] = jnp.where(tail, pltpu.roll(vn, new_shift, 1), pltpu.roll(cv_ref[i], w - t_new, 1))

    for i in range(bt):
        qrows = []
        for t in range(N_HEADS):
            h = t // GQA_GROUP
            qv = q_ref[i * t_new:(i + 1) * t_new, (t // 2) * LANES:(t // 2 + 1) * LANES]
            if t % 2 != h % 2:
                qv = pltpu.roll(qv, HEAD_DIM, 1)
            keep = low_half if h % 2 == 0 else jnp.logical_not(low_half)
            cols = [zero_blk] * (KV_WIDTH // LANES)
            cols[h // 2] = jnp.where(keep, qv, 0.0)
            qrows.append(jnp.concatenate(cols, axis=1))
        qrow = jnp.concatenate(qrows, axis=0).astype(BF16)

        rhs_k = jnp.concatenate([ck_ref[i].astype(BF16), kn_bf], axis=1)
        s = jnp.dot(qrow, rhs_k, preferred_element_type=F32)
        own = w + i * t_new
        mask = cache_ok | ((j >= own) & (j <= own + r))
        s_ref[i * nrow:(i + 1) * nrow] = jnp.where(mask, s, fill)

    s = s_ref[...]
    mx = jnp.max(s, axis=-1, keepdims=True)
    e = jnp.exp2(s - mx)
    den = jnp.sum(e, axis=-1, keepdims=True)
    p_ref[:, 0:w] = jnp.where(key0, 0.0, e[:, 0:w]).astype(BF16)
    p_ref[:, w:2 * w] = e[:, w:2 * w].astype(BF16)
    inv_ref[...] = jnp.broadcast_to(1.0 / den, inv_ref.shape)

    for i in range(bt):
        rhs_v = jnp.concatenate([cv_ref[i].astype(BF16), vn_bf], axis=1)
        o_all = lax.dot_general(p_ref[i * nrow:(i + 1) * nrow], rhs_v, NT_DIMS, preferred_element_type=F32)
        for t2 in range(N_HEADS // 2):
            parts = []
            for side in range(2):
                t = 2 * t2 + side
                h = t // GQA_GROUP
                blk = o_all[t * t_new:(t + 1) * t_new, (h // 2) * LANES:(h // 2 + 1) * LANES]
                blk = blk * inv_ref[i * nrow + t * t_new:i * nrow + (t + 1) * t_new, :]
                if t % 2 != h % 2:
                    blk = pltpu.roll(blk, HEAD_DIM, 1)
                parts.append(blk)
            osc_ref[i * t_new:(i + 1) * t_new, t2 * LANES:(t2 + 1) * LANES] = (
                jnp.where(low_half, parts[0], parts[1]))

    o_ref[...] = osc_ref[...].astype(o_ref.dtype)


def _attn_sample(sink_rows, q, kt_new, vt_new, cache_kt, cache_vt, t_new, bt):
    nb, _, w = cache_kt.shape
    rows = bt * t_new
    cache_spec = pl.BlockSpec((bt, KV_WIDTH, w), lambda s: (s, 0, 0))
    new_spec = pl.BlockSpec((KV_WIDTH, rows), lambda s: (0, s))
    return pl.pallas_call(
        functools.partial(_attn_sample_kernel, t_new, bt),
        grid=(nb // bt,),
        in_specs=[
            _const_spec(sink_rows.shape),
            pl.BlockSpec((rows, ATTN_WIDTH), lambda s: (s, 0)),
            new_spec, new_spec, cache_spec, cache_spec,
        ],
        out_specs=[pl.BlockSpec((rows, ATTN_WIDTH), lambda s: (s, 0)), cache_spec, cache_spec],
        out_shape=[
            jax.ShapeDtypeStruct((nb * t_new, ATTN_WIDTH), BF16),
            jax.ShapeDtypeStruct(cache_kt.shape, F32),
            jax.ShapeDtypeStruct(cache_vt.shape, F32),
        ],
        scratch_shapes=[
            pltpu.VMEM((bt * N_HEADS * t_new, 2 * w), F32),
            pltpu.VMEM((bt * N_HEADS * t_new, 2 * w), BF16),
            pltpu.VMEM((bt * N_HEADS * t_new, LANES), F32),
            pltpu.VMEM((rows, ATTN_WIDTH), F32),
        ],
        compiler_params=_params("parallel"),
        name="attn_sample",
    )(sink_rows, q, kt_new, vt_new, cache_kt, cache_vt)


def _ln_swish(y, g, b):
    mu = jnp.mean(y, axis=-1, keepdims=True)
    d = y - mu
    var = jnp.mean(d * d, axis=-1, keepdims=True)
    yn = d * lax.rsqrt(var + LN_EPS) * g + b
    return yn * jax.nn.sigmoid(yn)


CONV_HALO = 32
CONV_SEG = 252
CONV_SEG_STEP = 12


def _conv_prompt_kernel(seq, u_ref, meta_ref, wb_ref, o_ref, win_ref):
    pad = CONV_HALO - N_META
    for c in range(N_SLAB):
        win_ref[c, 0:pad] = jnp.zeros((pad, LANES), F32)
        win_ref[c, pad:CONV_HALO] = meta_ref[c]
        win_ref[c, CONV_HALO:CONV_HALO + seq] = u_ref[c]
    first = CONV_HALO - (CONV_K - 1)

    rem_base = SUBLANES * CONV_SEG
    rem_seg = (seq - rem_base) // SUBLANES
    n_main = CONV_SEG // CONV_SEG_STEP

    def taps(c, wts, base, seg, n_g, i0):
        accs = [None] * n_g
        for m in range(n_g + CONV_K - 1):
            v = win_ref[c, pl.ds(base + first + i0 + m, SUBLANES, stride=seg), :]
            for g in range(max(0, m - CONV_K + 1), min(n_g, m + 1)):
                term = v * wts[m - g]
                accs[g] = term if accs[g] is None else accs[g] + term
        for g in range(n_g):
            o_ref[c, pl.ds(base + i0 + g, SUBLANES, stride=seg), :] = accs[g]

    for c in range(N_SLAB):
        wts = [wb_ref[tap, c] for tap in range(CONV_K)]

        def body(ci, carry, c=c, wts=wts):
            taps(c, wts, 0, CONV_SEG, CONV_SEG_STEP, ci * CONV_SEG_STEP)
            return carry

        lax.fori_loop(0, n_main, body, 0)
        taps(c, wts, rem_base, rem_seg, rem_seg, 0)


def _conv_prompt(u_slabs, u_meta, wb, n_batch, seq):
    assert CONV_SEG % CONV_SEG_STEP == 0 and seq > SUBLANES * CONV_SEG
    assert (seq - SUBLANES * CONV_SEG) % SUBLANES == 0
    blk = pl.BlockSpec((N_SLAB, seq, LANES), lambda b: (0, b, 0))
    return pl.pallas_call(
        functools.partial(_conv_prompt_kernel, seq),
        grid=(n_batch,),
        in_specs=[
            blk,
            _const_spec((N_SLAB, N_META, LANES)),
            _const_spec((CONV_K, N_SLAB, SUBLANES, LANES)),
        ],
        out_specs=blk,
        out_shape=jax.ShapeDtypeStruct((N_SLAB, n_batch * seq, LANES), F32),
        scratch_shapes=[pltpu.VMEM((N_SLAB, CONV_HALO + seq, LANES), F32)],
        compiler_params=_params("parallel"),
        name="conv_prompt",
    )(u_slabs, u_meta, wb)


def _conv_sample_kernel(t_new, st_ref, u_ref, wb_ref, bdw_ref, g_ref, b_ref, o_ref, so_ref):
    hist = st_ref.shape[0]
    bb = st_ref.shape[1]
    so_ref[0:hist - t_new] = st_ref[t_new:hist]
    so_ref[hist - t_new:hist] = u_ref[...]

    def src(k, rows):
        return st_ref[k, rows, :] if k < hist else u_ref[k - hist, rows, :]

    def body(rc, carry):
        rows = pl.ds(pl.multiple_of(rc * SUBLANES, SUBLANES), SUBLANES)
        for t in range(t_new):
            acc = jnp.zeros((SUBLANES, CONV_CH), F32)
            for tap in range(CONV_K):
                acc = acc + src(t + tap, rows) * wb_ref[tap]
            o_ref[t, rows, :] = _ln_swish(acc + bdw_ref[...], g_ref[...], b_ref[...]).astype(o_ref.dtype)
        return carry

    lax.fori_loop(0, bb // SUBLANES, body, 0)


def _conv_sample(state_t, u_t, wb, b_dw, ln_g, ln_b, bb):
    hist, nb, _ = state_t.shape
    t_new = u_t.shape[0]
    return pl.pallas_call(
        functools.partial(_conv_sample_kernel, t_new),
        grid=(nb // bb,),
        in_specs=[
            pl.BlockSpec((hist, bb, CONV_CH), lambda s: (0, s, 0)),
            pl.BlockSpec((t_new, bb, CONV_CH), lambda s: (0, s, 0)),
            _const_spec((CONV_K, SUBLANES, CONV_CH)),
            _const_spec((1, CONV_CH)),
            _const_spec((1, CONV_CH)),
            _const_spec((1, CONV_CH)),
        ],
        out_specs=[
            pl.BlockSpec((t_new, bb, CONV_CH), lambda s: (0, s, 0)),
            pl.BlockSpec((hist, bb, CONV_CH), lambda s: (0, s, 0)),
        ],
        out_shape=[
            jax.ShapeDtypeStruct((t_new, nb, CONV_CH), F32),
            jax.ShapeDtypeStruct(state_t.shape, F32),
        ],
        compiler_params=_params("parallel"),
        name="conv_sample",
    )(state_t, u_t, wb, b_dw, ln_g, ln_b)


def _outproj_kernel(raw_slabs, x_ref, a_ref, c_ref, bdw_ref, g_ref, b_ref, wo_ref, h_ref, cbf_ref):
    if raw_slabs:
        y = jnp.concatenate([c_ref[c] for c in range(N_SLAB)], axis=1)
        cbf_ref[...] = _ln_swish(y + bdw_ref[...], g_ref[...], b_ref[...]).astype(BF16)
    else:
        cbf_ref[...] = c_ref[...].astype(BF16)
    mix = jnp.dot(a_ref[...], wo_ref[0:ATTN_WIDTH].astype(BF16), preferred_element_type=F32)
    mix = mix + jnp.dot(cbf_ref[...], wo_ref[ATTN_WIDTH:].astype(BF16), preferred_element_type=F32)
    h_ref[...] = x_ref[...] + mix


def _outproj(x, attn, conv, b_dw, ln_g, ln_b, wo_bf, tm, raw_slabs):
    m = x.shape[0]
    if raw_slabs:
        conv_spec = pl.BlockSpec((N_SLAB, tm, LANES), lambda i: (0, i, 0))
    else:
        conv_spec = pl.BlockSpec((tm, CONV_CH), lambda i: (i, 0))
    return pl.pallas_call(
        functools.partial(_outproj_kernel, raw_slabs),
        grid=(m // tm,),
        in_specs=[
            pl.BlockSpec((tm, D_MODEL), lambda i: (i, 0)),
            pl.BlockSpec((tm, ATTN_WIDTH), lambda i: (i, 0)),
            conv_spec,
            _const_spec((1, CONV_CH)),
            _const_spec((1, CONV_CH)),
            _const_spec((1, CONV_CH)),
            _const_spec((D_MODEL, D_MODEL)),
        ],
        out_specs=pl.BlockSpec((tm, D_MODEL), lambda i: (i, 0)),
        out_shape=jax.ShapeDtypeStruct((m, D_MODEL), F32),
        scratch_shapes=[pltpu.VMEM((tm, CONV_CH), BF16)],
        compiler_params=_params("parallel"),
        name="outproj",
    )(x, attn, conv, b_dw, ln_g, ln_b, wo_bf)


def _mlp_kernel(h_ref, g2_ref, wu_ref, wd_ref, gf_ref, y_ref, xn_ref):
    f = pl.program_id(1)

    @pl.when(f == 0)
    def _():
        h1 = h_ref[...]
        y_ref[...] = h1
        xn_ref[...] = _rms(h1, g2_ref[...]).astype(BF16)

    up = jnp.dot(xn_ref[...], wu_ref[...].astype(BF16), preferred_element_type=F32)
    act = jnp.maximum(up, 0.0)
    act = (act * act).astype(BF16)
    y_ref[...] += jnp.dot(act, wd_ref[...].astype(BF16), preferred_element_type=F32)

    @pl.when(f == pl.num_programs(1) - 1)
    def _():
        y_ref[...] = _rms(y_ref[...], gf_ref[...])


def _mlp(h1, g2, wu_bf, wd_bf, gf, tm, tf):
    m = h1.shape[0]
    return pl.pallas_call(
        _mlp_kernel,
        grid=(m // tm, D_FF // tf),
        in_specs=[
            pl.BlockSpec((tm, D_MODEL), lambda i, f: (i, 0)),
            _const_spec((1, D_MODEL)),
            pl.BlockSpec((D_MODEL, tf), lambda i, f: (0, f)),
            pl.BlockSpec((tf, D_MODEL), lambda i, f: (f, 0)),
            _const_spec((1, D_MODEL)),
        ],
        out_specs=pl.BlockSpec((tm, D_MODEL), lambda i, f: (i, 0)),
        out_shape=jax.ShapeDtypeStruct((m, D_MODEL), F32),
        scratch_shapes=[pltpu.VMEM((tm, D_MODEL), BF16)],
        compiler_params=pltpu.CompilerParams(dimension_semantics=("parallel", "arbitrary"),
                                             vmem_limit_bytes=VMEM_LIMIT_MLP),
        name="mlp",
    )(h1, g2, wu_bf, wd_bf, gf)


def kernel(x_prompt, x_sample, cache_k, cache_v, state_conv, meta_tokens, norm_mix, w_in, attn_sinks,
           w_dw, b_dw, conv_ln_g, conv_ln_b, w_out, norm_mlp, w_up, w_down, norm_final):
    n_batch, seq, _ = x_prompt.shape
    n_dec, t_new, _ = x_sample.shape
    depth = w_in.shape[0]
    window = cache_k.shape[2]
    assert depth == 1 and seq % BLOCK == 0 and t_new == SUBLANES and window == WINDOW
    l = 0
    n_blk = seq // BLOCK
    tm = 512
    tm_mlp = 1024
    tf = 512
    bt = LANES // t_new

    w_in_bf = w_in[l].astype(BF16)
    g_mix = norm_mix[l][None, :]
    g_mlp = norm_mlp[l][None, :]
    g_fin = norm_final[None, :]
    sinks = attn_sinks[l].astype(F32) * LOG2_E
    bdw = b_dw[l][None, :]
    ln_g = conv_ln_g[l][None, :]
    ln_b = conv_ln_b[l][None, :]
    wb_rows = jnp.broadcast_to(w_dw[l][:, None, :], (CONV_K, SUBLANES, CONV_CH))
    wb_slabs = wb_rows.reshape(CONV_K, SUBLANES, N_SLAB, LANES).transpose(0, 2, 1, 3)

    rope_p = _rope_table(N_META + jnp.arange(seq, dtype=jnp.int32))
    rope_s = _rope_table(PAST_LEN + (jnp.arange(tm, dtype=jnp.int32) % t_new))
    ropet_s = jnp.tile(rope_s[:, :, :HEAD_DIM].transpose(0, 2, 1), (1, N_KV_HEADS, 1))
    rope_m = _rope_table(jnp.arange(N_META, dtype=jnp.int32))

    xp = x_prompt.reshape(n_batch * seq, D_MODEL)
    xs = x_sample.reshape(n_dec * t_new, D_MODEL)

    q_p, k_p, v_p, u_p = _inproj_prompt(xp, g_mix, w_in_bf, rope_p, tm_mlp)
    q_s, kt_s, vt_s, u_s, k_m, v_m, u_m = _inproj_sample(xs, g_mix, w_in_bf, rope_s, ropet_s,
                                                         meta_tokens.astype(F32), rope_m, tm)

    attn_p = _attn_prompt(sinks, q_p, k_p, v_p, k_m, v_m, n_batch, n_blk)
    conv_p = _conv_prompt(u_p, u_m, wb_slabs, n_batch, seq)

    ckt = cache_k[l].transpose(0, 2, 3, 1).reshape(n_dec, KV_WIDTH, window)
    cvt = cache_v[l].transpose(0, 2, 3, 1).reshape(n_dec, KV_WIDTH, window)
    sink_rows = jnp.broadcast_to(jnp.repeat(sinks, t_new)[:, None], (N_HEADS * t_new, LANES))
    attn_s, kt_out, vt_out = _attn_sample(sink_rows, q_s, kt_s, vt_s, ckt, cvt, t_new, bt)
    state_t = state_conv[l].transpose(1, 0, 2)
    u_t = u_s.reshape(n_dec, t_new, CONV_CH).transpose(1, 0, 2)
    conv_t, state_out_t = _conv_sample(state_t, u_t, wb_rows, bdw, ln_g, ln_b, 32)
    conv_s = conv_t.transpose(1, 0, 2).reshape(n_dec * t_new, CONV_CH)

    h_p = _outproj(xp, attn_p, conv_p, bdw, ln_g, ln_b, w_out[l], tm, True)
    h_s = _outproj(xs, attn_s, conv_s, bdw, ln_g, ln_b, w_out[l], tm, False)
    y_p = _mlp(h_p, g_mlp, w_up[l], w_down[l], g_fin, tm_mlp, tf)
    y_s = _mlp(h_s, g_mlp, w_up[l], w_down[l], g_fin, tm_mlp, tf)

    y_prompt = y_p.reshape(n_batch, seq, D_MODEL)
    y_sample = y_s.reshape(n_dec, t_new, D_MODEL)
    k_last = k_p.reshape(n_batch, seq, KV_WIDTH)[:, seq - WINDOW:]
    v_last = v_p.reshape(n_batch, seq, KV_WIDTH)[:, seq - WINDOW:]
    new_k_prompt = k_last.reshape(n_batch, WINDOW, N_KV_HEADS, HEAD_DIM)[None]
    new_v_prompt = v_last.reshape(n_batch, WINDOW, N_KV_HEADS, HEAD_DIM)[None]
    u_last = u_p.reshape(N_SLAB, n_batch, seq, LANES)[:, :, seq - (CONV_K - 1):]
    new_conv_prompt = u_last.transpose(1, 2, 0, 3).reshape(n_batch, CONV_K - 1, CONV_CH)[None]
    new_k_sample = kt_out.reshape(n_dec, N_KV_HEADS, HEAD_DIM, window).transpose(0, 3, 1, 2)[None]
    new_v_sample = vt_out.reshape(n_dec, N_KV_HEADS, HEAD_DIM, window).transpose(0, 3, 1, 2)[None]
    new_conv_sample = state_out_t.transpose(1, 0, 2)[None]
    return (y_prompt, y_sample, new_k_prompt, new_v_prompt, new_conv_prompt,
            new_k_sample, new_v_sample, new_conv_sample)
```

```python
import functools

import jax
import jax.numpy as jnp
from jax import lax
from jax.experimental import pallas as pl
from jax.experimental.pallas import tpu as pltpu

D_MODEL = 2048
N_META = 16
HEAD_DIM = 64
ATTN_WIDTH = D_MODEL // 2
CONV_CH = D_MODEL - ATTN_WIDTH
N_HEADS = ATTN_WIDTH // HEAD_DIM
N_KV_HEADS = 4
GQA_GROUP = N_HEADS // N_KV_HEADS
KV_WIDTH = N_KV_HEADS * HEAD_DIM
IN_COLS = ATTN_WIDTH + 2 * KV_WIDTH + 2 * CONV_CH
WINDOW = 128
BLOCK = 128
ROPE_THETA = 500000.0
ROT_DIM = HEAD_DIM // 4
CONV_K = 31
D_FF = 4 * D_MODEL
RMS_EPS = 1e-6
LN_EPS = 1e-5
MASK_VALUE = -1e30
PAST_LEN = 8192

LANES = 128
SUBLANES = 8
VMEM_LIMIT = 56 * 1024 * 1024
VMEM_LIMIT_MLP = 62 * 1024 * 1024
N_SLAB = CONV_CH // LANES

F32 = jnp.float32
BF16 = jnp.bfloat16

K_OFF = ATTN_WIDTH
V_OFF = ATTN_WIDTH + KV_WIDTH
A_OFF = ATTN_WIDTH + 2 * KV_WIDTH
B_OFF = A_OFF + CONV_CH
NT_DIMS = (((1,), (1,)), ((), ()))
LOG2_E = 1.4426950408889634
Q_SCALE = HEAD_DIM ** -0.5 * LOG2_E


def _params(*sem):
    return pltpu.CompilerParams(dimension_semantics=sem, vmem_limit_bytes=VMEM_LIMIT)


def _const_spec(shape):
    nd = len(shape)
    return pl.BlockSpec(shape, lambda *_: (0,) * nd, pipeline_mode=pl.Buffered(1))


def _rms(x, g):
    ms = jnp.mean(x * x, axis=-1, keepdims=True)
    return (x * lax.rsqrt(ms + RMS_EPS)) * g


def _rope_rows(z, rope_ref):
    return (z * rope_ref[0] + pltpu.roll(z, LANES - ROT_DIM // 2, 1) * rope_ref[1]
            + pltpu.roll(z, ROT_DIM // 2, 1) * rope_ref[2])


def _proj(xn_ref, w_ref, off, width):
    return jnp.dot(xn_ref[...], w_ref[:, off:off + width].astype(BF16), preferred_element_type=F32)


def _write_q(xn_ref, w_ref, rope_ref, q_ref):
    half = ATTN_WIDTH // 2
    for c in range(2):
        acc = _proj(xn_ref, w_ref, c * half, half)
        for cc in range(half // LANES):
            z = _rope_rows(acc[:, cc * LANES:(cc + 1) * LANES], rope_ref) * Q_SCALE
            q_ref[:, c * half + cc * LANES:c * half + (cc + 1) * LANES] = z.astype(q_ref.dtype)


def _glu(xn_ref, w_ref, c):
    half = CONV_CH // 2
    a = _proj(xn_ref, w_ref, A_OFF + c * half, half)
    b = _proj(xn_ref, w_ref, B_OFF + c * half, half)
    return a * jax.nn.sigmoid(b)


def _inproj_prompt_kernel(x_ref, g_ref, w_ref, rope_ref, q_ref, k_ref, v_ref, u_ref, xn_ref):
    xn_ref[...] = _rms(x_ref[...], g_ref[...]).astype(BF16)
    _write_q(xn_ref, w_ref, rope_ref, q_ref)
    _write_kvu(xn_ref, w_ref, rope_ref, k_ref, v_ref, u_ref)


def _write_kvu(xn_ref, w_ref, rope_ref, k_ref, v_ref, u_ref):
    acc = _proj(xn_ref, w_ref, K_OFF, 2 * KV_WIDTH)
    for cc in range(KV_WIDTH // LANES):
        k_ref[:, cc * LANES:(cc + 1) * LANES] = _rope_rows(acc[:, cc * LANES:(cc + 1) * LANES], rope_ref)
    v_ref[...] = acc[:, KV_WIDTH:]
    per = N_SLAB // 2
    for c in range(2):
        u = _glu(xn_ref, w_ref, c)
        for cc in range(per):
            u_ref[c * per + cc] = u[:, cc * LANES:(cc + 1) * LANES]


def _inproj_prompt(x, g, w_in, rope_tab, tm):
    m = x.shape[0]
    nrt = rope_tab.shape[1] // tm
    return pl.pallas_call(
        _inproj_prompt_kernel,
        grid=(m // tm,),
        in_specs=[
            pl.BlockSpec((tm, D_MODEL), lambda i: (i, 0)),
            _const_spec((1, D_MODEL)),
            _const_spec((D_MODEL, IN_COLS)),
            pl.BlockSpec((3, tm, LANES), lambda i: (0, i % nrt, 0)),
        ],
        out_specs=[
            pl.BlockSpec((tm, ATTN_WIDTH), lambda i: (i, 0)),
            pl.BlockSpec((tm, KV_WIDTH), lambda i: (i, 0)),
            pl.BlockSpec((tm, KV_WIDTH), lambda i: (i, 0)),
            pl.BlockSpec((N_SLAB, tm, LANES), lambda i: (0, i, 0)),
        ],
        out_shape=[
            jax.ShapeDtypeStruct((m, ATTN_WIDTH), BF16),
            jax.ShapeDtypeStruct((m, KV_WIDTH), F32),
            jax.ShapeDtypeStruct((m, KV_WIDTH), F32),
            jax.ShapeDtypeStruct((N_SLAB, m, LANES), F32),
        ],
        scratch_shapes=[pltpu.VMEM((tm, D_MODEL), BF16)],
        compiler_params=_params("parallel"),
        name="inproj_prompt",
    )(x, g, w_in, rope_tab)


def _inproj_sample_kernel(x_ref, g_ref, w_ref, rope_ref, ropet_ref, meta_ref, ropem_ref,
                          q_ref, kt_ref, vt_ref, u_ref, km_ref, vm_ref, um_ref, xn_ref, xm_ref):
    @pl.when(pl.program_id(0) == 0)
    def _():
        xm_ref[...] = _rms(meta_ref[...], g_ref[...]).astype(BF16)
        _write_kvu(xm_ref, w_ref, ropem_ref, km_ref, vm_ref, um_ref)

    xn_ref[...] = _rms(x_ref[...], g_ref[...]).astype(BF16)
    _write_q(xn_ref, w_ref, rope_ref, q_ref)
    kvt = _proj(xn_ref, w_ref, K_OFF, 2 * KV_WIDTH).T
    kt = kvt[0:KV_WIDTH]
    sh = ROT_DIM // 2
    up = jnp.concatenate([kt[sh:], kt[:sh]], axis=0)
    dn = jnp.concatenate([kt[-sh:], kt[:-sh]], axis=0)
    kt_ref[...] = kt * ropet_ref[0] + up * ropet_ref[1] + dn * ropet_ref[2]
    vt_ref[...] = kvt[KV_WIDTH:]
    per = N_SLAB // 2
    for c in range(2):
        u = _glu(xn_ref, w_ref, c)
        for cc in range(per):
            u_ref[c * per + cc] = u[:, cc * LANES:(cc + 1) * LANES]


def _inproj_sample(x, g, w_in, rope_tab, ropet_tab, meta, rope_meta, tm):
    m = x.shape[0]
    n_meta = meta.shape[0]
    return pl.pallas_call(
        _inproj_sample_kernel,
        grid=(m // tm,),
        in_specs=[
            pl.BlockSpec((tm, D_MODEL), lambda i: (i, 0)),
            _const_spec((1, D_MODEL)),
            _const_spec((D_MODEL, IN_COLS)),
            _const_spec((3, tm, LANES)),
            _const_spec((3, KV_WIDTH, tm)),
            _const_spec((n_meta, D_MODEL)),
            _const_spec((3, n_meta, LANES)),
        ],
        out_specs=[
            pl.BlockSpec((tm, ATTN_WIDTH), lambda i: (i, 0)),
            pl.BlockSpec((KV_WIDTH, tm), lambda i: (0, i)),
            pl.BlockSpec((KV_WIDTH, tm), lambda i: (0, i)),
            pl.BlockSpec((N_SLAB, tm, LANES), lambda i: (0, i, 0)),
            pl.BlockSpec((n_meta, KV_WIDTH), lambda i: (0, 0)),
            pl.BlockSpec((n_meta, KV_WIDTH), lambda i: (0, 0)),
            pl.BlockSpec((N_SLAB, n_meta, LANES), lambda i: (0, 0, 0)),
        ],
        out_shape=[
            jax.ShapeDtypeStruct((m, ATTN_WIDTH), F32),
            jax.ShapeDtypeStruct((KV_WIDTH, m), F32),
            jax.ShapeDtypeStruct((KV_WIDTH, m), F32),
            jax.ShapeDtypeStruct((N_SLAB, m, LANES), F32),
            jax.ShapeDtypeStruct((n_meta, KV_WIDTH), F32),
            jax.ShapeDtypeStruct((n_meta, KV_WIDTH), F32),
            jax.ShapeDtypeStruct((N_SLAB, n_meta, LANES), F32),
        ],
        scratch_shapes=[pltpu.VMEM((tm, D_MODEL), BF16), pltpu.VMEM((n_meta, D_MODEL), BF16)],
        compiler_params=_params("arbitrary"),
        name="inproj_sample",
    )(x, g, w_in, rope_tab, ropet_tab, meta, rope_meta)


def _rope_table(pos):
    half = ROT_DIM // 2
    inv = jnp.power(jnp.float32(ROPE_THETA), -jnp.arange(half, dtype=F32) * 2.0 / ROT_DIM)
    ang = pos.astype(F32)[:, None] * inv[None, :]
    cos, sin = jnp.cos(ang), jnp.sin(ang)
    t = pos.shape[0]
    rest = HEAD_DIM - ROT_DIM
    one = jnp.ones((t, rest), F32)
    zero = jnp.zeros((t, rest), F32)
    zh = jnp.zeros((t, half), F32)
    c = jnp.concatenate([cos, cos, one], axis=1)
    s_lo = jnp.concatenate([-sin, zh, zero], axis=1)
    s_hi = jnp.concatenate([zh, sin, zero], axis=1)
    tab = jnp.stack([c, s_lo, s_hi])
    return jnp.tile(tab, (1, 1, LANES // HEAD_DIM))


def _attn_prompt_kernel(sink_ref, q_ref, kc_ref, kp_ref, vc_ref, vp_ref, km_ref, vm_ref, o_ref,
                        kk_ref, vv_ref, s_ref, p_ref, inv_ref):
    n = pl.program_id(1)

    @pl.when(n == 0)
    def _():
        pad = jnp.zeros((BLOCK - N_META, KV_WIDTH), F32)
        kk_ref[0:BLOCK - N_META] = pad
        vv_ref[0:BLOCK - N_META] = pad
        kk_ref[BLOCK - N_META:BLOCK] = km_ref[...]
        vv_ref[BLOCK - N_META:BLOCK] = vm_ref[...]

    @pl.when(n > 0)
    def _():
        kk_ref[0:BLOCK] = kp_ref[...]
        vv_ref[0:BLOCK] = vp_ref[...]

    kk_ref[BLOCK:2 * BLOCK] = kc_ref[...]
    vv_ref[BLOCK:2 * BLOCK] = vc_ref[...]
    vv_ref[0:1, :] = jnp.zeros((1, KV_WIDTH), F32)

    lo = jnp.where(n == 0, BLOCK - N_META, 0)
    r = lax.broadcasted_iota(jnp.int32, (BLOCK, 2 * BLOCK), 0)
    j = lax.broadcasted_iota(jnp.int32, (BLOCK, 2 * BLOCK), 1)
    mask3 = ((j > r) & (j <= r + BLOCK) & (j >= lo)).reshape(BLOCK // SUBLANES, SUBLANES, 2 * BLOCK)
    col0 = lax.broadcasted_iota(jnp.int32, (1, SUBLANES, 2 * BLOCK), 2) == 0
    low_half = lax.broadcasted_iota(jnp.int32, (BLOCK, LANES), 1) < HEAD_DIM
    low_half2 = lax.broadcasted_iota(jnp.int32, (2 * BLOCK, LANES), 1) < HEAD_DIM

    def dup_halves(ref, h):
        xf = ref[:, (h // 2) * LANES:(h // 2 + 1) * LANES]
        x_sw = pltpu.roll(xf, HEAD_DIM, 1)
        both = jnp.where(low_half2, xf, x_sw) if h % 2 == 0 else jnp.where(low_half2, x_sw, xf)
        return both.astype(BF16)

    rows_kv = GQA_GROUP * BLOCK
    for h in range(N_KV_HEADS):
        qa = q_ref[:, (2 * h) * LANES:(2 * h + 1) * LANES]
        qb = q_ref[:, (2 * h + 1) * LANES:(2 * h + 2) * LANES]
        zero = jnp.zeros_like(qa)
        lhs = jnp.concatenate([jnp.where(low_half, qa, zero), jnp.where(low_half, qb, zero),
                               jnp.where(low_half, zero, qa), jnp.where(low_half, zero, qb)], axis=0)
        s_all = lax.dot_general(lhs, dup_halves(kk_ref, h), NT_DIMS, preferred_element_type=F32)
        heads = (4 * h, 4 * h + 2, 4 * h + 1, 4 * h + 3)
        for g in range(GQA_GROUP):
            fill = jnp.where(col0, sink_ref[heads[g]], MASK_VALUE)
            s = jnp.where(mask3, s_all[g * BLOCK:(g + 1) * BLOCK].reshape(mask3.shape), fill)
            s_ref[h * rows_kv + g * BLOCK:h * rows_kv + (g + 1) * BLOCK] = s.reshape(BLOCK, 2 * BLOCK)

    s = s_ref[...]
    mx = jnp.max(s, axis=-1, keepdims=True)
    e = jnp.exp2(s - mx)
    p_ref[...] = e.astype(BF16)
    inv_ref[...] = jnp.broadcast_to(1.0 / jnp.sum(e, axis=-1, keepdims=True), inv_ref.shape)

    for h in range(N_KV_HEADS):
        base = h * rows_kv
        o_all = jnp.dot(p_ref[base:base + rows_kv], dup_halves(vv_ref, h), preferred_element_type=F32)
        o = [o_all[g * BLOCK:(g + 1) * BLOCK] * inv_ref[base + g * BLOCK:base + (g + 1) * BLOCK]
             for g in range(GQA_GROUP)]
        o_ref[:, (2 * h) * LANES:(2 * h + 1) * LANES] = jnp.where(low_half, o[0], o[2]).astype(o_ref.dtype)
        o_ref[:, (2 * h + 1) * LANES:(2 * h + 2) * LANES] = jnp.where(low_half, o[1], o[3]).astype(o_ref.dtype)


def _attn_prompt(sinks, q, k, v, k_meta, v_meta, n_batch, n_blk):
    kv_cur = pl.BlockSpec((BLOCK, KV_WIDTH), lambda b, n: (b * n_blk + n, 0))
    kv_prev = pl.BlockSpec((BLOCK, KV_WIDTH), lambda b, n: (b * n_blk + jnp.maximum(n - 1, 0), 0))
    return pl.pallas_call(
        _attn_prompt_kernel,
        grid=(n_batch, n_blk),
        in_specs=[
            pl.BlockSpec(memory_space=pltpu.SMEM),
            pl.BlockSpec((BLOCK, ATTN_WIDTH), lambda b, n: (b * n_blk + n, 0)),
            kv_cur, kv_prev, kv_cur, kv_prev,
            _const_spec((N_META, KV_WIDTH)),
            _const_spec((N_META, KV_WIDTH)),
        ],
        out_specs=pl.BlockSpec((BLOCK, ATTN_WIDTH), lambda b, n: (b * n_blk + n, 0)),
        out_shape=jax.ShapeDtypeStruct((n_batch * n_blk * BLOCK, ATTN_WIDTH), BF16),
        scratch_shapes=[
            pltpu.VMEM((2 * BLOCK, KV_WIDTH), F32),
            pltpu.VMEM((2 * BLOCK, KV_WIDTH), F32),
            pltpu.VMEM((N_HEADS * BLOCK, 2 * BLOCK), F32),
            pltpu.VMEM((N_HEADS * BLOCK, 2 * BLOCK), BF16),
            pltpu.VMEM((N_HEADS * BLOCK, LANES), F32),
        ],
        compiler_params=_params("parallel", "arbitrary"),
        name="attn_prompt",
    )(sinks, q, k, k, v, v, k_meta, v_meta)


def _attn_sample_kernel(t_new, bt, sink_ref, q_ref, kn_ref, vn_ref, ck_ref, cv_ref,
                        o_ref, ko_ref, vo_ref, s_ref, p_ref, inv_ref, osc_ref):
    nrow = N_HEADS * t_new
    w = ck_ref.shape[2]
    assert bt * t_new == LANES and w == LANES and nrow == LANES
    zero_blk = jnp.zeros((t_new, LANES), F32)

    r = lax.broadcasted_iota(jnp.int32, (nrow, 2 * w), 0) % t_new
    j = lax.broadcasted_iota(jnp.int32, (nrow, 2 * w), 1)
    cache_ok = (j > r) & (j < w)
    fill = jnp.where(j == 0, sink_ref[...][:, 0:1], MASK_VALUE)
    key0 = lax.broadcasted_iota(jnp.int32, (bt * nrow, w), 1) == 0
    low_half =lax.broadcasted_iota(jnp.int32, (t_new, LANES), 1) < HEAD_DIM
    tail = lax.broadcasted_iota(jnp.int32, (KV_WIDTH, LANES), 1) >= w - t_new
    kn = kn_ref[...]
    vn = vn_ref[...]
    kn_bf = kn.astype(BF16)
    vn_bf = vn.astype(BF16)

    for i in range(bt):
        new_shift = (w - t_new - i * t_new) % LANES
        ko_ref[i] = jnp.where(tail, pltpu.roll(kn, new_shift, 1), pltpu.roll(ck_ref[i], w - t_new, 1))
        vo_ref[i] = jnp.where(tail, pltpu.roll(vn, new_shift, 1), pltpu.roll(cv_ref[i], w - t_new, 1))

    for i in range(bt):
        qrows = []
        for t in range(N_HEADS):
            h = t // GQA_GROUP
            qv = q_ref[i * t_new:(i + 1) * t_new, (t // 2) * LANES:(t // 2 + 1) * LANES]
            if t % 2 != h % 2:
                qv = pltpu.roll(qv, HEAD_DIM, 1)
            keep = low_half if h % 2 == 0 else jnp.logical_not(low_half)
            cols = [zero_blk] * (KV_WIDTH // LANES)
            cols[h // 2] = jnp.where(keep, qv, 0.0)
            qrows.append(jnp.concatenate(cols, axis=1))
        qrow = jnp.concatenate(qrows, axis=0).astype(BF16)

        rhs_k = jnp.concatenate([ck_ref[i].astype(BF16), kn_bf], axis=1)
        s = jnp.dot(qrow, rhs_k, preferred_element_type=F32)
        own = w + i * t_new
        mask = cache_ok | ((j >= own) & (j <= own + r))
        s_ref[i * nrow:(i + 1) * nrow] = jnp.where(mask, s, fill)

    s = s_ref[...]
    mx = jnp.max(s, axis=-1, keepdims=True)
    e = jnp.exp2(s - mx)
    den = jnp.sum(e, axis=-1, keepdims=True)
    p_ref[:, 0:w] = jnp.where(key0, 0.0, e[:, 0:w]).astype(BF16)
    p_ref[:, w:2 * w] = e[:, w:2 * w].astype(BF16)
    inv_ref[...] = jnp.broadcast_to(1.0 / den, inv_ref.shape)

    for i in range(bt):
        rhs_v = jnp.concatenate([cv_ref[i].astype(BF16), vn_bf], axis=1)
        o_all = lax.dot_general(p_ref[i * nrow:(i + 1) * nrow], rhs_v, NT_DIMS, preferred_element_type=F32)
        for t2 in range(N_HEADS // 2):
            parts = []
            for side in range(2):
                t = 2 * t2 + side
                h = t // GQA_GROUP
                blk = o_all[t * t_new:(t + 1) * t_new, (h // 2) * LANES:(h // 2 + 1) * LANES]
                blk = blk * inv_ref[i * nrow + t * t_new:i * nrow + (t + 1) * t_new, :]
                if t % 2 != h % 2:
                    blk = pltpu.roll(blk, HEAD_DIM, 1)
                parts.append(blk)
            osc_ref[i * t_new:(i + 1) * t_new, t2 * LANES:(t2 + 1) * LANES] = (
                jnp.where(low_half, parts[0], parts[1]))

    o_ref[...] = osc_ref[...].astype(o_ref.dtype)


def _attn_sample(sink_rows, q, kt_new, vt_new, cache_kt, cache_vt, t_new, bt):
    nb, _, w = cache_kt.shape
    rows = bt * t_new
    cache_spec = pl.BlockSpec((bt, KV_WIDTH, w), lambda s: (s, 0, 0))
    new_spec = pl.BlockSpec((KV_WIDTH, rows), lambda s: (0, s))
    return pl.pallas_call(
        functools.partial(_attn_sample_kernel, t_new, bt),
        grid=(nb // bt,),
        in_specs=[
            _const_spec(sink_rows.shape),
            pl.BlockSpec((rows, ATTN_WIDTH), lambda s: (s, 0)),
            new_spec, new_spec, cache_spec, cache_spec,
        ],
        out_specs=[pl.BlockSpec((rows, ATTN_WIDTH), lambda s: (s, 0)), cache_spec, cache_spec],
        out_shape=[
            jax.ShapeDtypeStruct((nb * t_new, ATTN_WIDTH), BF16),
            jax.ShapeDtypeStruct(cache_kt.shape, F32),
            jax.ShapeDtypeStruct(cache_vt.shape, F32),
        ],
        scratch_shapes=[
            pltpu.VMEM((bt * N_HEADS * t_new, 2 * w), F32),
            pltpu.VMEM((bt * N_HEADS * t_new, 2 * w), BF16),
            pltpu.VMEM((bt * N_HEADS * t_new, LANES), F32),
            pltpu.VMEM((rows, ATTN_WIDTH), F32),
        ],
        compiler_params=_params("parallel"),
        name="attn_sample",
    )(sink_rows, q, kt_new, vt_new, cache_kt, cache_vt)


def _ln_swish(y, g, b):
    mu = jnp.mean(y, axis=-1, keepdims=True)
    d = y - mu
    var = jnp.mean(d * d, axis=-1, keepdims=True)
    yn = d * lax.rsqrt(var + LN_EPS) * g + b
    return yn * jax.nn.sigmoid(yn)


CONV_HALO = 32
CONV_SEG = 252
CONV_SEG_STEP = 12


def _conv_prompt_kernel(seq, u_ref, meta_ref, wb_ref, o_ref, win_ref):
    pad = CONV_HALO - N_META
    for c in range(N_SLAB):
        win_ref[c, 0:pad] = jnp.zeros((pad, LANES), F32)
        win_ref[c, pad:CONV_HALO] = meta_ref[c]
        win_ref[c, CONV_HALO:CONV_HALO + seq] = u_ref[c]
    first = CONV_HALO - (CONV_K - 1)

    rem_base = SUBLANES * CONV_SEG
    rem_seg = (seq - rem_base) // SUBLANES
    n_main = CONV_SEG // CONV_SEG_STEP

    def taps(c, wts, base, seg, n_g, i0):
        accs = [None] * n_g
        for m in range(n_g + CONV_K - 1):
            v = win_ref[c, pl.ds(base + first + i0 + m, SUBLANES, stride=seg), :]
            for g in range(max(0, m - CONV_K + 1), min(n_g, m + 1)):
                term = v * wts[m - g]
                accs[g] = term if accs[g] is None else accs[g] + term
        for g in range(n_g):
            o_ref[c, pl.ds(base + i0 + g, SUBLANES, stride=seg), :] = accs[g]

    for c in range(N_SLAB):
        wts = [wb_ref[tap, c] for tap in range(CONV_K)]

        def body(ci, carry, c=c, wts=wts):
            taps(c, wts, 0, CONV_SEG, CONV_SEG_STEP, ci * CONV_SEG_STEP)
            return carry

        lax.fori_loop(0, n_main, body, 0)
        taps(c, wts, rem_base, rem_seg, rem_seg, 0)


def _conv_prompt(u_slabs, u_meta, wb, n_batch, seq):
    assert CONV_SEG % CONV_SEG_STEP == 0 and seq > SUBLANES * CONV_SEG
    assert (seq - SUBLANES * CONV_SEG) % SUBLANES == 0
    blk = pl.BlockSpec((N_SLAB, seq, LANES), lambda b: (0, b, 0))
    return pl.pallas_call(
        functools.partial(_conv_prompt_kernel, seq),
        grid=(n_batch,),
        in_specs=[
            blk,
            _const_spec((N_SLAB, N_META, LANES)),
            _const_spec((CONV_K, N_SLAB, SUBLANES, LANES)),
        ],
        out_specs=blk,
        out_shape=jax.ShapeDtypeStruct((N_SLAB, n_batch * seq, LANES), F32),
        scratch_shapes=[pltpu.VMEM((N_SLAB, CONV_HALO + seq, LANES), F32)],
        compiler_params=_params("parallel"),
        name="conv_prompt",
    )(u_slabs, u_meta, wb)


def _conv_sample_kernel(t_new, st_ref, u_ref, wb_ref, o_ref, so_ref):
    hist = st_ref.shape[0]
    bb = st_ref.shape[1]
    so_ref[0:hist - t_new] = st_ref[t_new:hist]

    for c in range(N_SLAB):
        lanes = slice(c * LANES, (c + 1) * LANES)
        for t in range(t_new):
            so_ref[hist - t_new + t, :, lanes] = u_ref[c, pl.ds(t, bb, stride=t_new), :]

        def src(k, rc, c=c, lanes=lanes):
            if k < hist:
                return st_ref[k, pl.ds(pl.multiple_of(rc * SUBLANES, SUBLANES), SUBLANES), lanes]
            return u_ref[c, pl.ds(rc * SUBLANES * t_new + (k - hist), SUBLANES, stride=t_new), :]

        def per_rows(rc, carry, c=c, src=src):
            for t in range(t_new):
                acc = None
                for tap in range(CONV_K):
                    term = src(t + tap, rc) * wb_ref[tap, c]
                    acc = term if acc is None else acc + term
                o_ref[c, pl.ds(rc * SUBLANES * t_new + t, SUBLANES, stride=t_new), :] = acc
            return carry

        lax.fori_loop(0, bb // SUBLANES, per_rows, 0)


def _conv_sample(state_t, u_slabs, wb, t_new, bb):
    hist, nb, _ = state_t.shape
    rows = bb * t_new
    slab_spec = pl.BlockSpec((N_SLAB, rows, LANES), lambda s: (0, s, 0))
    return pl.pallas_call(
        functools.partial(_conv_sample_kernel, t_new),
        grid=(nb // bb,),
        in_specs=[
            pl.BlockSpec((hist, bb, CONV_CH), lambda s: (0, s, 0)),
            slab_spec,
            _const_spec((CONV_K, N_SLAB, SUBLANES, LANES)),
        ],
        out_specs=[slab_spec, pl.BlockSpec((hist, bb, CONV_CH), lambda s: (0, s, 0))],
        out_shape=[
            jax.ShapeDtypeStruct((N_SLAB, nb * t_new, LANES), F32),
            jax.ShapeDtypeStruct(state_t.shape, F32),
        ],
        compiler_params=_params("parallel"),
        name="conv_sample",
    )(state_t, u_slabs, wb)


def _outproj_kernel(x_ref, a_ref, c_ref, bdw_ref, g_ref, b_ref, wo_ref, h_ref, cbf_ref):
    y = jnp.concatenate([c_ref[c] for c in range(N_SLAB)], axis=1)
    cbf_ref[...] = _ln_swish(y + bdw_ref[...], g_ref[...], b_ref[...]).astype(BF16)
    mix = jnp.dot(a_ref[...], wo_ref[0:ATTN_WIDTH].astype(BF16), preferred_element_type=F32)
    mix = mix + jnp.dot(cbf_ref[...], wo_ref[ATTN_WIDTH:].astype(BF16), preferred_element_type=F32)
    h_ref[...] = x_ref[...] + mix


def _outproj(x, attn, conv_taps, b_dw, ln_g, ln_b, w_out, tm):
    m = x.shape[0]
    return pl.pallas_call(
        _outproj_kernel,
        grid=(m // tm,),
        in_specs=[
            pl.BlockSpec((tm, D_MODEL), lambda i: (i, 0)),
            pl.BlockSpec((tm, ATTN_WIDTH), lambda i: (i, 0)),
            pl.BlockSpec((N_SLAB, tm, LANES), lambda i: (0, i, 0)),
            _const_spec((1, CONV_CH)),
            _const_spec((1, CONV_CH)),
            _const_spec((1, CONV_CH)),
            _const_spec((D_MODEL, D_MODEL)),
        ],
        out_specs=pl.BlockSpec((tm, D_MODEL), lambda i: (i, 0)),
        out_shape=jax.ShapeDtypeStruct((m, D_MODEL), F32),
        scratch_shapes=[pltpu.VMEM((tm, CONV_CH), BF16)],
        compiler_params=_params("parallel"),
        name="outproj",
    )(x, attn, conv_taps, b_dw, ln_g, ln_b, w_out)


def _mlp_kernel(h_ref, g2_ref, wu_ref, wd_ref, gf_ref, y_ref, xn_ref):
    f = pl.program_id(1)

    @pl.when(f == 0)
    def _():
        h1 = h_ref[...]
        y_ref[...] = h1
        xn_ref[...] = _rms(h1, g2_ref[...]).astype(BF16)

    up = jnp.dot(xn_ref[...], wu_ref[...].astype(BF16), preferred_element_type=F32)
    act = jnp.maximum(up, 0.0)
    act = (act * act).astype(BF16)
    y_ref[...] += jnp.dot(act, wd_ref[...].astype(BF16), preferred_element_type=F32)

    @pl.when(f == pl.num_programs(1) - 1)
    def _():
        y_ref[...] = _rms(y_ref[...], gf_ref[...])


def _mlp(h1, g2, wu_bf, wd_bf, gf, tm, tf):
    m = h1.shape[0]
    return pl.pallas_call(
        _mlp_kernel,
        grid=(m // tm, D_FF // tf),
        in_specs=[
            pl.BlockSpec((tm, D_MODEL), lambda i, f: (i, 0)),
            _const_spec((1, D_MODEL)),
            pl.BlockSpec((D_MODEL, tf), lambda i, f: (0, f)),
            pl.BlockSpec((tf, D_MODEL), lambda i, f: (f, 0)),
            _const_spec((1, D_MODEL)),
        ],
        out_specs=pl.BlockSpec((tm, D_MODEL), lambda i, f: (i, 0)),
        out_shape=jax.ShapeDtypeStruct((m, D_MODEL), F32),
        scratch_shapes=[pltpu.VMEM((tm, D_MODEL), BF16)],
        compiler_params=pltpu.CompilerParams(dimension_semantics=("parallel", "arbitrary"),
                                             vmem_limit_bytes=VMEM_LIMIT_MLP),
        name="mlp",
    )(h1, g2, wu_bf, wd_bf, gf)


def kernel(x_prompt, x_sample, cache_k, cache_v, state_conv, meta_tokens, norm_mix, w_in, attn_sinks,
           w_dw, b_dw, conv_ln_g, conv_ln_b, w_out, norm_mlp, w_up, w_down, norm_final):
    n_batch, seq, _ = x_prompt.shape
    n_dec, t_new, _ = x_sample.shape
    depth = w_in.shape[0]
    window = cache_k.shape[2]
    assert depth == 1 and seq % BLOCK == 0 and t_new == SUBLANES and window == WINDOW
    l = 0
    n_blk = seq // BLOCK
    tm = 512
    tm_mlp = 1024
    tf = 512
    bt = LANES // t_new

    g_mix = norm_mix[l][None, :]
    g_mlp = norm_mlp[l][None, :]
    g_fin = norm_final[None, :]
    sinks = attn_sinks[l].astype(F32) * LOG2_E
    bdw = b_dw[l][None, :]
    ln_g = conv_ln_g[l][None, :]
    ln_b = conv_ln_b[l][None, :]
    wb_slabs = jnp.broadcast_to(w_dw[l].reshape(CONV_K, N_SLAB, 1, LANES), (CONV_K, N_SLAB, SUBLANES, LANES))

    rope_p = _rope_table(N_META + jnp.arange(seq, dtype=jnp.int32))
    rope_s = _rope_table(PAST_LEN + (jnp.arange(tm, dtype=jnp.int32) % t_new))
    ropet_s = jnp.tile(rope_s[:, :, :HEAD_DIM].transpose(0, 2, 1), (1, N_KV_HEADS, 1))
    rope_m = _rope_table(jnp.arange(N_META, dtype=jnp.int32))

    xp = x_prompt.reshape(n_batch * seq, D_MODEL)
    xs = x_sample.reshape(n_dec * t_new, D_MODEL)

    q_p, k_p, v_p, u_p = _inproj_prompt(xp, g_mix, w_in[l], rope_p, tm)
    q_s, kt_s, vt_s, u_s, k_m, v_m, u_m = _inproj_sample(xs, g_mix, w_in[l], rope_s, ropet_s,
                                                         meta_tokens.astype(F32), rope_m, tm)

    attn_p = _attn_prompt(sinks, q_p, k_p, v_p, k_m, v_m, n_batch, n_blk)
    conv_p = _conv_prompt(u_p, u_m, wb_slabs, n_batch, seq)

    ckt = cache_k[l].transpose(0, 2, 3, 1).reshape(n_dec, KV_WIDTH, window)
    cvt = cache_v[l].transpose(0, 2, 3, 1).reshape(n_dec, KV_WIDTH, window)
    sink_rows = jnp.broadcast_to(jnp.repeat(sinks, t_new)[:, None], (N_HEADS * t_new, LANES))
    attn_s, kt_out, vt_out = _attn_sample(sink_rows, q_s, kt_s, vt_s, ckt, cvt, t_new, bt)
    state_t = state_conv[l].transpose(1, 0, 2)
    conv_s, state_out_t = _conv_sample(state_t, u_s, wb_slabs, t_new, 32)

    h_p = _outproj(xp, attn_p, conv_p, bdw, ln_g, ln_b, w_out[l], tm)
    h_s = _outproj(xs, attn_s, conv_s, bdw, ln_g, ln_b, w_out[l], tm)
    y_p = _mlp(h_p, g_mlp, w_up[l], w_down[l], g_fin, tm_mlp, tf)
    y_s = _mlp(h_s, g_mlp, w_up[l], w_down[l], g_fin, tm_mlp, tf)

    y_prompt = y_p.reshape(n_batch, seq, D_MODEL)
    y_sample = y_s.reshape(n_dec, t_new, D_MODEL)
    k_last = k_p.reshape(n_batch, seq, KV_WIDTH)[:, seq - WINDOW:]
    v_last = v_p.reshape(n_batch, seq, KV_WIDTH)[:, seq - WINDOW:]
    new_k_prompt = k_last.reshape(n_batch, WINDOW, N_KV_HEADS, HEAD_DIM)[None]
    new_v_prompt = v_last.reshape(n_batch, WINDOW, N_KV_HEADS, HEAD_DIM)[None]
    u_last = u_p.reshape(N_SLAB, n_batch, seq, LANES)[:, :, seq - (CONV_K - 1):]
    new_conv_prompt = u_last.transpose(1, 2, 0, 3).reshape(n_batch, CONV_K - 1, CONV_CH)[None]
    new_k_sample = kt_out.reshape(n_dec, N_KV_HEADS, HEAD_DIM, window).transpose(0, 3, 1, 2)[None]
    new_v_sample = vt_out.reshape(n_dec, N_KV_HEADS, HEAD_DIM, window).transpose(0, 3, 1, 2)[None]
    new_conv_sample = state_out_t.transpose(1, 0, 2)[None]
    return (y_prompt, y_sample, new_k_prompt, new_v_prompt, new_conv_prompt,
            new_k_sample, new_v_sample, new_conv_sample)
```

```python
import functools

import jax
import jax.numpy as jnp
from jax import lax
from jax.experimental import pallas as pl
from jax.experimental.pallas import tpu as pltpu

D_MODEL = 2048
N_META = 16
HEAD_DIM = 64
ATTN_WIDTH = D_MODEL // 2
CONV_CH = D_MODEL - ATTN_WIDTH
N_HEADS = ATTN_WIDTH // HEAD_DIM
N_KV_HEADS = 4
GQA_GROUP = N_HEADS // N_KV_HEADS
KV_WIDTH = N_KV_HEADS * HEAD_DIM
IN_COLS = ATTN_WIDTH + 2 * KV_WIDTH + 2 * CONV_CH
WINDOW = 128
BLOCK = 128
ROPE_THETA = 500000.0
ROT_DIM = HEAD_DIM // 4
CONV_K = 31
D_FF = 4 * D_MODEL
RMS_EPS = 1e-6
LN_EPS = 1e-5
MASK_VALUE = -1e30
PAST_LEN = 8192

LANES = 128
SUBLANES = 8
VMEM_LIMIT = 56 * 1024 * 1024
VMEM_LIMIT_MLP = 62 * 1024 * 1024
N_SLAB = CONV_CH // LANES

F32 = jnp.float32
BF16 = jnp.bfloat16

K_OFF = ATTN_WIDTH
V_OFF = ATTN_WIDTH + KV_WIDTH
A_OFF = ATTN_WIDTH + 2 * KV_WIDTH
B_OFF = A_OFF + CONV_CH
NT_DIMS = (((1,), (1,)), ((), ()))
LOG2_E = 1.4426950408889634
Q_SCALE = HEAD_DIM ** -0.5 * LOG2_E


def _params(*sem):
    return pltpu.CompilerParams(dimension_semantics=sem, vmem_limit_bytes=VMEM_LIMIT)


def _const_spec(shape):
    nd = len(shape)
    return pl.BlockSpec(shape, lambda *_: (0,) * nd, pipeline_mode=pl.Buffered(1))


def _rms(x, g):
    ms = jnp.mean(x * x, axis=-1, keepdims=True)
    return (x * lax.rsqrt(ms + RMS_EPS)) * g


def _rope_rows(z, rope_ref):
    return (z * rope_ref[0] + pltpu.roll(z, LANES - ROT_DIM // 2, 1) * rope_ref[1]
            + pltpu.roll(z, ROT_DIM // 2, 1) * rope_ref[2])


def _proj(xn_ref, w_ref, off, width):
    return jnp.dot(xn_ref[...], w_ref[:, off:off + width].astype(BF16), preferred_element_type=F32)


def _write_q(xn_ref, w_ref, rope_ref, q_ref):
    half = ATTN_WIDTH // 2
    for c in range(2):
        acc = _proj(xn_ref, w_ref, c * half, half)
        for cc in range(half // LANES):
            z = _rope_rows(acc[:, cc * LANES:(cc + 1) * LANES], rope_ref) * Q_SCALE
            q_ref[:, c * half + cc * LANES:c * half + (cc + 1) * LANES] = z.astype(q_ref.dtype)


def _glu(xn_ref, w_ref, c):
    half = CONV_CH // 2
    a = _proj(xn_ref, w_ref, A_OFF + c * half, half)
    b = _proj(xn_ref, w_ref, B_OFF + c * half, half)
    return a * jax.nn.sigmoid(b)


def _inproj_prompt_kernel(x_ref, g_ref, w_ref, rope_ref, q_ref, k_ref, v_ref, u_ref, xn_ref):
    xn_ref[...] = _rms(x_ref[...], g_ref[...]).astype(BF16)
    _write_q(xn_ref, w_ref, rope_ref, q_ref)
    _write_kvu(xn_ref, w_ref, rope_ref, k_ref, v_ref, u_ref)


def _write_kvu(xn_ref, w_ref, rope_ref, k_ref, v_ref, u_ref):
    acc = _proj(xn_ref, w_ref, K_OFF, 2 * KV_WIDTH)
    for cc in range(KV_WIDTH // LANES):
        k_ref[:, cc * LANES:(cc + 1) * LANES] = _rope_rows(acc[:, cc * LANES:(cc + 1) * LANES], rope_ref)
    v_ref[...] = acc[:, KV_WIDTH:]
    per = N_SLAB // 2
    for c in range(2):
        u = _glu(xn_ref, w_ref, c)
        for cc in range(per):
            u_ref[c * per + cc] = u[:, cc * LANES:(cc + 1) * LANES]


def _inproj_prompt(x, g, w_in, rope_tab, tm):
    m = x.shape[0]
    nrt = rope_tab.shape[1] // tm
    return pl.pallas_call(
        _inproj_prompt_kernel,
        grid=(m // tm,),
        in_specs=[
            pl.BlockSpec((tm, D_MODEL), lambda i: (i, 0)),
            _const_spec((1, D_MODEL)),
            _const_spec((D_MODEL, IN_COLS)),
            pl.BlockSpec((3, tm, LANES), lambda i: (0, i % nrt, 0)),
        ],
        out_specs=[
            pl.BlockSpec((tm, ATTN_WIDTH), lambda i: (i, 0)),
            pl.BlockSpec((tm, KV_WIDTH), lambda i: (i, 0)),
            pl.BlockSpec((tm, KV_WIDTH), lambda i: (i, 0)),
            pl.BlockSpec((N_SLAB, tm, LANES), lambda i: (0, i, 0)),
        ],
        out_shape=[
            jax.ShapeDtypeStruct((m, ATTN_WIDTH), BF16),
            jax.ShapeDtypeStruct((m, KV_WIDTH), F32),
            jax.ShapeDtypeStruct((m, KV_WIDTH), F32),
            jax.ShapeDtypeStruct((N_SLAB, m, LANES), F32),
        ],
        scratch_shapes=[pltpu.VMEM((tm, D_MODEL), BF16)],
        compiler_params=_params("parallel"),
        name="inproj_prompt",
    )(x, g, w_in, rope_tab)


def _inproj_sample_kernel(x_ref, g_ref, w_ref, rope_ref, ropet_ref, meta_ref, ropem_ref,
                          q_ref, kt_ref, vt_ref, u_ref, km_ref, vm_ref, um_ref, xn_ref, xm_ref):
    @pl.when(pl.program_id(0) == 0)
    def _():
        xm_ref[...] = _rms(meta_ref[...], g_ref[...]).astype(BF16)
        _write_kvu(xm_ref, w_ref, ropem_ref, km_ref, vm_ref, um_ref)

    xn_ref[...] = _rms(x_ref[...], g_ref[...]).astype(BF16)
    _write_q(xn_ref, w_ref, rope_ref, q_ref)
    kvt = _proj(xn_ref, w_ref, K_OFF, 2 * KV_WIDTH).T
    kt = kvt[0:KV_WIDTH]
    sh = ROT_DIM // 2
    up = jnp.concatenate([kt[sh:], kt[:sh]], axis=0)
    dn = jnp.concatenate([kt[-sh:], kt[:-sh]], axis=0)
    kt_ref[...] = kt * ropet_ref[0] + up * ropet_ref[1] + dn * ropet_ref[2]
    vt_ref[...] = kvt[KV_WIDTH:]
    per = N_SLAB // 2
    for c in range(2):
        u = _glu(xn_ref, w_ref, c)
        for cc in range(per):
            u_ref[c * per + cc] = u[:, cc * LANES:(cc + 1) * LANES]


def _inproj_sample(x, g, w_in, rope_tab, ropet_tab, meta, rope_meta, tm):
    m = x.shape[0]
    n_meta = meta.shape[0]
    return pl.pallas_call(
        _inproj_sample_kernel,
        grid=(m // tm,),
        in_specs=[
            pl.BlockSpec((tm, D_MODEL), lambda i: (i, 0)),
            _const_spec((1, D_MODEL)),
            _const_spec((D_MODEL, IN_COLS)),
            _const_spec((3, tm, LANES)),
            _const_spec((3, KV_WIDTH, tm)),
            _const_spec((n_meta, D_MODEL)),
            _const_spec((3, n_meta, LANES)),
        ],
        out_specs=[
            pl.BlockSpec((tm, ATTN_WIDTH), lambda i: (i, 0)),
            pl.BlockSpec((KV_WIDTH, tm), lambda i: (0, i)),
            pl.BlockSpec((KV_WIDTH, tm), lambda i: (0, i)),
            pl.BlockSpec((N_SLAB, tm, LANES), lambda i: (0, i, 0)),
            pl.BlockSpec((n_meta, KV_WIDTH), lambda i: (0, 0)),
            pl.BlockSpec((n_meta, KV_WIDTH), lambda i: (0, 0)),
            pl.BlockSpec((N_SLAB, n_meta, LANES), lambda i: (0, 0, 0)),
        ],
        out_shape=[
            jax.ShapeDtypeStruct((m, ATTN_WIDTH), F32),
            jax.ShapeDtypeStruct((KV_WIDTH, m), F32),
            jax.ShapeDtypeStruct((KV_WIDTH, m), F32),
            jax.ShapeDtypeStruct((N_SLAB, m, LANES), F32),
            jax.ShapeDtypeStruct((n_meta, KV_WIDTH), F32),
            jax.ShapeDtypeStruct((n_meta, KV_WIDTH), F32),
            jax.ShapeDtypeStruct((N_SLAB, n_meta, LANES), F32),
        ],
        scratch_shapes=[pltpu.VMEM((tm, D_MODEL), BF16), pltpu.VMEM((n_meta, D_MODEL), BF16)],
        compiler_params=_params("arbitrary"),
        name="inproj_sample",
    )(x, g, w_in, rope_tab, ropet_tab, meta, rope_meta)


def _rope_table(pos):
    half = ROT_DIM // 2
    inv = jnp.power(jnp.float32(ROPE_THETA), -jnp.arange(half, dtype=F32) * 2.0 / ROT_DIM)
    ang = pos.astype(F32)[:, None] * inv[None, :]
    cos, sin = jnp.cos(ang), jnp.sin(ang)
    t = pos.shape[0]
    rest = HEAD_DIM - ROT_DIM
    one = jnp.ones((t, rest), F32)
    zero = jnp.zeros((t, rest), F32)
    zh = jnp.zeros((t, half), F32)
    c = jnp.concatenate([cos, cos, one], axis=1)
    s_lo = jnp.concatenate([-sin, zh, zero], axis=1)
    s_hi = jnp.concatenate([zh, sin, zero], axis=1)
    tab = jnp.stack([c, s_lo, s_hi])
    return jnp.tile(tab, (1, 1, LANES // HEAD_DIM))


def _attn_prompt_kernel(sink_ref, q_ref, kc_ref, kp_ref, vc_ref, vp_ref, km_ref, vm_ref, o_ref,
                        kk_ref, vv_ref, s_ref, p_ref, inv_ref):
    n = pl.program_id(1)

    @pl.when(n == 0)
    def _():
        pad = jnp.zeros((BLOCK - N_META, KV_WIDTH), F32)
        kk_ref[0:BLOCK - N_META] = pad
        vv_ref[0:BLOCK - N_META] = pad
        kk_ref[BLOCK - N_META:BLOCK] = km_ref[...]
        vv_ref[BLOCK - N_META:BLOCK] = vm_ref[...]

    @pl.when(n > 0)
    def _():
        kk_ref[0:BLOCK] = kp_ref[...]
        vv_ref[0:BLOCK] = vp_ref[...]

    kk_ref[BLOCK:2 * BLOCK] = kc_ref[...]
    vv_ref[BLOCK:2 * BLOCK] = vc_ref[...]
    vv_ref[0:1, :] = jnp.zeros((1, KV_WIDTH), F32)

    lo = jnp.where(n == 0, BLOCK - N_META, 0)
    r = lax.broadcasted_iota(jnp.int32, (BLOCK, 2 * BLOCK), 0)
    j = lax.broadcasted_iota(jnp.int32, (BLOCK, 2 * BLOCK), 1)
    mask3 = ((j > r) & (j <= r + BLOCK) & (j >= lo)).reshape(BLOCK // SUBLANES, SUBLANES, 2 * BLOCK)
    col0 = lax.broadcasted_iota(jnp.int32, (1, SUBLANES, 2 * BLOCK), 2) == 0
    low_half = lax.broadcasted_iota(jnp.int32, (BLOCK, LANES), 1) < HEAD_DIM
    low_half2 = lax.broadcasted_iota(jnp.int32, (2 * BLOCK, LANES), 1) < HEAD_DIM

    def dup_halves(ref, h):
        xf = ref[:, (h // 2) * LANES:(h // 2 + 1) * LANES]
        x_sw = pltpu.roll(xf, HEAD_DIM, 1)
        both = jnp.where(low_half2, xf, x_sw) if h % 2 == 0 else jnp.where(low_half2, x_sw, xf)
        return both.astype(BF16)

    rows_kv = GQA_GROUP * BLOCK
    for h in range(N_KV_HEADS):
        qa = q_ref[:, (2 * h) * LANES:(2 * h + 1) * LANES]
        qb = q_ref[:, (2 * h + 1) * LANES:(2 * h + 2) * LANES]
        zero = jnp.zeros_like(qa)
        lhs = jnp.concatenate([jnp.where(low_half, qa, zero), jnp.where(low_half, qb, zero),
                               jnp.where(low_half, zero, qa), jnp.where(low_half, zero, qb)], axis=0)
        s_all = lax.dot_general(lhs, dup_halves(kk_ref, h), NT_DIMS, preferred_element_type=F32)
        heads = (4 * h, 4 * h + 2, 4 * h + 1, 4 * h + 3)
        for g in range(GQA_GROUP):
            fill = jnp.where(col0, sink_ref[heads[g]], MASK_VALUE)
            s = jnp.where(mask3, s_all[g * BLOCK:(g + 1) * BLOCK].reshape(mask3.shape), fill)
            s_ref[h * rows_kv + g * BLOCK:h * rows_kv + (g + 1) * BLOCK] = s.reshape(BLOCK, 2 * BLOCK)

    s = s_ref[...]
    mx = jnp.max(s, axis=-1, keepdims=True)
    e = jnp.exp2(s - mx)
    p_ref[...] = e.astype(BF16)
    inv_ref[...] = jnp.broadcast_to(1.0 / jnp.sum(e, axis=-1, keepdims=True), inv_ref.shape)

    for h in range(N_KV_HEADS):
        base = h * rows_kv
        o_all = jnp.dot(p_ref[base:base + rows_kv], dup_halves(vv_ref, h), preferred_element_type=F32)
        o = [o_all[g * BLOCK:(g + 1) * BLOCK] * inv_ref[base + g * BLOCK:base + (g + 1) * BLOCK]
             for g in range(GQA_GROUP)]
        o_ref[:, (2 * h) * LANES:(2 * h + 1) * LANES] = jnp.where(low_half, o[0], o[2]).astype(o_ref.dtype)
        o_ref[:, (2 * h + 1) * LANES:(2 * h + 2) * LANES] = jnp.where(low_half, o[1], o[3]).astype(o_ref.dtype)


def _attn_prompt(sinks, q, k, v, k_meta, v_meta, n_batch, n_blk):
    kv_cur = pl.BlockSpec((BLOCK, KV_WIDTH), lambda b, n: (b * n_blk + n, 0))
    kv_prev = pl.BlockSpec((BLOCK, KV_WIDTH), lambda b, n: (b * n_blk + jnp.maximum(n - 1, 0), 0))
    return pl.pallas_call(
        _attn_prompt_kernel,
        grid=(n_batch, n_blk),
        in_specs=[
            pl.BlockSpec(memory_space=pltpu.SMEM),
            pl.BlockSpec((BLOCK, ATTN_WIDTH), lambda b, n: (b * n_blk + n, 0)),
            kv_cur, kv_prev, kv_cur, kv_prev,
            _const_spec((N_META, KV_WIDTH)),
            _const_spec((N_META, KV_WIDTH)),
        ],
        out_specs=pl.BlockSpec((BLOCK, ATTN_WIDTH), lambda b, n: (b * n_blk + n, 0)),
        out_shape=jax.ShapeDtypeStruct((n_batch * n_blk * BLOCK, ATTN_WIDTH), BF16),
        scratch_shapes=[
            pltpu.VMEM((2 * BLOCK, KV_WIDTH), F32),
            pltpu.VMEM((2 * BLOCK, KV_WIDTH), F32),
            pltpu.VMEM((N_HEADS * BLOCK, 2 * BLOCK), F32),
            pltpu.VMEM((N_HEADS * BLOCK, 2 * BLOCK), BF16),
            pltpu.VMEM((N_HEADS * BLOCK, LANES), F32),
        ],
        compiler_params=_params("parallel", "arbitrary"),
        name="attn_prompt",
    )(sinks, q, k, k, v, v, k_meta, v_meta)


def _attn_sample_kernel(t_new, bt, sink_ref, q_ref, kn_ref, vn_ref, ck_ref, cv_ref,
                        o_ref, ko_ref, vo_ref, s_ref, p_ref, inv_ref, osc_ref):
    nrow = N_HEADS * t_new
    w = ck_ref.shape[2]
    assert bt * t_new == LANES and w == LANES and nrow == LANES
    zero_blk = jnp.zeros((t_new, LANES), F32)

    r = lax.broadcasted_iota(jnp.int32, (nrow, 2 * w), 0) % t_new
    j = lax.broadcasted_iota(jnp.int32, (nrow, 2 * w), 1)
    cache_ok = (j > r) & (j < w)
    fill = jnp.where(j == 0, sink_ref[...][:, 0:1], MASK_VALUE)
    key0 = lax.broadcasted_iota(jnp.int32, (bt * nrow, w), 1) == 0
    low_half =lax.broadcasted_iota(jnp.int32, (t_new, LANES), 1) < HEAD_DIM
    tail = lax.broadcasted_iota(jnp.int32, (KV_WIDTH, LANES), 1) >= w - t_new
    kn = kn_ref[...]
    vn = vn_ref[...]
    kn_bf = kn.astype(BF16)
    vn_bf = vn.astype(BF16)

    for i in range(bt):
        new_shift = (w - t_new - i * t_new) % LANES
        ko_ref[i] = jnp.where(tail, pltpu.roll(kn, new_shift, 1), pltpu.roll(ck_ref[i], w - t_new, 1))
        vo_ref[i] = jnp.where(tail, pltpu.roll(vn, new_shift, 1), pltpu.roll(cv_ref[i], w - t_new, 1))

    for i in range(bt):
        qrows = []
        for t in range(N_HEADS):
            h = t // GQA_GROUP
            qv = q_ref[i * t_new:(i + 1) * t_new, (t // 2) * LANES:(t // 2 + 1) * LANES]
            if t % 2 != h % 2:
                qv = pltpu.roll(qv, HEAD_DIM, 1)
            keep = low_half if h % 2 == 0 else jnp.logical_not(low_half)
            cols = [zero_blk] * (KV_WIDTH // LANES)
            cols[h // 2] = jnp.where(keep, qv, 0.0)
            qrows.append(jnp.concatenate(cols, axis=1))
        qrow = jnp.concatenate(qrows, axis=0).astype(BF16)

        rhs_k = jnp.concatenate([ck_ref[i].astype(BF16), kn_bf], axis=1)
        s = jnp.dot(qrow, rhs_k, preferred_element_type=F32)
        own = w + i * t_new
        mask = cache_ok | ((j >= own) & (j <= own + r))
        s_ref[i * nrow:(i + 1) * nrow] = jnp.where(mask, s, fill)

    s = s_ref[...]
    mx = jnp.max(s, axis=-1, keepdims=True)
    e = jnp.exp2(s - mx)
    den = jnp.sum(e, axis=-1, keepdims=True)
    p_ref[:, 0:w] = jnp.where(key0, 0.0, e[:, 0:w]).astype(BF16)
    p_ref[:, w:2 * w] = e[:, w:2 * w].astype(BF16)
    inv_ref[...] = jnp.broadcast_to(1.0 / den, inv_ref.shape)

    for i in range(bt):
        rhs_v = jnp.concatenate([cv_ref[i].astype(BF16), vn_bf], axis=1)
        o_all = lax.dot_general(p_ref[i * nrow:(i + 1) * nrow], rhs_v, NT_DIMS, preferred_element_type=F32)
        for t2 in range(N_HEADS // 2):
            parts = []
            for side in range(2):
                t = 2 * t2 + side
                h = t // GQA_GROUP
                blk = o_all[t * t_new:(t + 1) * t_new, (h // 2) * LANES:(h // 2 + 1) * LANES]
                blk = blk * inv_ref[i * nrow + t * t_new:i * nrow + (t + 1) * t_new, :]
                if t % 2 != h % 2:
                    blk = pltpu.roll(blk, HEAD_DIM, 1)
                parts.append(blk)
            osc_ref[i * t_new:(i + 1) * t_new, t2 * LANES:(t2 + 1) * LANES] = (
                jnp.where(low_half, parts[0], parts[1]))

    o_ref[...] = osc_ref[...].astype(o_ref.dtype)


def _attn_sample(sink_rows, q, kt_new, vt_new, cache_kt, cache_vt, t_new, bt):
    nb, _, w = cache_kt.shape
    rows = bt * t_new
    cache_spec = pl.BlockSpec((bt, KV_WIDTH, w), lambda s: (s, 0, 0))
    new_spec = pl.BlockSpec((KV_WIDTH, rows), lambda s: (0, s))
    return pl.pallas_call(
        functools.partial(_attn_sample_kernel, t_new, bt),
        grid=(nb // bt,),
        in_specs=[
            _const_spec(sink_rows.shape),
            pl.BlockSpec((rows, ATTN_WIDTH), lambda s: (s, 0)),
            new_spec, new_spec, cache_spec, cache_spec,
        ],
        out_specs=[pl.BlockSpec((rows, ATTN_WIDTH), lambda s: (s, 0)), cache_spec, cache_spec],
        out_shape=[
            jax.ShapeDtypeStruct((nb * t_new, ATTN_WIDTH), BF16),
            jax.ShapeDtypeStruct(cache_kt.shape, F32),
            jax.ShapeDtypeStruct(cache_vt.shape, F32),
        ],
        scratch_shapes=[
            pltpu.VMEM((bt * N_HEADS * t_new, 2 * w), F32),
            pltpu.VMEM((bt * N_HEADS * t_new, 2 * w), BF16),
            pltpu.VMEM((bt * N_HEADS * t_new, LANES), F32),
            pltpu.VMEM((rows, ATTN_WIDTH), F32),
        ],
        compiler_params=_params("parallel"),
        name="attn_sample",
    )(sink_rows, q, kt_new, vt_new, cache_kt, cache_vt)


def _ln_swish(y, g, b):
    mu = jnp.mean(y, axis=-1, keepdims=True)
    d = y - mu
    var = jnp.mean(d * d, axis=-1, keepdims=True)
    yn = d * lax.rsqrt(var + LN_EPS) * g + b
    return yn * jax.nn.sigmoid(yn)


CONV_HALO = 32
CONV_SEG = 252
CONV_SEG_STEP = 126


def _conv_prompt_kernel(seq, u_ref, meta_ref, wb_ref, o_ref, win_ref):
    pad = CONV_HALO - N_META
    for c in range(N_SLAB):
        win_ref[c, 0:pad] = jnp.zeros((pad, LANES), F32)
        win_ref[c, pad:CONV_HALO] = meta_ref[c]
        win_ref[c, CONV_HALO:CONV_HALO + seq] = u_ref[c]
    first = CONV_HALO - (CONV_K - 1)

    rem_base = SUBLANES * CONV_SEG
    rem_seg = (seq - rem_base) // SUBLANES
    n_main = CONV_SEG // CONV_SEG_STEP

    def taps(c, wts, base, seg, n_g, i0):
        accs = [None] * n_g
        for m in range(n_g + CONV_K - 1):
            v = win_ref[c, pl.ds(base + first + i0 + m, SUBLANES, stride=seg), :]
            for g in range(max(0, m - CONV_K + 1), min(n_g, m + 1)):
                term = v * wts[m - g]
                accs[g] = term if accs[g] is None else accs[g] + term
        for g in range(n_g):
            o_ref[c, pl.ds(base + i0 + g, SUBLANES, stride=seg), :] = accs[g]

    for c in range(N_SLAB):
        wts = [wb_ref[tap, c] for tap in range(CONV_K)]

        def body(ci, carry, c=c, wts=wts):
            taps(c, wts, 0, CONV_SEG, CONV_SEG_STEP, ci * CONV_SEG_STEP)
            return carry

        lax.fori_loop(0, n_main, body, 0)
        taps(c, wts, rem_base, rem_seg, rem_seg, 0)


def _conv_prompt(u_slabs, u_meta, wb, n_batch, seq):
    assert CONV_SEG % CONV_SEG_STEP == 0 and seq > SUBLANES * CONV_SEG
    assert (seq - SUBLANES * CONV_SEG) % SUBLANES == 0
    blk = pl.BlockSpec((N_SLAB, seq, LANES), lambda b: (0, b, 0))
    return pl.pallas_call(
        functools.partial(_conv_prompt_kernel, seq),
        grid=(n_batch,),
        in_specs=[
            blk,
            _const_spec((N_SLAB, N_META, LANES)),
            _const_spec((CONV_K, N_SLAB, SUBLANES, LANES)),
        ],
        out_specs=blk,
        out_shape=jax.ShapeDtypeStruct((N_SLAB, n_batch * seq, LANES), F32),
        scratch_shapes=[pltpu.VMEM((N_SLAB, CONV_HALO + seq, LANES), F32)],
        compiler_params=_params("parallel"),
        name="conv_prompt",
    )(u_slabs, u_meta, wb)


def _conv_sample_kernel(t_new, st_ref, u_ref, wb_ref, o_ref, so_ref):
    hist = st_ref.shape[0]
    bb = st_ref.shape[1]
    so_ref[0:hist - t_new] = st_ref[t_new:hist]

    for c in range(N_SLAB):
        lanes = slice(c * LANES, (c + 1) * LANES)
        for t in range(t_new):
            so_ref[hist - t_new + t, :, lanes] = u_ref[c, pl.ds(t, bb, stride=t_new), :]

        def src(k, rc, c=c, lanes=lanes):
            if k < hist:
                return st_ref[k, pl.ds(pl.multiple_of(rc * SUBLANES, SUBLANES), SUBLANES), lanes]
            return u_ref[c, pl.ds(rc * SUBLANES * t_new + (k - hist), SUBLANES, stride=t_new), :]

        wts = [wb_ref[tap, c] for tap in range(CONV_K)]

        def per_rows(rc, carry, c=c, src=src, wts=wts):
            accs = [None] * t_new
            for k in range(hist + t_new):
                v = src(k, rc)
                for t in range(max(0, k - CONV_K + 1), min(t_new, k + 1)):
                    term = v * wts[k - t]
                    accs[t] = term if accs[t] is None else accs[t] + term
            for t in range(t_new):
                o_ref[c, pl.ds(rc * SUBLANES * t_new + t, SUBLANES, stride=t_new), :] = accs[t]
            return carry

        for rc in range(bb // SUBLANES):
            per_rows(rc, 0)


def _conv_sample(state_t, u_slabs, wb, t_new, bb):
    hist, nb, _ = state_t.shape
    rows = bb * t_new
    slab_spec = pl.BlockSpec((N_SLAB, rows, LANES), lambda s: (0, s, 0))
    return pl.pallas_call(
        functools.partial(_conv_sample_kernel, t_new),
        grid=(nb // bb,),
        in_specs=[
            pl.BlockSpec((hist, bb, CONV_CH), lambda s: (0, s, 0)),
            slab_spec,
            _const_spec((CONV_K, N_SLAB, SUBLANES, LANES)),
        ],
        out_specs=[slab_spec, pl.BlockSpec((hist, bb, CONV_CH), lambda s: (0, s, 0))],
        out_shape=[
            jax.ShapeDtypeStruct((N_SLAB, nb * t_new, LANES), F32),
            jax.ShapeDtypeStruct(state_t.shape, F32),
        ],
        compiler_params=_params("parallel"),
        name="conv_sample",
    )(state_t, u_slabs, wb)


def _outproj_kernel(x_ref, a_ref, c_ref, bdw_ref, g_ref, b_ref, wo_ref, h_ref, cbf_ref):
    y = jnp.concatenate([c_ref[c] for c in range(N_SLAB)], axis=1)
    cbf_ref[...] = _ln_swish(y + bdw_ref[...], g_ref[...], b_ref[...]).astype(BF16)
    mix = jnp.dot(a_ref[...], wo_ref[0:ATTN_WIDTH].astype(BF16), preferred_element_type=F32)
    mix = mix + jnp.dot(cbf_ref[...], wo_ref[ATTN_WIDTH:].astype(BF16), preferred_element_type=F32)
    h_ref[...] = x_ref[...] + mix


def _outproj(x, attn, conv_taps, b_dw, ln_g, ln_b, w_out, tm):
    m = x.shape[0]
    return pl.pallas_call(
        _outproj_kernel,
        grid=(m // tm,),
        in_specs=[
            pl.BlockSpec((tm, D_MODEL), lambda i: (i, 0)),
            pl.BlockSpec((tm, ATTN_WIDTH), lambda i: (i, 0)),
            pl.BlockSpec((N_SLAB, tm, LANES), lambda i: (0, i, 0)),
            _const_spec((1, CONV_CH)),
            _const_spec((1, CONV_CH)),
            _const_spec((1, CONV_CH)),
            _const_spec((D_MODEL, D_MODEL)),
        ],
        out_specs=pl.BlockSpec((tm, D_MODEL), lambda i: (i, 0)),
        out_shape=jax.ShapeDtypeStruct((m, D_MODEL), F32),
        scratch_shapes=[pltpu.VMEM((tm, CONV_CH), BF16)],
        compiler_params=_params("parallel"),
        name="outproj",
    )(x, attn, conv_taps, b_dw, ln_g, ln_b, w_out)


def _mlp_kernel(h_ref, g2_ref, wu_ref, wd_ref, gf_ref, y_ref, xn_ref):
    f = pl.program_id(1)

    @pl.when(f == 0)
    def _():
        h1 = h_ref[...]
        y_ref[...] = h1
        xn_ref[...] = _rms(h1, g2_ref[...]).astype(BF16)

    up = jnp.dot(xn_ref[...], wu_ref[...].astype(BF16), preferred_element_type=F32)
    act = jnp.maximum(up, 0.0)
    act = (act * act).astype(BF16)
    y_ref[...] += jnp.dot(act, wd_ref[...].astype(BF16), preferred_element_type=F32)

    @pl.when(f == pl.num_programs(1) - 1)
    def _():
        y_ref[...] = _rms(y_ref[...], gf_ref[...])


def _mlp(h1, g2, wu_bf, wd_bf, gf, tm, tf):
    m = h1.shape[0]
    return pl.pallas_call(
        _mlp_kernel,
        grid=(m // tm, D_FF // tf),
        in_specs=[
            pl.BlockSpec((tm, D_MODEL), lambda i, f: (i, 0)),
            _const_spec((1, D_MODEL)),
            pl.BlockSpec((D_MODEL, tf), lambda i, f: (0, f)),
            pl.BlockSpec((tf, D_MODEL), lambda i, f: (f, 0)),
            _const_spec((1, D_MODEL)),
        ],
        out_specs=pl.BlockSpec((tm, D_MODEL), lambda i, f: (i, 0)),
        out_shape=jax.ShapeDtypeStruct((m, D_MODEL), F32),
        scratch_shapes=[pltpu.VMEM((tm, D_MODEL), BF16)],
        compiler_params=pltpu.CompilerParams(dimension_semantics=("parallel", "arbitrary"),
                                             vmem_limit_bytes=VMEM_LIMIT_MLP),
        name="mlp",
    )(h1, g2, wu_bf, wd_bf, gf)


def kernel(x_prompt, x_sample, cache_k, cache_v, state_conv, meta_tokens, norm_mix, w_in, attn_sinks,
           w_dw, b_dw, conv_ln_g, conv_ln_b, w_out, norm_mlp, w_up, w_down, norm_final):
    n_batch, seq, _ = x_prompt.shape
    n_dec, t_new, _ = x_sample.shape
    depth = w_in.shape[0]
    window = cache_k.shape[2]
    assert depth == 1 and seq % BLOCK == 0 and t_new == SUBLANES and window == WINDOW
    l = 0
    n_blk = seq // BLOCK
    tm = 512
    tm_mlp = 1024
    tf = 512
    bt = LANES // t_new

    g_mix = norm_mix[l][None, :]
    g_mlp = norm_mlp[l][None, :]
    g_fin = norm_final[None, :]
    sinks = attn_sinks[l].astype(F32) * LOG2_E
    bdw = b_dw[l][None, :]
    ln_g = conv_ln_g[l][None, :]
    ln_b = conv_ln_b[l][None, :]
    wb_slabs = jnp.broadcast_to(w_dw[l].reshape(CONV_K, N_SLAB, 1, LANES), (CONV_K, N_SLAB, SUBLANES, LANES))

    rope_p = _rope_table(N_META + jnp.arange(seq, dtype=jnp.int32))
    rope_s = _rope_table(PAST_LEN + (jnp.arange(tm, dtype=jnp.int32) % t_new))
    ropet_s = jnp.tile(rope_s[:, :, :HEAD_DIM].transpose(0, 2, 1), (1, N_KV_HEADS, 1))
    rope_m = _rope_table(jnp.arange(N_META, dtype=jnp.int32))

    xp = x_prompt.reshape(n_batch * seq, D_MODEL)
    xs = x_sample.reshape(n_dec * t_new, D_MODEL)

    q_p, k_p, v_p, u_p = _inproj_prompt(xp, g_mix, w_in[l], rope_p, tm)
    q_s, kt_s, vt_s, u_s, k_m, v_m, u_m = _inproj_sample(xs, g_mix, w_in[l], rope_s, ropet_s,
                                                         meta_tokens.astype(F32), rope_m, tm)

    attn_p = _attn_prompt(sinks, q_p, k_p, v_p, k_m, v_m, n_batch, n_blk)
    conv_p = _conv_prompt(u_p, u_m, wb_slabs, n_batch, seq)

    ckt = cache_k[l].transpose(0, 2, 3, 1).reshape(n_dec, KV_WIDTH, window)
    cvt = cache_v[l].transpose(0, 2, 3, 1).reshape(n_dec, KV_WIDTH, window)
    sink_rows = jnp.broadcast_to(jnp.repeat(sinks, t_new)[:, None], (N_HEADS * t_new, LANES))
    attn_s, kt_out, vt_out = _attn_sample(sink_rows, q_s, kt_s, vt_s, ckt, cvt, t_new, bt)
    state_t = state_conv[l].transpose(1, 0, 2)
    conv_s, state_out_t = _conv_sample(state_t, u_s, wb_slabs, t_new, 32)

    h_p = _outproj(xp, attn_p, conv_p, bdw, ln_g, ln_b, w_out[l], tm)
    h_s = _outproj(xs, attn_s, conv_s, bdw, ln_g, ln_b, w_out[l], tm)
    y_p = _mlp(h_p, g_mlp, w_up[l], w_down[l], g_fin, tm_mlp, tf)
    y_s = _mlp(h_s, g_mlp, w_up[l], w_down[l], g_fin, tm_mlp, tf)

    y_prompt = y_p.reshape(n_batch, seq, D_MODEL)
    y_sample = y_s.reshape(n_dec, t_new, D_MODEL)
    k_last = k_p.reshape(n_batch, seq, KV_WIDTH)[:, seq - WINDOW:]
    v_last = v_p.reshape(n_batch, seq, KV_WIDTH)[:, seq - WINDOW:]
    new_k_prompt = k_last.reshape(n_batch, WINDOW, N_KV_HEADS, HEAD_DIM)[None]
    new_v_prompt = v_last.reshape(n_batch, WINDOW, N_KV_HEADS, HEAD_DIM)[None]
    u_last = u_p.reshape(N_SLAB, n_batch, seq, LANES)[:, :, seq - (CONV_K - 1):]
    new_conv_prompt = u_last.transpose(1, 2, 0, 3).reshape(n_batch, CONV_K - 1, CONV_CH)[None]
    new_k_sample = kt_out.reshape(n_dec, N_KV_HEADS, HEAD_DIM, window).transpose(0, 3, 1, 2)[None]
    new_v_sample = vt_out.reshape(n_dec, N_KV_HEADS, HEAD_DIM, window).transpose(0, 3, 1, 2)[None]
    new_conv_sample = state_out_t.transpose(1, 0, 2)[None]
    return (y_prompt, y_sample, new_k_prompt, new_v_prompt, new_conv_prompt,
            new_k_sample, new_v_sample, new_conv_sample)
```

```python
import functools

import jax
import jax.numpy as jnp
from jax import lax
from jax.experimental import pallas as pl
from jax.experimental.pallas import tpu as pltpu

D_MODEL = 2048
N_META = 16
HEAD_DIM = 64
ATTN_WIDTH = D_MODEL // 2
CONV_CH = D_MODEL - ATTN_WIDTH
N_HEADS = ATTN_WIDTH // HEAD_DIM
N_KV_HEADS = 4
GQA_GROUP = N_HEADS // N_KV_HEADS
KV_WIDTH = N_KV_HEADS * HEAD_DIM
IN_COLS = ATTN_WIDTH + 2 * KV_WIDTH + 2 * CONV_CH
WINDOW = 128
BLOCK = 128
ROPE_THETA = 500000.0
ROT_DIM = HEAD_DIM // 4
CONV_K = 31
D_FF = 4 * D_MODEL
RMS_EPS = 1e-6
LN_EPS = 1e-5
MASK_VALUE = -1e30
PAST_LEN = 8192

LANES = 128
SUBLANES = 8
VMEM_LIMIT = 56 * 1024 * 1024
VMEM_LIMIT_MLP = 62 * 1024 * 1024
N_SLAB = CONV_CH // LANES

F32 = jnp.float32
BF16 = jnp.bfloat16

K_OFF = ATTN_WIDTH
A_OFF = ATTN_WIDTH + 2 * KV_WIDTH
B_OFF = A_OFF + CONV_CH
NT_DIMS = (((1,), (1,)), ((), ()))
LOG2_E = 1.4426950408889634
Q_SCALE = HEAD_DIM ** -0.5 * LOG2_E


def _params(*sem):
    return pltpu.CompilerParams(dimension_semantics=sem, vmem_limit_bytes=VMEM_LIMIT)


def _const_spec(shape):
    nd = len(shape)
    return pl.BlockSpec(shape, lambda *_: (0,) * nd, pipeline_mode=pl.Buffered(1))


def _rms(x, g):
    ms = jnp.mean(x * x, axis=-1, keepdims=True)
    return (x * lax.rsqrt(ms + RMS_EPS)) * g


def _rope_rows(z, rope_ref):
    return (z * rope_ref[0] + pltpu.roll(z, LANES - ROT_DIM // 2, 1) * rope_ref[1]
            + pltpu.roll(z, ROT_DIM // 2, 1) * rope_ref[2])


def _proj(xn_ref, w_ref, off, width):
    return jnp.dot(xn_ref[...], w_ref[:, off:off + width].astype(BF16), preferred_element_type=F32)


def _write_q(xn_ref, w_ref, rope_ref, q_ref):
    half = ATTN_WIDTH // 2
    for c in range(2):
        acc = _proj(xn_ref, w_ref, c * half, half)
        for cc in range(half // LANES):
            z = _rope_rows(acc[:, cc * LANES:(cc + 1) * LANES], rope_ref) * Q_SCALE
            q_ref[:, c * half + cc * LANES:c * half + (cc + 1) * LANES] = z.astype(q_ref.dtype)


def _glu(xn_ref, w_ref, c):
    half = CONV_CH // 2
    a = _proj(xn_ref, w_ref, A_OFF + c * half, half)
    b = _proj(xn_ref, w_ref, B_OFF + c * half, half)
    return a * jax.nn.sigmoid(b)


def _inproj_prompt_kernel(x_ref, g_ref, w_ref, rope_ref, q_ref, k_ref, v_ref, u_ref, xn_ref):
    xn_ref[...] = _rms(x_ref[...], g_ref[...]).astype(BF16)
    _write_q(xn_ref, w_ref, rope_ref, q_ref)
    _write_kvu(xn_ref, w_ref, rope_ref, k_ref, v_ref, u_ref)


def _write_kvu(xn_ref, w_ref, rope_ref, k_ref, v_ref, u_ref):
    acc = _proj(xn_ref, w_ref, K_OFF, 2 * KV_WIDTH)
    for cc in range(KV_WIDTH // LANES):
        k_ref[:, cc * LANES:(cc + 1) * LANES] = _rope_rows(acc[:, cc * LANES:(cc + 1) * LANES], rope_ref)
    v_ref[...] = acc[:, KV_WIDTH:]
    per = N_SLAB // 2
    for c in range(2):
        u = _glu(xn_ref, w_ref, c)
        for cc in range(per):
            u_ref[c * per + cc] = u[:, cc * LANES:(cc + 1) * LANES]


def _inproj_prompt(x, g, w_in, rope_tab, tm):
    m = x.shape[0]
    nrt = rope_tab.shape[1] // tm
    return pl.pallas_call(
        _inproj_prompt_kernel,
        grid=(m // tm,),
        in_specs=[
            pl.BlockSpec((tm, D_MODEL), lambda i: (i, 0)),
            _const_spec((1, D_MODEL)),
            _const_spec((D_MODEL, IN_COLS)),
            pl.BlockSpec((3, tm, LANES), lambda i: (0, i % nrt, 0)),
        ],
        out_specs=[
            pl.BlockSpec((tm, ATTN_WIDTH), lambda i: (i, 0)),
            pl.BlockSpec((tm, KV_WIDTH), lambda i: (i, 0)),
            pl.BlockSpec((tm, KV_WIDTH), lambda i: (i, 0)),
            pl.BlockSpec((N_SLAB, tm, LANES), lambda i: (0, i, 0)),
        ],
        out_shape=[
            jax.ShapeDtypeStruct((m, ATTN_WIDTH), BF16),
            jax.ShapeDtypeStruct((m, KV_WIDTH), F32),
            jax.ShapeDtypeStruct((m, KV_WIDTH), F32),
            jax.ShapeDtypeStruct((N_SLAB, m, LANES), F32),
        ],
        scratch_shapes=[pltpu.VMEM((tm, D_MODEL), BF16)],
        compiler_params=_params("parallel"),
        name="inproj_prompt",
    )(x, g, w_in, rope_tab)


def _inproj_sample_kernel(x_ref, g_ref, w_ref, rope_ref, ropet_ref, meta_ref, ropem_ref,
                          q_ref, kt_ref, vt_ref, u_ref, km_ref, vm_ref, um_ref, xn_ref, xm_ref):
    @pl.when(pl.program_id(0) == 0)
    def _():
        xm_ref[...] = _rms(meta_ref[...], g_ref[...]).astype(BF16)
        _write_kvu(xm_ref, w_ref, ropem_ref, km_ref, vm_ref, um_ref)

    xn_ref[...] = _rms(x_ref[...], g_ref[...]).astype(BF16)
    _write_q(xn_ref, w_ref, rope_ref, q_ref)
    kvt = _proj(xn_ref, w_ref, K_OFF, 2 * KV_WIDTH).T
    kt = kvt[0:KV_WIDTH]
    sh = ROT_DIM // 2
    up = jnp.concatenate([kt[sh:], kt[:sh]], axis=0)
    dn = jnp.concatenate([kt[-sh:], kt[:-sh]], axis=0)
    kt_ref[...] = kt * ropet_ref[0] + up * ropet_ref[1] + dn * ropet_ref[2]
    vt_ref[...] = kvt[KV_WIDTH:]
    per = N_SLAB // 2
    for c in range(2):
        u = _glu(xn_ref, w_ref, c)
        for cc in range(per):
            u_ref[c * per + cc] = u[:, cc * LANES:(cc + 1) * LANES]


def _inproj_sample(x, g, w_in, rope_tab, ropet_tab, meta, rope_meta, tm):
    m = x.shape[0]
    n_meta = meta.shape[0]
    return pl.pallas_call(
        _inproj_sample_kernel,
        grid=(m // tm,),
        in_specs=[
            pl.BlockSpec((tm, D_MODEL), lambda i: (i, 0)),
            _const_spec((1, D_MODEL)),
            _const_spec((D_MODEL, IN_COLS)),
            _const_spec((3, tm, LANES)),
            _const_spec((3, KV_WIDTH, tm)),
            _const_spec((n_meta, D_MODEL)),
            _const_spec((3, n_meta, LANES)),
        ],
        out_specs=[
            pl.BlockSpec((tm, ATTN_WIDTH), lambda i: (i, 0)),
            pl.BlockSpec((KV_WIDTH, tm), lambda i: (0, i)),
            pl.BlockSpec((KV_WIDTH, tm), lambda i: (0, i)),
            pl.BlockSpec((N_SLAB, tm, LANES), lambda i: (0, i, 0)),
            pl.BlockSpec((n_meta, KV_WIDTH), lambda i: (0, 0)),
            pl.BlockSpec((n_meta, KV_WIDTH), lambda i: (0, 0)),
            pl.BlockSpec((N_SLAB, n_meta, LANES), lambda i: (0, 0, 0)),
        ],
        out_shape=[
            jax.ShapeDtypeStruct((m, ATTN_WIDTH), F32),
            jax.ShapeDtypeStruct((KV_WIDTH, m), F32),
            jax.ShapeDtypeStruct((KV_WIDTH, m), F32),
            jax.ShapeDtypeStruct((N_SLAB, m, LANES), F32),
            jax.ShapeDtypeStruct((n_meta, KV_WIDTH), F32),
            jax.ShapeDtypeStruct((n_meta, KV_WIDTH), F32),
            jax.ShapeDtypeStruct((N_SLAB, n_meta, LANES), F32),
        ],
        scratch_shapes=[pltpu.VMEM((tm, D_MODEL), BF16), pltpu.VMEM((n_meta, D_MODEL), BF16)],
        compiler_params=_params("arbitrary"),
        name="inproj_sample",
    )(x, g, w_in, rope_tab, ropet_tab, meta, rope_meta)


def _rope_table(pos):
    half = ROT_DIM // 2
    inv = jnp.power(jnp.float32(ROPE_THETA), -jnp.arange(half, dtype=F32) * 2.0 / ROT_DIM)
    ang = pos.astype(F32)[:, None] * inv[None, :]
    cos, sin = jnp.cos(ang), jnp.sin(ang)
    t = pos.shape[0]
    rest = HEAD_DIM - ROT_DIM
    one = jnp.ones((t, rest), F32)
    zero = jnp.zeros((t, rest), F32)
    zh = jnp.zeros((t, half), F32)
    c = jnp.concatenate([cos, cos, one], axis=1)
    s_lo = jnp.concatenate([-sin, zh, zero], axis=1)
    s_hi = jnp.concatenate([zh, sin, zero], axis=1)
    tab = jnp.stack([c, s_lo, s_hi])
    return jnp.tile(tab, (1, 1, LANES // HEAD_DIM))


def _attn_prompt_kernel(sink_ref, q_ref, kc_ref, kp_ref, vc_ref, vp_ref, km_ref, vm_ref, o_ref,
                        kk_ref, vv_ref, s_ref, p_ref, inv_ref):
    n = pl.program_id(1)

    @pl.when(n == 0)
    def _():
        pad = jnp.zeros((BLOCK - N_META, KV_WIDTH), F32)
        kk_ref[0:BLOCK - N_META] = pad
        vv_ref[0:BLOCK - N_META] = pad
        kk_ref[BLOCK - N_META:BLOCK] = km_ref[...]
        vv_ref[BLOCK - N_META:BLOCK] = vm_ref[...]

    @pl.when(n > 0)
    def _():
        kk_ref[0:BLOCK] = kp_ref[...]
        vv_ref[0:BLOCK] = vp_ref[...]

    kk_ref[BLOCK:2 * BLOCK] = kc_ref[...]
    vv_ref[BLOCK:2 * BLOCK] = vc_ref[...]
    vv_ref[0:1, :] = jnp.zeros((1, KV_WIDTH), F32)

    lo = jnp.where(n == 0, BLOCK - N_META, 0)
    r = lax.broadcasted_iota(jnp.int32, (BLOCK, 2 * BLOCK), 0)
    j = lax.broadcasted_iota(jnp.int32, (BLOCK, 2 * BLOCK), 1)
    mask3 = ((j > r) & (j <= r + BLOCK) & (j >= lo)).reshape(BLOCK // SUBLANES, SUBLANES, 2 * BLOCK)
    col0 = lax.broadcasted_iota(jnp.int32, (1, SUBLANES, 2 * BLOCK), 2) == 0
    low_half = lax.broadcasted_iota(jnp.int32, (BLOCK, LANES), 1) < HEAD_DIM
    low_half2 = lax.broadcasted_iota(jnp.int32, (2 * BLOCK, LANES), 1) < HEAD_DIM

    def dup_halves(ref, h):
        xf = ref[:, (h // 2) * LANES:(h // 2 + 1) * LANES]
        x_sw = pltpu.roll(xf, HEAD_DIM, 1)
        both = jnp.where(low_half2, xf, x_sw) if h % 2 == 0 else jnp.where(low_half2, x_sw, xf)
        return both.astype(BF16)

    rows_kv = GQA_GROUP * BLOCK
    for h in range(N_KV_HEADS):
        qa = q_ref[:, (2 * h) * LANES:(2 * h + 1) * LANES]
        qb = q_ref[:, (2 * h + 1) * LANES:(2 * h + 2) * LANES]
        zero = jnp.zeros_like(qa)
        lhs = jnp.concatenate([jnp.where(low_half, qa, zero), jnp.where(low_half, qb, zero),
                               jnp.where(low_half, zero, qa), jnp.where(low_half, zero, qb)], axis=0)
        s_all = lax.dot_general(lhs, dup_halves(kk_ref, h), NT_DIMS, preferred_element_type=F32)
        heads = (4 * h, 4 * h + 2, 4 * h + 1, 4 * h + 3)
        for g in range(GQA_GROUP):
            fill = jnp.where(col0, sink_ref[heads[g]], MASK_VALUE)
            s = jnp.where(mask3, s_all[g * BLOCK:(g + 1) * BLOCK].reshape(mask3.shape), fill)
            s_ref[h * rows_kv + g * BLOCK:h * rows_kv + (g + 1) * BLOCK] = s.reshape(BLOCK, 2 * BLOCK)

    s = s_ref[...]
    mx = jnp.max(s, axis=-1, keepdims=True)
    e = jnp.exp2(s - mx)
    p_ref[...] = e.astype(BF16)
    inv_ref[...] = jnp.broadcast_to(1.0 / jnp.sum(e, axis=-1, keepdims=True), inv_ref.shape)

    for h in range(N_KV_HEADS):
        base = h * rows_kv
        o_all = jnp.dot(p_ref[base:base + rows_kv], dup_halves(vv_ref, h), preferred_element_type=F32)
        o = [o_all[g * BLOCK:(g + 1) * BLOCK] * inv_ref[base + g * BLOCK:base + (g + 1) * BLOCK]
             for g in range(GQA_GROUP)]
        o_ref[:, (2 * h) * LANES:(2 * h + 1) * LANES] = jnp.where(low_half, o[0], o[2]).astype(o_ref.dtype)
        o_ref[:, (2 * h + 1) * LANES:(2 * h + 2) * LANES] = jnp.where(low_half, o[1], o[3]).astype(o_ref.dtype)


def _attn_prompt(sinks, q, k, v, k_meta, v_meta, n_batch, n_blk):
    kv_cur = pl.BlockSpec((BLOCK, KV_WIDTH), lambda b, n: (b * n_blk + n, 0))
    kv_prev = pl.BlockSpec((BLOCK, KV_WIDTH), lambda b, n: (b * n_blk + jnp.maximum(n - 1, 0), 0))
    return pl.pallas_call(
        _attn_prompt_kernel,
        grid=(n_batch, n_blk),
        in_specs=[
            pl.BlockSpec(memory_space=pltpu.SMEM),
            pl.BlockSpec((BLOCK, ATTN_WIDTH), lambda b, n: (b * n_blk + n, 0)),
            kv_cur, kv_prev, kv_cur, kv_prev,
            _const_spec((N_META, KV_WIDTH)),
            _const_spec((N_META, KV_WIDTH)),
        ],
        out_specs=pl.BlockSpec((BLOCK, ATTN_WIDTH), lambda b, n: (b * n_blk + n, 0)),
        out_shape=jax.ShapeDtypeStruct((n_batch * n_blk * BLOCK, ATTN_WIDTH), BF16),
        scratch_shapes=[
            pltpu.VMEM((2 * BLOCK, KV_WIDTH), F32),
            pltpu.VMEM((2 * BLOCK, KV_WIDTH), F32),
            pltpu.VMEM((N_HEADS * BLOCK, 2 * BLOCK), F32),
            pltpu.VMEM((N_HEADS * BLOCK, 2 * BLOCK), BF16),
            pltpu.VMEM((N_HEADS * BLOCK, LANES), F32),
        ],
        compiler_params=_params("parallel", "arbitrary"),
        name="attn_prompt",
    )(sinks, q, k, k, v, v, k_meta, v_meta)


def _attn_sample_kernel(t_new, bt, sink_ref, q_ref, kn_ref, vn_ref, ck_ref, cv_ref,
                        o_ref, ko_ref, vo_ref, s_ref, p_ref, inv_ref, osc_ref):
    nrow = N_HEADS * t_new
    w = ck_ref.shape[2]
    assert bt * t_new == LANES and w == LANES and nrow == LANES
    zero_blk = jnp.zeros((t_new, LANES), F32)

    r = lax.broadcasted_iota(jnp.int32, (nrow, 2 * w), 0) % t_new
    j = lax.broadcasted_iota(jnp.int32, (nrow, 2 * w), 1)
    cache_ok = (j > r) & (j < w)
    fill = jnp.where(j == 0, sink_ref[...][:, 0:1], MASK_VALUE)
    key0 = lax.broadcasted_iota(jnp.int32, (bt * nrow, w), 1) == 0
    low_half =lax.broadcasted_iota(jnp.int32, (t_new, LANES), 1) < HEAD_DIM
    tail = lax.broadcasted_iota(jnp.int32, (KV_WIDTH, LANES), 1) >= w - t_new
    kn = kn_ref[...]
    vn = vn_ref[...]
    kn_bf = kn.astype(BF16)
    vn_bf = vn.astype(BF16)

    for i in range(bt):
        new_shift = (w - t_new - i * t_new) % LANES
        ko_ref[i] = jnp.where(tail, pltpu.roll(kn, new_shift, 1), pltpu.roll(ck_ref[i], w - t_new, 1))
        vo_ref[i] = jnp.where(tail, pltpu.roll(vn, new_shift, 1), pltpu.roll(cv_ref[i], w - t_new, 1))

    for i in range(bt):
        qrows = []
        for t in range(N_HEADS):
            h = t // GQA_GROUP
            qv = q_ref[i * t_new:(i + 1) * t_new, (t // 2) * LANES:(t // 2 + 1) * LANES]
            if t % 2 != h % 2:
                qv = pltpu.roll(qv, HEAD_DIM, 1)
            keep = low_half if h % 2 == 0 else jnp.logical_not(low_half)
            cols = [zero_blk] * (KV_WIDTH // LANES)
            cols[h // 2] = jnp.where(keep, qv, 0.0)
            qrows.append(jnp.concatenate(cols, axis=1))
        qrow = jnp.concatenate(qrows, axis=0).astype(BF16)

        rhs_k = jnp.concatenate([ck_ref[i].astype(BF16), kn_bf], axis=1)
        s = jnp.dot(qrow, rhs_k, preferred_element_type=F32)
        own = w + i * t_new
        mask = cache_ok | ((j >= own) & (j <= own + r))
        s_ref[i * nrow:(i + 1) * nrow] = jnp.where(mask, s, fill)

    s = s_ref[...]
    mx = jnp.max(s, axis=-1, keepdims=True)
    e = jnp.exp2(s - mx)
    den = jnp.sum(e, axis=-1, keepdims=True)
    p_ref[:, 0:w] = jnp.where(key0, 0.0, e[:, 0:w]).astype(BF16)
    p_ref[:, w:2 * w] = e[:, w:2 * w].astype(BF16)
    inv_ref[...] = jnp.broadcast_to(1.0 / den, inv_ref.shape)

    for i in range(bt):
        rhs_v = jnp.concatenate([cv_ref[i].astype(BF16), vn_bf], axis=1)
        o_all = lax.dot_general(p_ref[i * nrow:(i + 1) * nrow], rhs_v, NT_DIMS, preferred_element_type=F32)
        for t2 in range(N_HEADS // 2):
            parts = []
            for side in range(2):
                t = 2 * t2 + side
                h = t // GQA_GROUP
                blk = o_all[t * t_new:(t + 1) * t_new, (h // 2) * LANES:(h // 2 + 1) * LANES]
                blk = blk * inv_ref[i * nrow + t * t_new:i * nrow + (t + 1) * t_new, :]
                if t % 2 != h % 2:
                    blk = pltpu.roll(blk, HEAD_DIM, 1)
                parts.append(blk)
            osc_ref[i * t_new:(i + 1) * t_new, t2 * LANES:(t2 + 1) * LANES] = (
                jnp.where(low_half, parts[0], parts[1]))

    o_ref[...] = osc_ref[...].astype(o_ref.dtype)


def _attn_sample(sink_rows, q, kt_new, vt_new, cache_kt, cache_vt, t_new, bt):
    nb, _, w = cache_kt.shape
    rows = bt * t_new
    cache_spec = pl.BlockSpec((bt, KV_WIDTH, w), lambda s: (s, 0, 0))
    new_spec = pl.BlockSpec((KV_WIDTH, rows), lambda s: (0, s))
    return pl.pallas_call(
        functools.partial(_attn_sample_kernel, t_new, bt),
        grid=(nb // bt,),
        in_specs=[
            _const_spec(sink_rows.shape),
            pl.BlockSpec((rows, ATTN_WIDTH), lambda s: (s, 0)),
            new_spec, new_spec, cache_spec, cache_spec,
        ],
        out_specs=[pl.BlockSpec((rows, ATTN_WIDTH), lambda s: (s, 0)), cache_spec, cache_spec],
        out_shape=[
            jax.ShapeDtypeStruct((nb * t_new, ATTN_WIDTH), BF16),
            jax.ShapeDtypeStruct(cache_kt.shape, F32),
            jax.ShapeDtypeStruct(cache_vt.shape, F32),
        ],
        scratch_shapes=[
            pltpu.VMEM((bt * N_HEADS * t_new, 2 * w), F32),
            pltpu.VMEM((bt * N_HEADS * t_new, 2 * w), BF16),
            pltpu.VMEM((bt * N_HEADS * t_new, LANES), F32),
            pltpu.VMEM((rows, ATTN_WIDTH), F32),
        ],
        compiler_params=_params("parallel"),
        name="attn_sample",
    )(sink_rows, q, kt_new, vt_new, cache_kt, cache_vt)


def _ln_swish(y, g, b):
    mu = jnp.mean(y, axis=-1, keepdims=True)
    d = y - mu
    var = jnp.mean(d * d, axis=-1, keepdims=True)
    yn = d * lax.rsqrt(var + LN_EPS) * g + b
    return yn * jax.nn.sigmoid(yn)


CONV_HALO = 32
CONV_SEG = 252
CONV_SEG_STEP = 126


def _conv_prompt_kernel(seq, u_ref, meta_ref, wb_ref, o_ref, win_ref):
    pad = CONV_HALO - N_META
    for c in range(N_SLAB):
        win_ref[c, 0:pad] = jnp.zeros((pad, LANES), F32)
        win_ref[c, pad:CONV_HALO] = meta_ref[c]
        win_ref[c, CONV_HALO:CONV_HALO + seq] = u_ref[c]
    first = CONV_HALO - (CONV_K - 1)

    rem_base = SUBLANES * CONV_SEG
    rem_seg = (seq - rem_base) // SUBLANES
    n_main = CONV_SEG // CONV_SEG_STEP

    def taps(c, wts, base, seg, n_g, i0):
        accs = [None] * n_g
        for m in range(n_g + CONV_K - 1):
            v = win_ref[c, pl.ds(base + first + i0 + m, SUBLANES, stride=seg), :]
            for g in range(max(0, m - CONV_K + 1), min(n_g, m + 1)):
                term = v * wts[m - g]
                accs[g] = term if accs[g] is None else accs[g] + term
        for g in range(n_g):
            o_ref[c, pl.ds(base + i0 + g, SUBLANES, stride=seg), :] = accs[g]

    for c in range(N_SLAB):
        wts = [wb_ref[tap, c] for tap in range(CONV_K)]

        def body(ci, carry, c=c, wts=wts):
            taps(c, wts, 0, CONV_SEG, CONV_SEG_STEP, ci * CONV_SEG_STEP)
            return carry

        lax.fori_loop(0, n_main, body, 0)
        taps(c, wts, rem_base, rem_seg, rem_seg, 0)


def _conv_prompt(u_slabs, u_meta, wb, n_batch, seq):
    assert CONV_SEG % CONV_SEG_STEP == 0 and seq > SUBLANES * CONV_SEG
    assert (seq - SUBLANES * CONV_SEG) % SUBLANES == 0
    blk = pl.BlockSpec((N_SLAB, seq, LANES), lambda b: (0, b, 0))
    return pl.pallas_call(
        functools.partial(_conv_prompt_kernel, seq),
        grid=(n_batch,),
        in_specs=[
            blk,
            _const_spec((N_SLAB, N_META, LANES)),
            _const_spec((CONV_K, N_SLAB, SUBLANES, LANES)),
        ],
        out_specs=blk,
        out_shape=jax.ShapeDtypeStruct((N_SLAB, n_batch * seq, LANES), F32),
        scratch_shapes=[pltpu.VMEM((N_SLAB, CONV_HALO + seq, LANES), F32)],
        compiler_params=_params("parallel"),
        name="conv_prompt",
    )(u_slabs, u_meta, wb)


def _conv_sample_kernel(t_new, st_ref, u_ref, wb_ref, o_ref, so_ref):
    hist = st_ref.shape[0]
    bb = st_ref.shape[1]
    so_ref[0:hist - t_new] = st_ref[t_new:hist]

    for c in range(N_SLAB):
        lanes = slice(c * LANES, (c + 1) * LANES)
        for t in range(t_new):
            so_ref[hist - t_new + t, :, lanes] = u_ref[c, pl.ds(t, bb, stride=t_new), :]

        def src(k, rc, c=c, lanes=lanes):
            if k < hist:
                return st_ref[k, pl.ds(pl.multiple_of(rc * SUBLANES, SUBLANES), SUBLANES), lanes]
            return u_ref[c, pl.ds(rc * SUBLANES * t_new + (k - hist), SUBLANES, stride=t_new), :]

        wts = [wb_ref[tap, c] for tap in range(CONV_K)]

        def per_rows(rc, carry, c=c, src=src, wts=wts):
            accs = [None] * t_new
            for k in range(hist + t_new):
                v = src(k, rc)
                for t in range(max(0, k - CONV_K + 1), min(t_new, k + 1)):
                    term = v * wts[k - t]
                    accs[t] = term if accs[t] is None else accs[t] + term
            for t in range(t_new):
                o_ref[c, pl.ds(rc * SUBLANES * t_new + t, SUBLANES, stride=t_new), :] = accs[t]
            return carry

        for rc in range(bb // SUBLANES):
            per_rows(rc, 0)


def _conv_sample(state_t, u_slabs, wb, t_new, bb):
    hist, nb, _ = state_t.shape
    rows = bb * t_new
    slab_spec = pl.BlockSpec((N_SLAB, rows, LANES), lambda s: (0, s, 0))
    return pl.pallas_call(
        functools.partial(_conv_sample_kernel, t_new),
        grid=(nb // bb,),
        in_specs=[
            pl.BlockSpec((hist, bb, CONV_CH), lambda s: (0, s, 0)),
            slab_spec,
            _const_spec((CONV_K, N_SLAB, SUBLANES, LANES)),
        ],
        out_specs=[slab_spec, pl.BlockSpec((hist, bb, CONV_CH), lambda s: (0, s, 0))],
        out_shape=[
            jax.ShapeDtypeStruct((N_SLAB, nb * t_new, LANES), F32),
            jax.ShapeDtypeStruct(state_t.shape, F32),
        ],
        compiler_params=_params("parallel"),
        name="conv_sample",
    )(state_t, u_slabs, wb)


def _outproj_kernel(x_ref, a_ref, c_ref, bdw_ref, g_ref, b_ref, wo_ref, h_ref, cbf_ref):
    y = jnp.concatenate([c_ref[c] for c in range(N_SLAB)], axis=1)
    cbf_ref[...] = _ln_swish(y + bdw_ref[...], g_ref[...], b_ref[...]).astype(BF16)
    mix = jnp.dot(a_ref[...], wo_ref[0:ATTN_WIDTH].astype(BF16), preferred_element_type=F32)
    mix = mix + jnp.dot(cbf_ref[...], wo_ref[ATTN_WIDTH:].astype(BF16), preferred_element_type=F32)
    h_ref[...] = x_ref[...] + mix


def _outproj(x, attn, conv_taps, b_dw, ln_g, ln_b, w_out, tm):
    m = x.shape[0]
    return pl.pallas_call(
        _outproj_kernel,
        grid=(m // tm,),
        in_specs=[
            pl.BlockSpec((tm, D_MODEL), lambda i: (i, 0)),
            pl.BlockSpec((tm, ATTN_WIDTH), lambda i: (i, 0)),
            pl.BlockSpec((N_SLAB, tm, LANES), lambda i: (0, i, 0)),
            _const_spec((1, CONV_CH)),
            _const_spec((1, CONV_CH)),
            _const_spec((1, CONV_CH)),
            _const_spec((D_MODEL, D_MODEL)),
        ],
        out_specs=pl.BlockSpec((tm, D_MODEL), lambda i: (i, 0)),
        out_shape=jax.ShapeDtypeStruct((m, D_MODEL), F32),
        scratch_shapes=[pltpu.VMEM((tm, CONV_CH), BF16)],
        compiler_params=_params("parallel"),
        name="outproj",
    )(x, attn, conv_taps, b_dw, ln_g, ln_b, w_out)


def _mlp_kernel(h_ref, g2_ref, wu_ref, wd_ref, gf_ref, y_ref, xn_ref):
    f = pl.program_id(1)

    @pl.when(f == 0)
    def _():
        h1 = h_ref[...]
        y_ref[...] = h1
        xn_ref[...] = _rms(h1, g2_ref[...]).astype(BF16)

    up = jnp.dot(xn_ref[...], wu_ref[...].astype(BF16), preferred_element_type=F32)
    act = jnp.maximum(up, 0.0)
    act = (act * act).astype(BF16)
    y_ref[...] += jnp.dot(act, wd_ref[...].astype(BF16), preferred_element_type=F32)

    @pl.when(f == pl.num_programs(1) - 1)
    def _():
        y_ref[...] = _rms(y_ref[...], gf_ref[...])


def _mlp(h1, g2, wu_bf, wd_bf, gf, tm, tf):
    m = h1.shape[0]
    return pl.pallas_call(
        _mlp_kernel,
        grid=(m // tm, D_FF // tf),
        in_specs=[
            pl.BlockSpec((tm, D_MODEL), lambda i, f: (i, 0)),
            _const_spec((1, D_MODEL)),
            pl.BlockSpec((D_MODEL, tf), lambda i, f: (0, f)),
            pl.BlockSpec((tf, D_MODEL), lambda i, f: (f, 0)),
            _const_spec((1, D_MODEL)),
        ],
        out_specs=pl.BlockSpec((tm, D_MODEL), lambda i, f: (i, 0)),
        out_shape=jax.ShapeDtypeStruct((m, D_MODEL), F32),
        scratch_shapes=[pltpu.VMEM((tm, D_MODEL), BF16)],
        compiler_params=pltpu.CompilerParams(dimension_semantics=("parallel", "arbitrary"),
                                             vmem_limit_bytes=VMEM_LIMIT_MLP),
        name="mlp",
    )(h1, g2, wu_bf, wd_bf, gf)


def kernel(x_prompt, x_sample, cache_k, cache_v, state_conv, meta_tokens, norm_mix, w_in, attn_sinks,
           w_dw, b_dw, conv_ln_g, conv_ln_b, w_out, norm_mlp, w_up, w_down, norm_final):
    n_batch, seq, _ = x_prompt.shape
    n_dec, t_new, _ = x_sample.shape
    depth = w_in.shape[0]
    window = cache_k.shape[2]
    assert depth == 1 and seq % BLOCK == 0 and t_new == SUBLANES and window == WINDOW
    l = 0
    n_blk = seq // BLOCK
    tm = 512
    tm_mlp = 1024
    tf = 512
    bt = LANES // t_new
    conv_bb = 32
    assert (n_batch * seq) % tm_mlp == 0 and (n_dec * t_new) % tm_mlp == 0 and n_dec % conv_bb == 0

    g_mix = norm_mix[l][None, :]
    g_mlp = norm_mlp[l][None, :]
    g_fin = norm_final[None, :]
    sinks = attn_sinks[l].astype(F32) * LOG2_E
    bdw = b_dw[l][None, :]
    ln_g = conv_ln_g[l][None, :]
    ln_b = conv_ln_b[l][None, :]
    wb_slabs = jnp.broadcast_to(w_dw[l].reshape(CONV_K, N_SLAB, 1, LANES), (CONV_K, N_SLAB, SUBLANES, LANES))

    rope_p = _rope_table(N_META + jnp.arange(seq, dtype=jnp.int32))
    rope_s = _rope_table(PAST_LEN + (jnp.arange(tm, dtype=jnp.int32) % t_new))
    ropet_s = jnp.tile(rope_s[:, :, :HEAD_DIM].transpose(0, 2, 1), (1, N_KV_HEADS, 1))
    rope_m = _rope_table(jnp.arange(N_META, dtype=jnp.int32))

    xp = x_prompt.reshape(n_batch * seq, D_MODEL)
    xs = x_sample.reshape(n_dec * t_new, D_MODEL)

    q_p, k_p, v_p, u_p = _inproj_prompt(xp, g_mix, w_in[l], rope_p, tm)
    q_s, kt_s, vt_s, u_s, k_m, v_m, u_m = _inproj_sample(xs, g_mix, w_in[l], rope_s, ropet_s,
                                                         meta_tokens.astype(F32), rope_m, tm)

    attn_p = _attn_prompt(sinks, q_p, k_p, v_p, k_m, v_m, n_batch, n_blk)
    conv_p = _conv_prompt(u_p, u_m, wb_slabs, n_batch, seq)

    ckt = cache_k[l].transpose(0, 2, 3, 1).reshape(n_dec, KV_WIDTH, window)
    cvt = cache_v[l].transpose(0, 2, 3, 1).reshape(n_dec, KV_WIDTH, window)
    sink_rows = jnp.broadcast_to(jnp.repeat(sinks, t_new)[:, None], (N_HEADS * t_new, LANES))
    attn_s, kt_out, vt_out = _attn_sample(sink_rows, q_s, kt_s, vt_s, ckt, cvt, t_new, bt)
    state_t = state_conv[l].transpose(1, 0, 2)
    conv_s, state_out_t = _conv_sample(state_t, u_s, wb_slabs, t_new, conv_bb)

    h_p = _outproj(xp, attn_p, conv_p, bdw, ln_g, ln_b, w_out[l], tm)
    h_s = _outproj(xs, attn_s, conv_s, bdw, ln_g, ln_b, w_out[l], tm)
    y_p = _mlp(h_p, g_mlp, w_up[l], w_down[l], g_fin, tm_mlp, tf)
    y_s = _mlp(h_s, g_mlp, w_up[l], w_down[l], g_fin, tm_mlp, tf)

    y_prompt = y_p.reshape(n_batch, seq, D_MODEL)
    y_sample = y_s.reshape(n_dec, t_new, D_MODEL)
    k_last = k_p.reshape(n_batch, seq, KV_WIDTH)[:, seq - WINDOW:]
    v_last = v_p.reshape(n_batch, seq, KV_WIDTH)[:, seq - WINDOW:]
    new_k_prompt = k_last.reshape(n_batch, WINDOW, N_KV_HEADS, HEAD_DIM)[None]
    new_v_prompt = v_last.reshape(n_batch, WINDOW, N_KV_HEADS, HEAD_DIM)[None]
    u_last = u_p.reshape(N_SLAB, n_batch, seq, LANES)[:, :, seq - (CONV_K - 1):]
    new_conv_prompt = u_last.transpose(1, 2, 0, 3).reshape(n_batch, CONV_K - 1, CONV_CH)[None]
    new_k_sample = kt_out.reshape(n_dec, N_KV_HEADS, HEAD_DIM, window).transpose(0, 3, 1, 2)[None]
    new_v_sample = vt_out.reshape(n_dec, N_KV_HEADS, HEAD_DIM, window).transpose(0, 3, 1, 2)[None]
    new_conv_sample = state_out_t.transpose(1, 0, 2)[None]
    return (y_prompt, y_sample, new_k_prompt, new_v_prompt, new_conv_prompt,
            new_k_sample, new_v_sample, new_conv_sample)
```

```python
import functools

import jax
import jax.numpy as jnp
from jax import lax
from jax.experimental import pallas as pl
from jax.experimental.pallas import tpu as pltpu

D_MODEL = 2048
N_META = 16
HEAD_DIM = 64
ATTN_WIDTH = D_MODEL // 2
CONV_CH = D_MODEL - ATTN_WIDTH
N_HEADS = ATTN_WIDTH // HEAD_DIM
N_KV_HEADS = 4
GQA_GROUP = N_HEADS // N_KV_HEADS
KV_WIDTH = N_KV_HEADS * HEAD_DIM
IN_COLS = ATTN_WIDTH + 2 * KV_WIDTH + 2 * CONV_CH
WINDOW = 128
BLOCK = 128
ROPE_THETA = 500000.0
ROT_DIM = HEAD_DIM // 4
CONV_K = 31
D_FF = 4 * D_MODEL
RMS_EPS = 1e-6
LN_EPS = 1e-5
MASK_VALUE = -1e30
PAST_LEN = 8192

LANES = 128
SUBLANES = 8
VMEM_LIMIT = 56 * 1024 * 1024
VMEM_LIMIT_MLP = 62 * 1024 * 1024
N_SLAB = CONV_CH // LANES

F32 = jnp.float32
BF16 = jnp.bfloat16

K_OFF = ATTN_WIDTH
A_OFF = ATTN_WIDTH + 2 * KV_WIDTH
B_OFF = A_OFF + CONV_CH
NT_DIMS = (((1,), (1,)), ((), ()))
LOG2_E = 1.4426950408889634
Q_SCALE = HEAD_DIM ** -0.5 * LOG2_E


def _params(*sem):
    return pltpu.CompilerParams(dimension_semantics=sem, vmem_limit_bytes=VMEM_LIMIT)


def _const_spec(shape):
    nd = len(shape)
    return pl.BlockSpec(shape, lambda *_: (0,) * nd, pipeline_mode=pl.Buffered(1))


def _rms(x, g):
    ms = jnp.mean(x * x, axis=-1, keepdims=True)
    return (x * lax.rsqrt(ms + RMS_EPS)) * g


def _rope_rows(z, rope_ref):
    return (z * rope_ref[0] + pltpu.roll(z, LANES - ROT_DIM // 2, 1) * rope_ref[1]
            + pltpu.roll(z, ROT_DIM // 2, 1) * rope_ref[2])


def _proj(xn_ref, w_ref, off, width):
    return jnp.dot(xn_ref[...], w_ref[:, off:off + width].astype(BF16), preferred_element_type=F32)


def _write_q(xn_ref, w_ref, rope_ref, q_ref):
    half = ATTN_WIDTH // 2
    for c in range(2):
        acc = _proj(xn_ref, w_ref, c * half, half)
        for cc in range(half // LANES):
            z = _rope_rows(acc[:, cc * LANES:(cc + 1) * LANES], rope_ref) * Q_SCALE
            q_ref[:, c * half + cc * LANES:c * half + (cc + 1) * LANES] = z.astype(q_ref.dtype)


def _glu(xn_ref, w_ref, c):
    half = CONV_CH // 2
    a = _proj(xn_ref, w_ref, A_OFF + c * half, half)
    b = _proj(xn_ref, w_ref, B_OFF + c * half, half)
    return a * jax.nn.sigmoid(b)


def _inproj_prompt_kernel(x_ref, g_ref, w_ref, rope_ref, q_ref, k_ref, v_ref, u_ref, xn_ref):
    xn_ref[...] = _rms(x_ref[...], g_ref[...]).astype(BF16)
    _write_q(xn_ref, w_ref, rope_ref, q_ref)
    _write_kvu(xn_ref, w_ref, rope_ref, k_ref, v_ref, u_ref)


def _write_kvu(xn_ref, w_ref, rope_ref, k_ref, v_ref, u_ref):
    acc = _proj(xn_ref, w_ref, K_OFF, 2 * KV_WIDTH)
    for cc in range(KV_WIDTH // LANES):
        k_ref[:, cc * LANES:(cc + 1) * LANES] = _rope_rows(acc[:, cc * LANES:(cc + 1) * LANES], rope_ref)
    v_ref[...] = acc[:, KV_WIDTH:]
    per = N_SLAB // 2
    for c in range(2):
        u = _glu(xn_ref, w_ref, c)
        for cc in range(per):
            u_ref[c * per + cc] = u[:, cc * LANES:(cc + 1) * LANES]


def _inproj_prompt(x, g, w_in, rope_tab, tm):
    m = x.shape[0]
    nrt = rope_tab.shape[1] // tm
    return pl.pallas_call(
        _inproj_prompt_kernel,
        grid=(m // tm,),
        in_specs=[
            pl.BlockSpec((tm, D_MODEL), lambda i: (i, 0)),
            _const_spec((1, D_MODEL)),
            _const_spec((D_MODEL, IN_COLS)),
            pl.BlockSpec((3, tm, LANES), lambda i: (0, i % nrt, 0)),
        ],
        out_specs=[
            pl.BlockSpec((tm, ATTN_WIDTH), lambda i: (i, 0)),
            pl.BlockSpec((tm, KV_WIDTH), lambda i: (i, 0)),
            pl.BlockSpec((tm, KV_WIDTH), lambda i: (i, 0)),
            pl.BlockSpec((N_SLAB, tm, LANES), lambda i: (0, i, 0)),
        ],
        out_shape=[
            jax.ShapeDtypeStruct((m, ATTN_WIDTH), BF16),
            jax.ShapeDtypeStruct((m, KV_WIDTH), F32),
            jax.ShapeDtypeStruct((m, KV_WIDTH), F32),
            jax.ShapeDtypeStruct((N_SLAB, m, LANES), F32),
        ],
        scratch_shapes=[pltpu.VMEM((tm, D_MODEL), BF16)],
        compiler_params=_params("parallel"),
        name="inproj_prompt",
    )(x, g, w_in, rope_tab)


def _inproj_sample_kernel(x_ref, g_ref, w_ref, rope_ref, ropet_ref, meta_ref, ropem_ref,
                          q_ref, kt_ref, vt_ref, u_ref, km_ref, vm_ref, um_ref, xn_ref, xm_ref):
    @pl.when(pl.program_id(0) == 0)
    def _():
        xm_ref[...] = _rms(meta_ref[...], g_ref[...]).astype(BF16)
        _write_kvu(xm_ref, w_ref, ropem_ref, km_ref, vm_ref, um_ref)

    xn_ref[...] = _rms(x_ref[...], g_ref[...]).astype(BF16)
    _write_q(xn_ref, w_ref, rope_ref, q_ref)
    kvt = _proj(xn_ref, w_ref, K_OFF, 2 * KV_WIDTH).T
    kt = kvt[0:KV_WIDTH]
    sh = ROT_DIM // 2
    up = jnp.concatenate([kt[sh:], kt[:sh]], axis=0)
    dn = jnp.concatenate([kt[-sh:], kt[:-sh]], axis=0)
    kt_ref[...] = kt * ropet_ref[0] + up * ropet_ref[1] + dn * ropet_ref[2]
    vt_ref[...] = kvt[KV_WIDTH:]
    per = N_SLAB // 2
    for c in range(2):
        u = _glu(xn_ref, w_ref, c)
        for cc in range(per):
            u_ref[c * per + cc] = u[:, cc * LANES:(cc + 1) * LANES]


def _inproj_sample(x, g, w_in, rope_tab, ropet_tab, meta, rope_meta, tm):
    m = x.shape[0]
    n_meta = meta.shape[0]
    return pl.pallas_call(
        _inproj_sample_kernel,
        grid=(m // tm,),
        in_specs=[
            pl.BlockSpec((tm, D_MODEL), lambda i: (i, 0)),
            _const_spec((1, D_MODEL)),
            _const_spec((D_MODEL, IN_COLS)),
            _const_spec((3, tm, LANES)),
            _const_spec((3, KV_WIDTH, tm)),
            _const_spec((n_meta, D_MODEL)),
            _const_spec((3, n_meta, LANES)),
        ],
        out_specs=[
            pl.BlockSpec((tm, ATTN_WIDTH), lambda i: (i, 0)),
            pl.BlockSpec((KV_WIDTH, tm), lambda i: (0, i)),
            pl.BlockSpec((KV_WIDTH, tm), lambda i: (0, i)),
            pl.BlockSpec((N_SLAB, tm, LANES), lambda i: (0, i, 0)),
            pl.BlockSpec((n_meta, KV_WIDTH), lambda i: (0, 0)),
            pl.BlockSpec((n_meta, KV_WIDTH), lambda i: (0, 0)),
            pl.BlockSpec((N_SLAB, n_meta, LANES), lambda i: (0, 0, 0)),
        ],
        out_shape=[
            jax.ShapeDtypeStruct((m, ATTN_WIDTH), F32),
            jax.ShapeDtypeStruct((KV_WIDTH, m), F32),
            jax.ShapeDtypeStruct((KV_WIDTH, m), F32),
            jax.ShapeDtypeStruct((N_SLAB, m, LANES), F32),
            jax.ShapeDtypeStruct((n_meta, KV_WIDTH), F32),
            jax.ShapeDtypeStruct((n_meta, KV_WIDTH), F32),
            jax.ShapeDtypeStruct((N_SLAB, n_meta, LANES), F32),
        ],
        scratch_shapes=[pltpu.VMEM((tm, D_MODEL), BF16), pltpu.VMEM((n_meta, D_MODEL), BF16)],
        compiler_params=_params("arbitrary"),
        name="inproj_sample",
    )(x, g, w_in, rope_tab, ropet_tab, meta, rope_meta)


def _rope_table(pos):
    half = ROT_DIM // 2
    inv = jnp.power(jnp.float32(ROPE_THETA), -jnp.arange(half, dtype=F32) * 2.0 / ROT_DIM)
    ang = pos.astype(F32)[:, None] * inv[None, :]
    cos, sin = jnp.cos(ang), jnp.sin(ang)
    t = pos.shape[0]
    rest = HEAD_DIM - ROT_DIM
    one = jnp.ones((t, rest), F32)
    zero = jnp.zeros((t, rest), F32)
    zh = jnp.zeros((t, half), F32)
    c = jnp.concatenate([cos, cos, one], axis=1)
    s_lo = jnp.concatenate([-sin, zh, zero], axis=1)
    s_hi = jnp.concatenate([zh, sin, zero], axis=1)
    tab = jnp.stack([c, s_lo, s_hi])
    return jnp.tile(tab, (1, 1, LANES // HEAD_DIM))


ATTN_NB = 4


def _attn_prompt_kernel(sink_ref, q_ref, kc_ref, kp_ref, vc_ref, vp_ref, km_ref, vm_ref, o_ref,
                        kk_ref, vv_ref, s_ref, p_ref, inv_ref):
    n = pl.program_id(1)

    @pl.when(n == 0)
    def _():
        pad = jnp.zeros((BLOCK - N_META, KV_WIDTH), F32)
        kk_ref[0, 0:BLOCK - N_META] = pad
        vv_ref[0, 0:BLOCK - N_META] = pad
        kk_ref[0, BLOCK - N_META:BLOCK] = km_ref[...]
        vv_ref[0, BLOCK - N_META:BLOCK] = vm_ref[...]

    @pl.when(n > 0)
    def _():
        kk_ref[0, 0:BLOCK] = kp_ref[...]
        vv_ref[0, 0:BLOCK] = vp_ref[...]

    for sb in range(ATTN_NB):
        k_cur = kc_ref[sb * BLOCK:(sb + 1) * BLOCK]
        v_cur = vc_ref[sb * BLOCK:(sb + 1) * BLOCK]
        kk_ref[sb, BLOCK:2 * BLOCK] = k_cur
        vv_ref[sb, BLOCK:2 * BLOCK] = v_cur
        if sb + 1 < ATTN_NB:
            kk_ref[sb + 1, 0:BLOCK] = k_cur
            vv_ref[sb + 1, 0:BLOCK] = v_cur
    for sb in range(ATTN_NB):
        vv_ref[sb, 0:1, :] = jnp.zeros((1, KV_WIDTH), F32)

    lo = jnp.where(n == 0, BLOCK - N_META, 0)
    r = lax.broadcasted_iota(jnp.int32, (BLOCK, 2 * BLOCK), 0)
    j = lax.broadcasted_iota(jnp.int32, (BLOCK, 2 * BLOCK), 1)
    band = (j > r) & (j <= r + BLOCK)
    shape3 = (BLOCK // SUBLANES, SUBLANES, 2 * BLOCK)
    col0 = lax.broadcasted_iota(jnp.int32, (1, SUBLANES, 2 * BLOCK), 2) == 0
    low_half = lax.broadcasted_iota(jnp.int32, (BLOCK, LANES), 1) < HEAD_DIM
    low_half2 = lax.broadcasted_iota(jnp.int32, (2 * BLOCK, LANES), 1) < HEAD_DIM

    def dup_halves(ref, sb, h):
        xf = ref[sb, :, (h // 2) * LANES:(h // 2 + 1) * LANES]
        x_sw = pltpu.roll(xf, HEAD_DIM, 1)
        both = jnp.where(low_half2, xf, x_sw) if h % 2 == 0 else jnp.where(low_half2, x_sw, xf)
        return both.astype(BF16)

    rows_kv = GQA_GROUP * BLOCK
    rows_blk = N_HEADS * BLOCK
    for sb in range(ATTN_NB):
        mask3 = (band & (j >= lo) if sb == 0 else band).reshape(shape3)
        q_rows = slice(sb * BLOCK, (sb + 1) * BLOCK)
        for h in range(N_KV_HEADS):
            qa = q_ref[q_rows, (2 * h) * LANES:(2 * h + 1) * LANES]
            qb = q_ref[q_rows, (2 * h + 1) * LANES:(2 * h + 2) * LANES]
            zero = jnp.zeros_like(qa)
            lhs = jnp.concatenate([jnp.where(low_half, qa, zero), jnp.where(low_half, qb, zero),
                                   jnp.where(low_half, zero, qa), jnp.where(low_half, zero, qb)], axis=0)
            s_all = lax.dot_general(lhs, dup_halves(kk_ref, sb, h), NT_DIMS, preferred_element_type=F32)
            heads = (4 * h, 4 * h + 2, 4 * h + 1, 4 * h + 3)
            for g in range(GQA_GROUP):
                fill = jnp.where(col0, sink_ref[heads[g]], MASK_VALUE)
                s = jnp.where(mask3, s_all[g * BLOCK:(g + 1) * BLOCK].reshape(shape3), fill)
                row0 = sb * rows_blk + h * rows_kv + g * BLOCK
                s_ref[row0:row0 + BLOCK] = s.reshape(BLOCK, 2 * BLOCK)

    s = s_ref[...]
    mx = jnp.max(s, axis=-1, keepdims=True)
    e = jnp.exp2(s - mx)
    p_ref[...] = e.astype(BF16)
    inv_ref[...] = jnp.broadcast_to(1.0 / jnp.sum(e, axis=-1, keepdims=True), inv_ref.shape)

    for sb in range(ATTN_NB):
        q_rows = slice(sb * BLOCK, (sb + 1) * BLOCK)
        for h in range(N_KV_HEADS):
            base = sb * rows_blk + h * rows_kv
            o_all = jnp.dot(p_ref[base:base + rows_kv], dup_halves(vv_ref, sb, h), preferred_element_type=F32)
            o = [o_all[g * BLOCK:(g + 1) * BLOCK] * inv_ref[base + g * BLOCK:base + (g + 1) * BLOCK]
                 for g in range(GQA_GROUP)]
            o_ref[q_rows, (2 * h) * LANES:(2 * h + 1) * LANES] = jnp.where(low_half, o[0], o[2]).astype(o_ref.dtype)
            o_ref[q_rows, (2 * h + 1) * LANES:(2 * h + 2) * LANES] = (
                jnp.where(low_half, o[1], o[3]).astype(o_ref.dtype))


def _attn_prompt(sinks, q, k, v, k_meta, v_meta, n_batch, n_blk):
    assert n_blk % ATTN_NB == 0
    n_step = n_blk // ATTN_NB
    rows = ATTN_NB * BLOCK
    kv_cur = pl.BlockSpec((rows, KV_WIDTH), lambda b, n: (b * n_step + n, 0))
    kv_prev = pl.BlockSpec((BLOCK, KV_WIDTH), lambda b, n: (b * n_blk + jnp.maximum(n * ATTN_NB - 1, 0), 0))
    return pl.pallas_call(
        _attn_prompt_kernel,
        grid=(n_batch, n_step),
        in_specs=[
            pl.BlockSpec(memory_space=pltpu.SMEM),
            pl.BlockSpec((rows, ATTN_WIDTH), lambda b, n: (b * n_step + n, 0)),
            kv_cur, kv_prev, kv_cur, kv_prev,
            _const_spec((N_META, KV_WIDTH)),
            _const_spec((N_META, KV_WIDTH)),
        ],
        out_specs=pl.BlockSpec((rows, ATTN_WIDTH), lambda b, n: (b * n_step + n, 0)),
        out_shape=jax.ShapeDtypeStruct((n_batch * n_blk * BLOCK, ATTN_WIDTH), BF16),
        scratch_shapes=[
            pltpu.VMEM((ATTN_NB, 2 * BLOCK, KV_WIDTH), F32),
            pltpu.VMEM((ATTN_NB, 2 * BLOCK, KV_WIDTH), F32),
            pltpu.VMEM((ATTN_NB * N_HEADS * BLOCK, 2 * BLOCK), F32),
            pltpu.VMEM((ATTN_NB * N_HEADS * BLOCK, 2 * BLOCK), BF16),
            pltpu.VMEM((ATTN_NB * N_HEADS * BLOCK, LANES), F32),
        ],
        compiler_params=_params("parallel", "arbitrary"),
        name="attn_prompt",
    )(sinks, q, k, k, v, v, k_meta, v_meta)


def _attn_sample_kernel(t_new, bt, sink_ref, q_ref, kn_ref, vn_ref, ck_ref, cv_ref,
                        o_ref, ko_ref, vo_ref, s_ref, p_ref, inv_ref, osc_ref):
    nrow = N_HEADS * t_new
    w = ck_ref.shape[2]
    assert bt * t_new == LANES and w == LANES and nrow == LANES
    zero_blk = jnp.zeros((t_new, LANES), F32)

    r = lax.broadcasted_iota(jnp.int32, (nrow, 2 * w), 0) % t_new
    j = lax.broadcasted_iota(jnp.int32, (nrow, 2 * w), 1)
    cache_ok = (j > r) & (j < w)
    fill = jnp.where(j == 0, sink_ref[...][:, 0:1], MASK_VALUE)
    key0 = lax.broadcasted_iota(jnp.int32, (bt * nrow, w), 1) == 0
    low_half =lax.broadcasted_iota(jnp.int32, (t_new, LANES), 1) < HEAD_DIM
    tail = lax.broadcasted_iota(jnp.int32, (KV_WIDTH, LANES), 1) >= w - t_new
    kn = kn_ref[...]
    vn = vn_ref[...]
    kn_bf = kn.astype(BF16)
    vn_bf = vn.astype(BF16)

    for i in range(bt):
        new_shift = (w - t_new - i * t_new) % LANES
        ko_ref[i] = jnp.where(tail, pltpu.roll(kn, new_shift, 1), pltpu.roll(ck_ref[i], w - t_new, 1))
        vo_ref[i] = jnp.where(tail, pltpu.roll(vn, new_shift, 1), pltpu.roll(cv_ref[i], w - t_new, 1))

    for i in range(bt):
        qrows = []
        for t in range(N_HEADS):
            h = t // GQA_GROUP
            qv = q_ref[i * t_new:(i + 1) * t_new, (t // 2) * LANES:(t // 2 + 1) * LANES]
            if t % 2 != h % 2:
                qv = pltpu.roll(qv, HEAD_DIM, 1)
            keep = low_half if h % 2 == 0 else jnp.logical_not(low_half)
            cols = [zero_blk] * (KV_WIDTH // LANES)
            cols[h // 2] = jnp.where(keep, qv, 0.0)
            qrows.append(jnp.concatenate(cols, axis=1))
        qrow = jnp.concatenate(qrows, axis=0).astype(BF16)

        rhs_k = jnp.concatenate([ck_ref[i].astype(BF16), kn_bf], axis=1)
        s = jnp.dot(qrow, rhs_k, preferred_element_type=F32)
        own = w + i * t_new
        mask = cache_ok | ((j >= own) & (j <= own + r))
        s_ref[i * nrow:(i + 1) * nrow] = jnp.where(mask, s, fill)

    s = s_ref[...]
    mx = jnp.max(s, axis=-1, keepdims=True)
    e = jnp.exp2(s - mx)
    den = jnp.sum(e, axis=-1, keepdims=True)
    p_ref[:, 0:w] = jnp.where(key0, 0.0, e[:, 0:w]).astype(BF16)
    p_ref[:, w:2 * w] = e[:, w:2 * w].astype(BF16)
    inv_ref[...] = jnp.broadcast_to(1.0 / den, inv_ref.shape)

    for i in range(bt):
        rhs_v = jnp.concatenate([cv_ref[i].astype(BF16), vn_bf], axis=1)
        o_all = lax.dot_general(p_ref[i * nrow:(i + 1) * nrow], rhs_v, NT_DIMS, preferred_element_type=F32)
        for t2 in range(N_HEADS // 2):
            parts = []
            for side in range(2):
                t = 2 * t2 + side
                h = t // GQA_GROUP
                blk = o_all[t * t_new:(t + 1) * t_new, (h // 2) * LANES:(h // 2 + 1) * LANES]
                blk = blk * inv_ref[i * nrow + t * t_new:i * nrow + (t + 1) * t_new, :]
                if t % 2 != h % 2:
                    blk = pltpu.roll(blk, HEAD_DIM, 1)
                parts.append(blk)
            osc_ref[i * t_new:(i + 1) * t_new, t2 * LANES:(t2 + 1) * LANES] = (
                jnp.where(low_half, parts[0], parts[1]))

    o_ref[...] = osc_ref[...].astype(o_ref.dtype)


def _attn_sample(sink_rows, q, kt_new, vt_new, cache_kt, cache_vt, t_new, bt):
    nb, _, w = cache_kt.shape
    rows = bt * t_new
    cache_spec = pl.BlockSpec((bt, KV_WIDTH, w), lambda s: (s, 0, 0))
    new_spec = pl.BlockSpec((KV_WIDTH, rows), lambda s: (0, s))
    return pl.pallas_call(
        functools.partial(_attn_sample_kernel, t_new, bt),
        grid=(nb // bt,),
        in_specs=[
            _const_spec(sink_rows.shape),
            pl.BlockSpec((rows, ATTN_WIDTH), lambda s: (s, 0)),
            new_spec, new_spec, cache_spec, cache_spec,
        ],
        out_specs=[pl.BlockSpec((rows, ATTN_WIDTH), lambda s: (s, 0)), cache_spec, cache_spec],
        out_shape=[
            jax.ShapeDtypeStruct((nb * t_new, ATTN_WIDTH), BF16),
            jax.ShapeDtypeStruct(cache_kt.shape, F32),
            jax.ShapeDtypeStruct(cache_vt.shape, F32),
        ],
        scratch_shapes=[
            pltpu.VMEM((bt * N_HEADS * t_new, 2 * w), F32),
            pltpu.VMEM((bt * N_HEADS * t_new, 2 * w), BF16),
            pltpu.VMEM((bt * N_HEADS * t_new, LANES), F32),
            pltpu.VMEM((rows, ATTN_WIDTH), F32),
        ],
        compiler_params=_params("parallel"),
        name="attn_sample",
    )(sink_rows, q, kt_new, vt_new, cache_kt, cache_vt)


def _ln_swish(y, g, b):
    mu = jnp.mean(y, axis=-1, keepdims=True)
    d = y - mu
    var = jnp.mean(d * d, axis=-1, keepdims=True)
    yn = d * lax.rsqrt(var + LN_EPS) * g + b
    return yn * jax.nn.sigmoid(yn)


CONV_HALO = 32
CONV_SEG = 252
CONV_SEG_STEP = 126


def _conv_prompt_kernel(seq, u_ref, meta_ref, wb_ref, o_ref, win_ref):
    pad = CONV_HALO - N_META
    for c in range(N_SLAB):
        win_ref[c, 0:pad] = jnp.zeros((pad, LANES), F32)
        win_ref[c, pad:CONV_HALO] = meta_ref[c]
        win_ref[c, CONV_HALO:CONV_HALO + seq] = u_ref[c]
    first = CONV_HALO - (CONV_K - 1)

    rem_base = SUBLANES * CONV_SEG
    rem_seg = (seq - rem_base) // SUBLANES
    n_main = CONV_SEG // CONV_SEG_STEP

    def taps(c, wts, base, seg, n_g, i0):
        accs = [None] * n_g
        for m in range(n_g + CONV_K - 1):
            v = win_ref[c, pl.ds(base + first + i0 + m, SUBLANES, stride=seg), :]
            for g in range(max(0, m - CONV_K + 1), min(n_g, m + 1)):
                term = v * wts[m - g]
                accs[g] = term if accs[g] is None else accs[g] + term
        for g in range(n_g):
            o_ref[c, pl.ds(base + i0 + g, SUBLANES, stride=seg), :] = accs[g]

    for c in range(N_SLAB):
        wts = [wb_ref[tap, c] for tap in range(CONV_K)]

        def body(ci, carry, c=c, wts=wts):
            taps(c, wts, 0, CONV_SEG, CONV_SEG_STEP, ci * CONV_SEG_STEP)
            return carry

        lax.fori_loop(0, n_main, body, 0)
        taps(c, wts, rem_base, rem_seg, rem_seg, 0)


def _conv_prompt(u_slabs, u_meta, wb, n_batch, seq):
    assert CONV_SEG % CONV_SEG_STEP == 0 and seq > SUBLANES * CONV_SEG
    assert (seq - SUBLANES * CONV_SEG) % SUBLANES == 0
    blk = pl.BlockSpec((N_SLAB, seq, LANES), lambda b: (0, b, 0))
    return pl.pallas_call(
        functools.partial(_conv_prompt_kernel, seq),
        grid=(n_batch,),
        in_specs=[
            blk,
            _const_spec((N_SLAB, N_META, LANES)),
            _const_spec((CONV_K, N_SLAB, SUBLANES, LANES)),
        ],
        out_specs=blk,
        out_shape=jax.ShapeDtypeStruct((N_SLAB, n_batch * seq, LANES), F32),
        scratch_shapes=[pltpu.VMEM((N_SLAB, CONV_HALO + seq, LANES), F32)],
        compiler_params=_params("parallel"),
        name="conv_prompt",
    )(u_slabs, u_meta, wb)


def _conv_sample_kernel(t_new, st_ref, u_ref, wb_ref, o_ref, so_ref):
    hist = st_ref.shape[0]
    bb = st_ref.shape[1]
    so_ref[0:hist - t_new] = st_ref[t_new:hist]

    for c in range(N_SLAB):
        lanes = slice(c * LANES, (c + 1) * LANES)
        for t in range(t_new):
            so_ref[hist - t_new + t, :, lanes] = u_ref[c, pl.ds(t, bb, stride=t_new), :]

        def src(k, rc, c=c, lanes=lanes):
            if k < hist:
                return st_ref[k, pl.ds(pl.multiple_of(rc * SUBLANES, SUBLANES), SUBLANES), lanes]
            return u_ref[c, pl.ds(rc * SUBLANES * t_new + (k - hist), SUBLANES, stride=t_new), :]

        wts = [wb_ref[tap, c] for tap in range(CONV_K)]

        def per_rows(rc, carry, c=c, src=src, wts=wts):
            accs = [None] * t_new
            for k in range(hist + t_new):
                v = src(k, rc)
                for t in range(max(0, k - CONV_K + 1), min(t_new, k + 1)):
                    term = v * wts[k - t]
                    accs[t] = term if accs[t] is None else accs[t] + term
            for t in range(t_new):
                o_ref[c, pl.ds(rc * SUBLANES * t_new + t, SUBLANES, stride=t_new), :] = accs[t]
            return carry

        for rc in range(bb // SUBLANES):
            per_rows(rc, 0)


def _conv_sample(state_t, u_slabs, wb, t_new, bb):
    hist, nb, _ = state_t.shape
    rows = bb * t_new
    slab_spec = pl.BlockSpec((N_SLAB, rows, LANES), lambda s: (0, s, 0))
    return pl.pallas_call(
        functools.partial(_conv_sample_kernel, t_new),
        grid=(nb // bb,),
        in_specs=[
            pl.BlockSpec((hist, bb, CONV_CH), lambda s: (0, s, 0)),
            slab_spec,
            _const_spec((CONV_K, N_SLAB, SUBLANES, LANES)),
        ],
        out_specs=[slab_spec, pl.BlockSpec((hist, bb, CONV_CH), lambda s: (0, s, 0))],
        out_shape=[
            jax.ShapeDtypeStruct((N_SLAB, nb * t_new, LANES), F32),
            jax.ShapeDtypeStruct(state_t.shape, F32),
        ],
        compiler_params=_params("parallel"),
        name="conv_sample",
    )(state_t, u_slabs, wb)


def _outproj_kernel(x_ref, a_ref, c_ref, bdw_ref, g_ref, b_ref, wo_ref, h_ref, cbf_ref):
    y = jnp.concatenate([c_ref[c] for c in range(N_SLAB)], axis=1)
    cbf_ref[...] = _ln_swish(y + bdw_ref[...], g_ref[...], b_ref[...]).astype(BF16)
    mix = jnp.dot(a_ref[...], wo_ref[0:ATTN_WIDTH].astype(BF16), preferred_element_type=F32)
    mix = mix + jnp.dot(cbf_ref[...], wo_ref[ATTN_WIDTH:].astype(BF16), preferred_element_type=F32)
    h_ref[...] = x_ref[...] + mix


def _outproj(x, attn, conv_taps, b_dw, ln_g, ln_b, w_out, tm):
    m = x.shape[0]
    return pl.pallas_call(
        _outproj_kernel,
        grid=(m // tm,),
        in_specs=[
            pl.BlockSpec((tm, D_MODEL), lambda i: (i, 0)),
            pl.BlockSpec((tm, ATTN_WIDTH), lambda i: (i, 0)),
            pl.BlockSpec((N_SLAB, tm, LANES), lambda i: (0, i, 0)),
            _const_spec((1, CONV_CH)),
            _const_spec((1, CONV_CH)),
            _const_spec((1, CONV_CH)),
            _const_spec((D_MODEL, D_MODEL)),
        ],
        out_specs=pl.BlockSpec((tm, D_MODEL), lambda i: (i, 0)),
        out_shape=jax.ShapeDtypeStruct((m, D_MODEL), F32),
        scratch_shapes=[pltpu.VMEM((tm, CONV_CH), BF16)],
        compiler_params=_params("parallel"),
        name="outproj",
    )(x, attn, conv_taps, b_dw, ln_g, ln_b, w_out)


def _mlp_kernel(h_ref, g2_ref, wu_ref, wd_ref, gf_ref, y_ref, xn_ref):
    f = pl.program_id(1)

    @pl.when(f == 0)
    def _():
        h1 = h_ref[...]
        y_ref[...] = h1
        xn_ref[...] = _rms(h1, g2_ref[...]).astype(BF16)

    up = jnp.dot(xn_ref[...], wu_ref[...].astype(BF16), preferred_element_type=F32)
    act = jnp.maximum(up, 0.0)
    act = (act * act).astype(BF16)
    y_ref[...] += jnp.dot(act, wd_ref[...].astype(BF16), preferred_element_type=F32)

    @pl.when(f == pl.num_programs(1) - 1)
    def _():
        y_ref[...] = _rms(y_ref[...], gf_ref[...])


def _mlp(h1, g2, wu_bf, wd_bf, gf, tm, tf):
    m = h1.shape[0]
    return pl.pallas_call(
        _mlp_kernel,
        grid=(m // tm, D_FF // tf),
        in_specs=[
            pl.BlockSpec((tm, D_MODEL), lambda i, f: (i, 0)),
            _const_spec((1, D_MODEL)),
            pl.BlockSpec((D_MODEL, tf), lambda i, f: (0, f)),
            pl.BlockSpec((tf, D_MODEL), lambda i, f: (f, 0)),
            _const_spec((1, D_MODEL)),
        ],
        out_specs=pl.BlockSpec((tm, D_MODEL), lambda i, f: (i, 0)),
        out_shape=jax.ShapeDtypeStruct((m, D_MODEL), F32),
        scratch_shapes=[pltpu.VMEM((tm, D_MODEL), BF16)],
        compiler_params=pltpu.CompilerParams(dimension_semantics=("parallel", "arbitrary"),
                                             vmem_limit_bytes=VMEM_LIMIT_MLP),
        name="mlp",
    )(h1, g2, wu_bf, wd_bf, gf)


def kernel(x_prompt, x_sample, cache_k, cache_v, state_conv, meta_tokens, norm_mix, w_in, attn_sinks,
           w_dw, b_dw, conv_ln_g, conv_ln_b, w_out, norm_mlp, w_up, w_down, norm_final):
    n_batch, seq, _ = x_prompt.shape
    n_dec, t_new, _ = x_sample.shape
    depth = w_in.shape[0]
    window = cache_k.shape[2]
    assert depth == 1 and seq % BLOCK == 0 and t_new == SUBLANES and window == WINDOW
    l = 0
    n_blk = seq // BLOCK
    tm = 512
    tm_mlp = 1024
    tf = 512
    bt = LANES // t_new
    conv_bb = 32
    assert (n_batch * seq) % tm_mlp == 0 and (n_dec * t_new) % tm_mlp == 0 and n_dec % conv_bb == 0

    g_mix = norm_mix[l][None, :]
    g_mlp = norm_mlp[l][None, :]
    g_fin = norm_final[None, :]
    sinks = attn_sinks[l].astype(F32) * LOG2_E
    bdw = b_dw[l][None, :]
    ln_g = conv_ln_g[l][None, :]
    ln_b = conv_ln_b[l][None, :]
    wb_slabs = jnp.broadcast_to(w_dw[l].reshape(CONV_K, N_SLAB, 1, LANES), (CONV_K, N_SLAB, SUBLANES, LANES))

    rope_p = _rope_table(N_META + jnp.arange(seq, dtype=jnp.int32))
    rope_s = _rope_table(PAST_LEN + (jnp.arange(tm, dtype=jnp.int32) % t_new))
    ropet_s = jnp.tile(rope_s[:, :, :HEAD_DIM].transpose(0, 2, 1), (1, N_KV_HEADS, 1))
    rope_m = _rope_table(jnp.arange(N_META, dtype=jnp.int32))

    xp = x_prompt.reshape(n_batch * seq, D_MODEL)
    xs = x_sample.reshape(n_dec * t_new, D_MODEL)

    q_p, k_p, v_p, u_p = _inproj_prompt(xp, g_mix, w_in[l], rope_p, tm)
    q_s, kt_s, vt_s, u_s, k_m, v_m, u_m = _inproj_sample(xs, g_mix, w_in[l], rope_s, ropet_s,
                                                         meta_tokens.astype(F32), rope_m, tm)

    attn_p = _attn_prompt(sinks, q_p, k_p, v_p, k_m, v_m, n_batch, n_blk)
    conv_p = _conv_prompt(u_p, u_m, wb_slabs, n_batch, seq)

    ckt = cache_k[l].transpose(0, 2, 3, 1).reshape(n_dec, KV_WIDTH, window)
    cvt = cache_v[l].transpose(0, 2, 3, 1).reshape(n_dec, KV_WIDTH, window)
    sink_rows = jnp.broadcast_to(jnp.repeat(sinks, t_new)[:, None], (N_HEADS * t_new, LANES))
    attn_s, kt_out, vt_out = _attn_sample(sink_rows, q_s, kt_s, vt_s, ckt, cvt, t_new, bt)
    state_t = state_conv[l].transpose(1, 0, 2)
    conv_s, state_out_t = _conv_sample(state_t, u_s, wb_slabs, t_new, conv_bb)

    h_p = _outproj(xp, attn_p, conv_p, bdw, ln_g, ln_b, w_out[l], tm)
    h_s = _outproj(xs, attn_s, conv_s, bdw, ln_g, ln_b, w_out[l], tm)
    y_p = _mlp(h_p, g_mlp, w_up[l], w_down[l], g_fin, tm_mlp, tf)
    y_s = _mlp(h_s, g_mlp, w_up[l], w_down[l], g_fin, tm_mlp, tf)

    y_prompt = y_p.reshape(n_batch, seq, D_MODEL)
    y_sample = y_s.reshape(n_dec, t_new, D_MODEL)
    k_last = k_p.reshape(n_batch, seq, KV_WIDTH)[:, seq - WINDOW:]
    v_last = v_p.reshape(n_batch, seq, KV_WIDTH)[:, seq - WINDOW:]
    new_k_prompt = k_last.reshape(n_batch, WINDOW, N_KV_HEADS, HEAD_DIM)[None]
    new_v_prompt = v_last.reshape(n_batch, WINDOW, N_KV_HEADS, HEAD_DIM)[None]
    u_last = u_p.reshape(N_SLAB, n_batch, seq, LANES)[:, :, seq - (CONV_K - 1):]
    new_conv_prompt = u_last.transpose(1, 2, 0, 3).reshape(n_batch, CONV_K - 1, CONV_CH)[None]
    new_k_sample = kt_out.reshape(n_dec, N_KV_HEADS, HEAD_DIM, window).transpose(0, 3, 1, 2)[None]
    new_v_sample = vt_out.reshape(n_dec, N_KV_HEADS, HEAD_DIM, window).transpose(0, 3, 1, 2)[None]
    new_conv_sample = state_out_t.transpose(1, 0, 2)[None]
    return (y_prompt, y_sample, new_k_prompt, new_v_prompt, new_conv_prompt,
            new_k_sample, new_v_sample, new_conv_sample)
```

```python
import functools

import jax
import jax.numpy as jnp
from jax import lax
from jax.experimental import pallas as pl
from jax.experimental.pallas import tpu as pltpu

D_MODEL = 2048
N_META = 16
HEAD_DIM = 64
ATTN_WIDTH = D_MODEL // 2
CONV_CH = D_MODEL - ATTN_WIDTH
N_HEADS = ATTN_WIDTH // HEAD_DIM
N_KV_HEADS = 4
GQA_GROUP = N_HEADS // N_KV_HEADS
KV_WIDTH = N_KV_HEADS * HEAD_DIM
IN_COLS = ATTN_WIDTH + 2 * KV_WIDTH + 2 * CONV_CH
WINDOW = 128
BLOCK = 128
ROPE_THETA = 500000.0
ROT_DIM = HEAD_DIM // 4
CONV_K = 31
D_FF = 4 * D_MODEL
RMS_EPS = 1e-6
LN_EPS = 1e-5
MASK_VALUE = -1e30
PAST_LEN = 8192

LANES = 128
SUBLANES = 8
VMEM_LIMIT = 56 * 1024 * 1024
VMEM_LIMIT_MLP = 62 * 1024 * 1024
N_SLAB = CONV_CH // LANES

F32 = jnp.float32
BF16 = jnp.bfloat16

K_OFF = ATTN_WIDTH
A_OFF = ATTN_WIDTH + 2 * KV_WIDTH
B_OFF = A_OFF + CONV_CH
NT_DIMS = (((1,), (1,)), ((), ()))
LOG2_E = 1.4426950408889634
Q_SCALE = HEAD_DIM ** -0.5 * LOG2_E


def _params(*sem):
    return pltpu.CompilerParams(dimension_semantics=sem, vmem_limit_bytes=VMEM_LIMIT)


def _const_spec(shape):
    nd = len(shape)
    return pl.BlockSpec(shape, lambda *_: (0,) * nd, pipeline_mode=pl.Buffered(1))


def _rms(x, g):
    ms = jnp.mean(x * x, axis=-1, keepdims=True)
    return (x * lax.rsqrt(ms + RMS_EPS)) * g


def _rope_rows(z, rope_ref):
    return (z * rope_ref[0] + pltpu.roll(z, LANES - ROT_DIM // 2, 1) * rope_ref[1]
            + pltpu.roll(z, ROT_DIM // 2, 1) * rope_ref[2])


def _proj(xn_ref, w_ref, off, width):
    return jnp.dot(xn_ref[...], w_ref[:, off:off + width].astype(BF16), preferred_element_type=F32)


def _write_q(xn_ref, w_ref, rope_ref, q_ref):
    half = ATTN_WIDTH // 2
    for c in range(2):
        acc = _proj(xn_ref, w_ref, c * half, half)
        for cc in range(half // LANES):
            z = _rope_rows(acc[:, cc * LANES:(cc + 1) * LANES], rope_ref) * Q_SCALE
            q_ref[:, c * half + cc * LANES:c * half + (cc + 1) * LANES] = z.astype(q_ref.dtype)


def _glu(xn_ref, w_ref, c):
    half = CONV_CH // 2
    a = _proj(xn_ref, w_ref, A_OFF + c * half, half)
    b = _proj(xn_ref, w_ref, B_OFF + c * half, half)
    return a * jax.nn.sigmoid(b)


def _inproj_prompt_kernel(x_ref, g_ref, w_ref, rope_ref, q_ref, k_ref, v_ref, u_ref, xn_ref):
    xn_ref[...] = _rms(x_ref[...], g_ref[...]).astype(BF16)
    _write_q(xn_ref, w_ref, rope_ref, q_ref)
    _write_kvu(xn_ref, w_ref, rope_ref, k_ref, v_ref, u_ref)


def _write_kvu(xn_ref, w_ref, rope_ref, k_ref, v_ref, u_ref):
    acc = _proj(xn_ref, w_ref, K_OFF, 2 * KV_WIDTH)
    for cc in range(KV_WIDTH // LANES):
        k_ref[:, cc * LANES:(cc + 1) * LANES] = _rope_rows(acc[:, cc * LANES:(cc + 1) * LANES], rope_ref)
    v_ref[...] = acc[:, KV_WIDTH:]
    per = N_SLAB // 2
    for c in range(2):
        u = _glu(xn_ref, w_ref, c)
        for cc in range(per):
            u_ref[c * per + cc] = u[:, cc * LANES:(cc + 1) * LANES]


def _inproj_prompt(x, g, w_in, rope_tab, tm):
    m = x.shape[0]
    nrt = rope_tab.shape[1] // tm
    return pl.pallas_call(
        _inproj_prompt_kernel,
        grid=(m // tm,),
        in_specs=[
            pl.BlockSpec((tm, D_MODEL), lambda i: (i, 0)),
            _const_spec((1, D_MODEL)),
            _const_spec((D_MODEL, IN_COLS)),
            pl.BlockSpec((3, tm, LANES), lambda i: (0, i % nrt, 0)),
        ],
        out_specs=[
            pl.BlockSpec((tm, ATTN_WIDTH), lambda i: (i, 0)),
            pl.BlockSpec((tm, KV_WIDTH), lambda i: (i, 0)),
            pl.BlockSpec((tm, KV_WIDTH), lambda i: (i, 0)),
            pl.BlockSpec((N_SLAB, tm, LANES), lambda i: (0, i, 0)),
        ],
        out_shape=[
            jax.ShapeDtypeStruct((m, ATTN_WIDTH), BF16),
            jax.ShapeDtypeStruct((m, KV_WIDTH), F32),
            jax.ShapeDtypeStruct((m, KV_WIDTH), F32),
            jax.ShapeDtypeStruct((N_SLAB, m, LANES), F32),
        ],
        scratch_shapes=[pltpu.VMEM((tm, D_MODEL), BF16)],
        compiler_params=_params("parallel"),
        name="inproj_prompt",
    )(x, g, w_in, rope_tab)


def _inproj_sample_kernel(x_ref, g_ref, w_ref, rope_ref, ropet_ref, meta_ref, ropem_ref,
                          q_ref, kt_ref, vt_ref, u_ref, km_ref, vm_ref, um_ref, xn_ref, xm_ref):
    @pl.when(pl.program_id(0) == 0)
    def _():
        xm_ref[...] = _rms(meta_ref[...], g_ref[...]).astype(BF16)
        _write_kvu(xm_ref, w_ref, ropem_ref, km_ref, vm_ref, um_ref)

    xn_ref[...] = _rms(x_ref[...], g_ref[...]).astype(BF16)
    _write_q(xn_ref, w_ref, rope_ref, q_ref)
    kvt = _proj(xn_ref, w_ref, K_OFF, 2 * KV_WIDTH).T
    kt = kvt[0:KV_WIDTH]
    sh = ROT_DIM // 2
    up = jnp.concatenate([kt[sh:], kt[:sh]], axis=0)
    dn = jnp.concatenate([kt[-sh:], kt[:-sh]], axis=0)
    kt_ref[...] = kt * ropet_ref[0] + up * ropet_ref[1] + dn * ropet_ref[2]
    vt_ref[...] = kvt[KV_WIDTH:]
    per = N_SLAB // 2
    for c in range(2):
        u = _glu(xn_ref, w_ref, c)
        for cc in range(per):
            u_ref[c * per + cc] = u[:, cc * LANES:(cc + 1) * LANES]


def _inproj_sample(x, g, w_in, rope_tab, ropet_tab, meta, rope_meta, tm):
    m = x.shape[0]
    n_meta = meta.shape[0]
    return pl.pallas_call(
        _inproj_sample_kernel,
        grid=(m // tm,),
        in_specs=[
            pl.BlockSpec((tm, D_MODEL), lambda i: (i, 0)),
            _const_spec((1, D_MODEL)),
            _const_spec((D_MODEL, IN_COLS)),
            _const_spec((3, tm, LANES)),
            _const_spec((3, KV_WIDTH, tm)),
            _const_spec((n_meta, D_MODEL)),
            _const_spec((3, n_meta, LANES)),
        ],
        out_specs=[
            pl.BlockSpec((tm, ATTN_WIDTH), lambda i: (i, 0)),
            pl.BlockSpec((KV_WIDTH, tm), lambda i: (0, i)),
            pl.BlockSpec((KV_WIDTH, tm), lambda i: (0, i)),
            pl.BlockSpec((N_SLAB, tm, LANES), lambda i: (0, i, 0)),
            pl.BlockSpec((n_meta, KV_WIDTH), lambda i: (0, 0)),
            pl.BlockSpec((n_meta, KV_WIDTH), lambda i: (0, 0)),
            pl.BlockSpec((N_SLAB, n_meta, LANES), lambda i: (0, 0, 0)),
        ],
        out_shape=[
            jax.ShapeDtypeStruct((m, ATTN_WIDTH), F32),
            jax.ShapeDtypeStruct((KV_WIDTH, m), F32),
            jax.ShapeDtypeStruct((KV_WIDTH, m), F32),
            jax.ShapeDtypeStruct((N_SLAB, m, LANES), F32),
            jax.ShapeDtypeStruct((n_meta, KV_WIDTH), F32),
            jax.ShapeDtypeStruct((n_meta, KV_WIDTH), F32),
            jax.ShapeDtypeStruct((N_SLAB, n_meta, LANES), F32),
        ],
        scratch_shapes=[pltpu.VMEM((tm, D_MODEL), BF16), pltpu.VMEM((n_meta, D_MODEL), BF16)],
        compiler_params=_params("arbitrary"),
        name="inproj_sample",
    )(x, g, w_in, rope_tab, ropet_tab, meta, rope_meta)


def _rope_table(pos):
    half = ROT_DIM // 2
    inv = jnp.power(jnp.float32(ROPE_THETA), -jnp.arange(half, dtype=F32) * 2.0 / ROT_DIM)
    ang = pos.astype(F32)[:, None] * inv[None, :]
    cos, sin = jnp.cos(ang), jnp.sin(ang)
    t = pos.shape[0]
    rest = HEAD_DIM - ROT_DIM
    one = jnp.ones((t, rest), F32)
    zero = jnp.zeros((t, rest), F32)
    zh = jnp.zeros((t, half), F32)
    c = jnp.concatenate([cos, cos, one], axis=1)
    s_lo = jnp.concatenate([-sin, zh, zero], axis=1)
    s_hi = jnp.concatenate([zh, sin, zero], axis=1)
    tab = jnp.stack([c, s_lo, s_hi])
    return jnp.tile(tab, (1, 1, LANES // HEAD_DIM))


ATTN_NB = 4


def _attn_prompt_kernel(sink_ref, q_ref, kc_ref, kp_ref, vc_ref, vp_ref, km_ref, vm_ref, o_ref,
                        kk_ref, vv_ref, s_ref, p_ref, inv_ref):
    n = pl.program_id(1)

    @pl.when(n == 0)
    def _():
        pad = jnp.zeros((BLOCK - N_META, KV_WIDTH), F32)
        kk_ref[0, 0:BLOCK - N_META] = pad
        vv_ref[0, 0:BLOCK - N_META] = pad
        kk_ref[0, BLOCK - N_META:BLOCK] = km_ref[...]
        vv_ref[0, BLOCK - N_META:BLOCK] = vm_ref[...]

    @pl.when(n > 0)
    def _():
        kk_ref[0, 0:BLOCK] = kp_ref[...]
        vv_ref[0, 0:BLOCK] = vp_ref[...]

    for sb in range(ATTN_NB):
        k_cur = kc_ref[sb * BLOCK:(sb + 1) * BLOCK]
        v_cur = vc_ref[sb * BLOCK:(sb + 1) * BLOCK]
        kk_ref[sb, BLOCK:2 * BLOCK] = k_cur
        vv_ref[sb, BLOCK:2 * BLOCK] = v_cur
        if sb + 1 < ATTN_NB:
            kk_ref[sb + 1, 0:BLOCK] = k_cur
            vv_ref[sb + 1, 0:BLOCK] = v_cur
    for sb in range(ATTN_NB):
        vv_ref[sb, 0:1, :] = jnp.zeros((1, KV_WIDTH), F32)

    lo = jnp.where(n == 0, BLOCK - N_META, 0)
    r = lax.broadcasted_iota(jnp.int32, (BLOCK, 2 * BLOCK), 0)
    j = lax.broadcasted_iota(jnp.int32, (BLOCK, 2 * BLOCK), 1)
    band = (j > r) & (j <= r + BLOCK)
    shape3 = (BLOCK // SUBLANES, SUBLANES, 2 * BLOCK)
    col0 = lax.broadcasted_iota(jnp.int32, (1, SUBLANES, 2 * BLOCK), 2) == 0
    low_half = lax.broadcasted_iota(jnp.int32, (BLOCK, LANES), 1) < HEAD_DIM
    low_half2 = lax.broadcasted_iota(jnp.int32, (2 * BLOCK, LANES), 1) < HEAD_DIM

    def dup_halves(ref, sb, h):
        xf = ref[sb, :, (h // 2) * LANES:(h // 2 + 1) * LANES]
        x_sw = pltpu.roll(xf, HEAD_DIM, 1)
        both = jnp.where(low_half2, xf, x_sw) if h % 2 == 0 else jnp.where(low_half2, x_sw, xf)
        return both.astype(BF16)

    rows_kv = GQA_GROUP * BLOCK
    rows_blk = N_HEADS * BLOCK
    for sb in range(ATTN_NB):
        mask3 = (band & (j >= lo) if sb == 0 else band).reshape(shape3)
        q_rows = slice(sb * BLOCK, (sb + 1) * BLOCK)
        for h in range(N_KV_HEADS):
            qa = q_ref[q_rows, (2 * h) * LANES:(2 * h + 1) * LANES]
            qb = q_ref[q_rows, (2 * h + 1) * LANES:(2 * h + 2) * LANES]
            zero = jnp.zeros_like(qa)
            lhs = jnp.concatenate([jnp.where(low_half, qa, zero), jnp.where(low_half, qb, zero),
                                   jnp.where(low_half, zero, qa), jnp.where(low_half, zero, qb)], axis=0)
            s_all = lax.dot_general(lhs, dup_halves(kk_ref, sb, h), NT_DIMS, preferred_element_type=F32)
            heads = (4 * h, 4 * h + 2, 4 * h + 1, 4 * h + 3)
            for g in range(GQA_GROUP):
                fill = jnp.where(col0, sink_ref[heads[g]], MASK_VALUE)
                s = jnp.where(mask3, s_all[g * BLOCK:(g + 1) * BLOCK].reshape(shape3), fill)
                row0 = sb * rows_blk + h * rows_kv + g * BLOCK
                s_ref[row0:row0 + BLOCK] = s.reshape(BLOCK, 2 * BLOCK)

    s = s_ref[...]
    mx = jnp.max(s, axis=-1, keepdims=True)
    e = jnp.exp2(s - mx)
    p_ref[...] = e.astype(BF16)
    inv_ref[...] = jnp.broadcast_to(1.0 / jnp.sum(e, axis=-1, keepdims=True), inv_ref.shape)

    for sb in range(ATTN_NB):
        q_rows = slice(sb * BLOCK, (sb + 1) * BLOCK)
        for h in range(N_KV_HEADS):
            base = sb * rows_blk + h * rows_kv
            o_all = jnp.dot(p_ref[base:base + rows_kv], dup_halves(vv_ref, sb, h), preferred_element_type=F32)
            o = [o_all[g * BLOCK:(g + 1) * BLOCK] * inv_ref[base + g * BLOCK:base + (g + 1) * BLOCK]
                 for g in range(GQA_GROUP)]
            o_ref[q_rows, (2 * h) * LANES:(2 * h + 1) * LANES] = jnp.where(low_half, o[0], o[2]).astype(o_ref.dtype)
            o_ref[q_rows, (2 * h + 1) * LANES:(2 * h + 2) * LANES] = (
                jnp.where(low_half, o[1], o[3]).astype(o_ref.dtype))


def _attn_prompt(sinks, q, k, v, k_meta, v_meta, n_batch, n_blk):
    assert n_blk % ATTN_NB == 0
    n_step = n_blk // ATTN_NB
    rows = ATTN_NB * BLOCK
    kv_cur = pl.BlockSpec((rows, KV_WIDTH), lambda b, n: (b * n_step + n, 0))
    kv_prev = pl.BlockSpec((BLOCK, KV_WIDTH), lambda b, n: (b * n_blk + jnp.maximum(n * ATTN_NB - 1, 0), 0))
    return pl.pallas_call(
        _attn_prompt_kernel,
        grid=(n_batch, n_step),
        in_specs=[
            pl.BlockSpec(memory_space=pltpu.SMEM),
            pl.BlockSpec((rows, ATTN_WIDTH), lambda b, n: (b * n_step + n, 0)),
            kv_cur, kv_prev, kv_cur, kv_prev,
            _const_spec((N_META, KV_WIDTH)),
            _const_spec((N_META, KV_WIDTH)),
        ],
        out_specs=pl.BlockSpec((rows, ATTN_WIDTH), lambda b, n: (b * n_step + n, 0)),
        out_shape=jax.ShapeDtypeStruct((n_batch * n_blk * BLOCK, ATTN_WIDTH), BF16),
        scratch_shapes=[
            pltpu.VMEM((ATTN_NB, 2 * BLOCK, KV_WIDTH), F32),
            pltpu.VMEM((ATTN_NB, 2 * BLOCK, KV_WIDTH), F32),
            pltpu.VMEM((ATTN_NB * N_HEADS * BLOCK, 2 * BLOCK), F32),
            pltpu.VMEM((ATTN_NB * N_HEADS * BLOCK, 2 * BLOCK), BF16),
            pltpu.VMEM((ATTN_NB * N_HEADS * BLOCK, LANES), F32),
        ],
        compiler_params=_params("parallel", "arbitrary"),
        name="attn_prompt",
    )(sinks, q, k, k, v, v, k_meta, v_meta)


def _attn_sample_kernel(t_new, bt, sink_ref, q_ref, kn_ref, vn_ref, ck_ref, cv_ref,
                        o_ref, ko_ref, vo_ref, s_ref, p_ref, inv_ref, osc_ref):
    nrow = N_HEADS * t_new
    w = ck_ref.shape[2]
    assert bt * t_new == LANES and w == LANES and nrow == LANES
    zero_blk = jnp.zeros((t_new, LANES), F32)

    r = lax.broadcasted_iota(jnp.int32, (nrow, 2 * w), 0) % t_new
    j = lax.broadcasted_iota(jnp.int32, (nrow, 2 * w), 1)
    cache_ok = (j > r) & (j < w)
    fill = jnp.where(j == 0, sink_ref[...][:, 0:1], MASK_VALUE)
    key0 = lax.broadcasted_iota(jnp.int32, (bt * nrow, w), 1) == 0
    low_half =lax.broadcasted_iota(jnp.int32, (t_new, LANES), 1) < HEAD_DIM
    tail = lax.broadcasted_iota(jnp.int32, (KV_WIDTH, LANES), 1) >= w - t_new
    kn = kn_ref[...]
    vn = vn_ref[...]
    kn_bf = kn.astype(BF16)
    vn_bf = vn.astype(BF16)

    for i in range(bt):
        new_shift = (w - t_new - i * t_new) % LANES
        ko_ref[i] = jnp.where(tail, pltpu.roll(kn, new_shift, 1), pltpu.roll(ck_ref[i], w - t_new, 1))
        vo_ref[i] = jnp.where(tail, pltpu.roll(vn, new_shift, 1), pltpu.roll(cv_ref[i], w - t_new, 1))

    for i in range(bt):
        qrows = []
        for t in range(N_HEADS):
            h = t // GQA_GROUP
            qv = q_ref[i * t_new:(i + 1) * t_new, (t // 2) * LANES:(t // 2 + 1) * LANES]
            if t % 2 != h % 2:
                qv = pltpu.roll(qv, HEAD_DIM, 1)
            keep = low_half if h % 2 == 0 else jnp.logical_not(low_half)
            cols = [zero_blk] * (KV_WIDTH // LANES)
            cols[h // 2] = jnp.where(keep, qv, 0.0)
            qrows.append(jnp.concatenate(cols, axis=1))
        qrow = jnp.concatenate(qrows, axis=0).astype(BF16)

        rhs_k = jnp.concatenate([ck_ref[i].astype(BF16), kn_bf], axis=1)
        s = jnp.dot(qrow, rhs_k, preferred_element_type=F32)
        own = w + i * t_new
        mask = cache_ok | ((j >= own) & (j <= own + r))
        s_ref[i * nrow:(i + 1) * nrow] = jnp.where(mask, s, fill)

    s = s_ref[...]
    mx = jnp.max(s, axis=-1, keepdims=True)
    e = jnp.exp2(s - mx)
    den = jnp.sum(e, axis=-1, keepdims=True)
    p_ref[:, 0:w] = jnp.where(key0, 0.0, e[:, 0:w]).astype(BF16)
    p_ref[:, w:2 * w] = e[:, w:2 * w].astype(BF16)
    inv_ref[...] = jnp.broadcast_to(1.0 / den, inv_ref.shape)

    for i in range(bt):
        rhs_v = jnp.concatenate([cv_ref[i].astype(BF16), vn_bf], axis=1)
        o_all = lax.dot_general(p_ref[i * nrow:(i + 1) * nrow], rhs_v, NT_DIMS, preferred_element_type=F32)
        for t2 in range(N_HEADS // 2):
            parts = []
            for side in range(2):
                t = 2 * t2 + side
                h = t // GQA_GROUP
                blk = o_all[t * t_new:(t + 1) * t_new, (h // 2) * LANES:(h // 2 + 1) * LANES]
                blk = blk * inv_ref[i * nrow + t * t_new:i * nrow + (t + 1) * t_new, :]
                if t % 2 != h % 2:
                    blk = pltpu.roll(blk, HEAD_DIM, 1)
                parts.append(blk)
            osc_ref[i * t_new:(i + 1) * t_new, t2 * LANES:(t2 + 1) * LANES] = (
                jnp.where(low_half, parts[0], parts[1]))

    o_ref[...] = osc_ref[...].astype(o_ref.dtype)


def _attn_sample(sink_rows, q, kt_new, vt_new, cache_kt, cache_vt, t_new, bt):
    nb, _, w = cache_kt.shape
    rows = bt * t_new
    cache_spec = pl.BlockSpec((bt, KV_WIDTH, w), lambda s: (s, 0, 0))
    new_spec = pl.BlockSpec((KV_WIDTH, rows), lambda s: (0, s))
    return pl.pallas_call(
        functools.partial(_attn_sample_kernel, t_new, bt),
        grid=(nb // bt,),
        in_specs=[
            _const_spec(sink_rows.shape),
            pl.BlockSpec((rows, ATTN_WIDTH), lambda s: (s, 0)),
            new_spec, new_spec, cache_spec, cache_spec,
        ],
        out_specs=[pl.BlockSpec((rows, ATTN_WIDTH), lambda s: (s, 0)), cache_spec, cache_spec],
        out_shape=[
            jax.ShapeDtypeStruct((nb * t_new, ATTN_WIDTH), BF16),
            jax.ShapeDtypeStruct(cache_kt.shape, F32),
            jax.ShapeDtypeStruct(cache_vt.shape, F32),
        ],
        scratch_shapes=[
            pltpu.VMEM((bt * N_HEADS * t_new, 2 * w), F32),
            pltpu.VMEM((bt * N_HEADS * t_new, 2 * w), BF16),
            pltpu.VMEM((bt * N_HEADS * t_new, LANES), F32),
            pltpu.VMEM((rows, ATTN_WIDTH), F32),
        ],
        compiler_params=_params("parallel"),
        name="attn_sample",
    )(sink_rows, q, kt_new, vt_new, cache_kt, cache_vt)


def _ln_swish(y, g, b):
    mu = jnp.mean(y, axis=-1, keepdims=True)
    d = y - mu
    var = jnp.mean(d * d, axis=-1, keepdims=True)
    yn = d * lax.rsqrt(var + LN_EPS) * g + b
    return yn * jax.nn.sigmoid(yn)


CONV_HALO = 32
CONV_SEG = 252
CONV_SEG_STEP = 126


def _conv_prompt_kernel(seq, u_ref, meta_ref, wb_ref, o_ref, win_ref):
    pad = CONV_HALO - N_META
    for c in range(N_SLAB):
        win_ref[c, 0:pad] = jnp.zeros((pad, LANES), F32)
        win_ref[c, pad:CONV_HALO] = meta_ref[c]
        win_ref[c, CONV_HALO:CONV_HALO + seq] = u_ref[c]
    first = CONV_HALO - (CONV_K - 1)

    rem_base = SUBLANES * CONV_SEG
    rem_seg = (seq - rem_base) // SUBLANES
    n_main = CONV_SEG // CONV_SEG_STEP

    def taps(c, wts, base, seg, n_g, i0):
        accs = [None] * n_g
        for m in range(n_g + CONV_K - 1):
            v = win_ref[c, pl.ds(base + first + i0 + m, SUBLANES, stride=seg), :]
            for g in range(max(0, m - CONV_K + 1), min(n_g, m + 1)):
                term = v * wts[m - g]
                accs[g] = term if accs[g] is None else accs[g] + term
        for g in range(n_g):
            o_ref[c, pl.ds(base + i0 + g, SUBLANES, stride=seg), :] = accs[g]

    for c in range(N_SLAB):
        wts = [wb_ref[tap, c] for tap in range(CONV_K)]

        def body(ci, carry, c=c, wts=wts):
            taps(c, wts, 0, CONV_SEG, CONV_SEG_STEP, ci * CONV_SEG_STEP)
            return carry

        lax.fori_loop(0, n_main, body, 0)
        taps(c, wts, rem_base, rem_seg, rem_seg, 0)


def _conv_prompt(u_slabs, u_meta, wb, n_batch, seq):
    assert CONV_SEG % CONV_SEG_STEP == 0 and seq > SUBLANES * CONV_SEG
    assert (seq - SUBLANES * CONV_SEG) % SUBLANES == 0
    blk = pl.BlockSpec((N_SLAB, seq, LANES), lambda b: (0, b, 0))
    return pl.pallas_call(
        functools.partial(_conv_prompt_kernel, seq),
        grid=(n_batch,),
        in_specs=[
            blk,
            _const_spec((N_SLAB, N_META, LANES)),
            _const_spec((CONV_K, N_SLAB, SUBLANES, LANES)),
        ],
        out_specs=blk,
        out_shape=jax.ShapeDtypeStruct((N_SLAB, n_batch * seq, LANES), F32),
        scratch_shapes=[pltpu.VMEM((N_SLAB, CONV_HALO + seq, LANES), F32)],
        compiler_params=_params("parallel"),
        name="conv_prompt",
    )(u_slabs, u_meta, wb)


def _conv_sample_kernel(t_new, st_ref, u_ref, wb_ref, o_ref, so_ref):
    hist = st_ref.shape[0]
    bb = st_ref.shape[1]
    so_ref[0:hist - t_new] = st_ref[t_new:hist]

    for c in range(N_SLAB):
        lanes = slice(c * LANES, (c + 1) * LANES)
        for t in range(t_new):
            so_ref[hist - t_new + t, :, lanes] = u_ref[c, pl.ds(t, bb, stride=t_new), :]

        def src(k, rc, c=c, lanes=lanes):
            if k < hist:
                return st_ref[k, pl.ds(pl.multiple_of(rc * SUBLANES, SUBLANES), SUBLANES), lanes]
            return u_ref[c, pl.ds(rc * SUBLANES * t_new + (k - hist), SUBLANES, stride=t_new), :]

        wts = [wb_ref[tap, c] for tap in range(CONV_K)]

        def per_rows(rc, carry, c=c, src=src, wts=wts):
            accs = [None] * t_new
            for k in range(hist + t_new):
                v = src(k, rc)
                for t in range(max(0, k - CONV_K + 1), min(t_new, k + 1)):
                    term = v * wts[k - t]
                    accs[t] = term if accs[t] is None else accs[t] + term
            for t in range(t_new):
                o_ref[c, pl.ds(rc * SUBLANES * t_new + t, SUBLANES, stride=t_new), :] = accs[t]
            return carry

        for rc in range(bb // SUBLANES):
            per_rows(rc, 0)


def _conv_sample(state_t, u_slabs, wb, t_new, bb):
    hist, nb, _ = state_t.shape
    rows = bb * t_new
    slab_spec = pl.BlockSpec((N_SLAB, rows, LANES), lambda s: (0, s, 0))
    return pl.pallas_call(
        functools.partial(_conv_sample_kernel, t_new),
        grid=(nb // bb,),
        in_specs=[
            pl.BlockSpec((hist, bb, CONV_CH), lambda s: (0, s, 0)),
            slab_spec,
            _const_spec((CONV_K, N_SLAB, SUBLANES, LANES)),
        ],
        out_specs=[slab_spec, pl.BlockSpec((hist, bb, CONV_CH), lambda s: (0, s, 0))],
        out_shape=[
            jax.ShapeDtypeStruct((N_SLAB, nb * t_new, LANES), F32),
            jax.ShapeDtypeStruct(state_t.shape, F32),
        ],
        compiler_params=_params("parallel"),
        name="conv_sample",
    )(state_t, u_slabs, wb)


def _outproj_kernel(x_ref, a_ref, c_ref, bdw_ref, g_ref, b_ref, wo_ref, h_ref, cbf_ref):
    h_ref[...] = x_ref[...] + jnp.dot(a_ref[...], wo_ref[0:ATTN_WIDTH].astype(BF16), preferred_element_type=F32)
    y = jnp.concatenate([c_ref[c] for c in range(N_SLAB)], axis=1)
    cbf_ref[...] = _ln_swish(y + bdw_ref[...], g_ref[...], b_ref[...]).astype(BF16)
    h_ref[...] += jnp.dot(cbf_ref[...], wo_ref[ATTN_WIDTH:].astype(BF16), preferred_element_type=F32)


def _outproj(x, attn, conv_taps, b_dw, ln_g, ln_b, w_out, tm):
    m = x.shape[0]
    return pl.pallas_call(
        _outproj_kernel,
        grid=(m // tm,),
        in_specs=[
            pl.BlockSpec((tm, D_MODEL), lambda i: (i, 0)),
            pl.BlockSpec((tm, ATTN_WIDTH), lambda i: (i, 0)),
            pl.BlockSpec((N_SLAB, tm, LANES), lambda i: (0, i, 0)),
            _const_spec((1, CONV_CH)),
            _const_spec((1, CONV_CH)),
            _const_spec((1, CONV_CH)),
            _const_spec((D_MODEL, D_MODEL)),
        ],
        out_specs=pl.BlockSpec((tm, D_MODEL), lambda i: (i, 0)),
        out_shape=jax.ShapeDtypeStruct((m, D_MODEL), F32),
        scratch_shapes=[pltpu.VMEM((tm, CONV_CH), BF16)],
        compiler_params=_params("parallel"),
        name="outproj",
    )(x, attn, conv_taps, b_dw, ln_g, ln_b, w_out)


def _mlp_kernel(h_ref, g2_ref, wu_ref, wd_ref, gf_ref, y_ref, xn_ref):
    f = pl.program_id(1)

    @pl.when(f == 0)
    def _():
        h1 = h_ref[...]
        y_ref[...] = h1
        xn_ref[...] = _rms(h1, g2_ref[...]).astype(BF16)

    up = jnp.dot(xn_ref[...], wu_ref[...].astype(BF16), preferred_element_type=F32)
    act = jnp.maximum(up, 0.0)
    act = (act * act).astype(BF16)
    y_ref[...] += jnp.dot(act, wd_ref[...].astype(BF16), preferred_element_type=F32)

    @pl.when(f == pl.num_programs(1) - 1)
    def _():
        y_ref[...] = _rms(y_ref[...], gf_ref[...])


def _mlp(h1, g2, wu_bf, wd_bf, gf, tm, tf):
    m = h1.shape[0]
    return pl.pallas_call(
        _mlp_kernel,
        grid=(m // tm, D_FF // tf),
        in_specs=[
            pl.BlockSpec((tm, D_MODEL), lambda i, f: (i, 0)),
            _const_spec((1, D_MODEL)),
            pl.BlockSpec((D_MODEL, tf), lambda i, f: (0, f)),
            pl.BlockSpec((tf, D_MODEL), lambda i, f: (f, 0)),
            _const_spec((1, D_MODEL)),
        ],
        out_specs=pl.BlockSpec((tm, D_MODEL), lambda i, f: (i, 0)),
        out_shape=jax.ShapeDtypeStruct((m, D_MODEL), F32),
        scratch_shapes=[pltpu.VMEM((tm, D_MODEL), BF16)],
        compiler_params=pltpu.CompilerParams(dimension_semantics=("parallel", "arbitrary"),
                                             vmem_limit_bytes=VMEM_LIMIT_MLP),
        name="mlp",
    )(h1, g2, wu_bf, wd_bf, gf)


def kernel(x_prompt, x_sample, cache_k, cache_v, state_conv, meta_tokens, norm_mix, w_in, attn_sinks,
           w_dw, b_dw, conv_ln_g, conv_ln_b, w_out, norm_mlp, w_up, w_down, norm_final):
    n_batch, seq, _ = x_prompt.shape
    n_dec, t_new, _ = x_sample.shape
    depth = w_in.shape[0]
    window = cache_k.shape[2]
    assert depth == 1 and seq % BLOCK == 0 and t_new == SUBLANES and window == WINDOW
    l = 0
    n_blk = seq // BLOCK
    tm = 512
    tm_mlp = 1024
    tf = 512
    bt = LANES // t_new
    conv_bb = 32
    assert (n_batch * seq) % tm_mlp == 0 and (n_dec * t_new) % tm_mlp == 0 and n_dec % conv_bb == 0

    g_mix = norm_mix[l][None, :]
    g_mlp = norm_mlp[l][None, :]
    g_fin = norm_final[None, :]
    sinks = attn_sinks[l].astype(F32) * LOG2_E
    bdw = b_dw[l][None, :]
    ln_g = conv_ln_g[l][None, :]
    ln_b = conv_ln_b[l][None, :]
    wb_slabs = jnp.broadcast_to(w_dw[l].reshape(CONV_K, N_SLAB, 1, LANES), (CONV_K, N_SLAB, SUBLANES, LANES))

    rope_p = _rope_table(N_META + jnp.arange(seq, dtype=jnp.int32))
    rope_s = _rope_table(PAST_LEN + (jnp.arange(tm, dtype=jnp.int32) % t_new))
    ropet_s = jnp.tile(rope_s[:, :, :HEAD_DIM].transpose(0, 2, 1), (1, N_KV_HEADS, 1))
    rope_m = _rope_table(jnp.arange(N_META, dtype=jnp.int32))

    xp = x_prompt.reshape(n_batch * seq, D_MODEL)
    xs = x_sample.reshape(n_dec * t_new, D_MODEL)

    q_p, k_p, v_p, u_p = _inproj_prompt(xp, g_mix, w_in[l], rope_p, tm)
    q_s, kt_s, vt_s, u_s, k_m, v_m, u_m = _inproj_sample(xs, g_mix, w_in[l], rope_s, ropet_s,
                                                         meta_tokens.astype(F32), rope_m, tm)

    attn_p = _attn_prompt(sinks, q_p, k_p, v_p, k_m, v_m, n_batch, n_blk)
    conv_p = _conv_prompt(u_p, u_m, wb_slabs, n_batch, seq)

    ckt = cache_k[l].transpose(0, 2, 3, 1).reshape(n_dec, KV_WIDTH, window)
    cvt = cache_v[l].transpose(0, 2, 3, 1).reshape(n_dec, KV_WIDTH, window)
    sink_rows = jnp.broadcast_to(jnp.repeat(sinks, t_new)[:, None], (N_HEADS * t_new, LANES))
    attn_s, kt_out, vt_out = _attn_sample(sink_rows, q_s, kt_s, vt_s, ckt, cvt, t_new, bt)
    state_t = state_conv[l].transpose(1, 0, 2)
    conv_s, state_out_t = _conv_sample(state_t, u_s, wb_slabs, t_new, conv_bb)

    h_p = _outproj(xp, attn_p, conv_p, bdw, ln_g, ln_b, w_out[l], tm)
    h_s = _outproj(xs, attn_s, conv_s, bdw, ln_g, ln_b, w_out[l], tm)
    y_p = _mlp(h_p, g_mlp, w_up[l], w_down[l], g_fin, tm_mlp, tf)
    y_s = _mlp(h_s, g_mlp, w_up[l], w_down[l], g_fin, tm_mlp, tf)

    y_prompt = y_p.reshape(n_batch, seq, D_MODEL)
    y_sample = y_s.reshape(n_dec, t_new, D_MODEL)
    k_last = k_p.reshape(n_batch, seq, KV_WIDTH)[:, seq - WINDOW:]
    v_last = v_p.reshape(n_batch, seq, KV_WIDTH)[:, seq - WINDOW:]
    new_k_prompt = k_last.reshape(n_batch, WINDOW, N_KV_HEADS, HEAD_DIM)[None]
    new_v_prompt = v_last.reshape(n_batch, WINDOW, N_KV_HEADS, HEAD_DIM)[None]
    u_last = u_p.reshape(N_SLAB, n_batch, seq, LANES)[:, :, seq - (CONV_K - 1):]
    new_conv_prompt = u_last.transpose(1, 2, 0, 3).reshape(n_batch, CONV_K - 1, CONV_CH)[None]
    new_k_sample = kt_out.reshape(n_dec, N_KV_HEADS, HEAD_DIM, window).transpose(0, 3, 1, 2)[None]
    new_v_sample = vt_out.reshape(n_dec, N_KV_HEADS, HEAD_DIM, window).transpose(0, 3, 1, 2)[None]
    new_conv_sample = state_out_t.transpose(1, 0, 2)[None]
    return (y_prompt, y_sample, new_k_prompt, new_v_prompt, new_conv_prompt,
            new_k_sample, new_v_sample, new_conv_sample)
```

```python
import functools

import jax
import jax.numpy as jnp
from jax import lax
from jax.experimental import pallas as pl
from jax.experimental.pallas import tpu as pltpu

D_MODEL = 2048
N_META = 16
HEAD_DIM = 64
ATTN_WIDTH = D_MODEL // 2
CONV_CH = D_MODEL - ATTN_WIDTH
N_HEADS = ATTN_WIDTH // HEAD_DIM
N_KV_HEADS = 4
GQA_GROUP = N_HEADS // N_KV_HEADS
KV_WIDTH = N_KV_HEADS * HEAD_DIM
IN_COLS = ATTN_WIDTH + 2 * KV_WIDTH + 2 * CONV_CH
WINDOW = 128
BLOCK = 128
ROPE_THETA = 500000.0
ROT_DIM = HEAD_DIM // 4
CONV_K = 31
D_FF = 4 * D_MODEL
RMS_EPS = 1e-6
LN_EPS = 1e-5
MASK_VALUE = -1e30
PAST_LEN = 8192

LANES = 128
SUBLANES = 8
VMEM_LIMIT = 56 * 1024 * 1024
VMEM_LIMIT_MLP = 62 * 1024 * 1024
N_SLAB = CONV_CH // LANES

F32 = jnp.float32
BF16 = jnp.bfloat16

K_OFF = ATTN_WIDTH
A_OFF = ATTN_WIDTH + 2 * KV_WIDTH
B_OFF = A_OFF + CONV_CH
NT_DIMS = (((1,), (1,)), ((), ()))
LOG2_E = 1.4426950408889634
Q_SCALE = HEAD_DIM ** -0.5 * LOG2_E


def _params(*sem):
    return pltpu.CompilerParams(dimension_semantics=sem, vmem_limit_bytes=VMEM_LIMIT)


def _const_spec(shape):
    nd = len(shape)
    return pl.BlockSpec(shape, lambda *_: (0,) * nd, pipeline_mode=pl.Buffered(1))


def _rms(x, g):
    ms = jnp.mean(x * x, axis=-1, keepdims=True)
    return (x * lax.rsqrt(ms + RMS_EPS)) * g


def _rope_rows(z, rope_ref):
    return (z * rope_ref[0] + pltpu.roll(z, LANES - ROT_DIM // 2, 1) * rope_ref[1]
            + pltpu.roll(z, ROT_DIM // 2, 1) * rope_ref[2])


def _proj(xn_ref, w_ref, off, width):
    return jnp.dot(xn_ref[...], w_ref[:, off:off + width].astype(BF16), preferred_element_type=F32)


def _write_q(xn_ref, w_ref, rope_ref, q_ref):
    half = ATTN_WIDTH // 2
    for c in range(2):
        acc = _proj(xn_ref, w_ref, c * half, half)
        for cc in range(half // LANES):
            z = _rope_rows(acc[:, cc * LANES:(cc + 1) * LANES], rope_ref) * Q_SCALE
            q_ref[:, c * half + cc * LANES:c * half + (cc + 1) * LANES] = z.astype(q_ref.dtype)


GLU_CHUNK = 256


def _glu(xn_ref, w_ref, c):
    a = _proj(xn_ref, w_ref, A_OFF + c * GLU_CHUNK, GLU_CHUNK)
    b = _proj(xn_ref, w_ref, B_OFF + c * GLU_CHUNK, GLU_CHUNK)
    return a * jax.nn.sigmoid(b)


def _inproj_prompt_kernel(x_ref, g_ref, w_ref, rope_ref, q_ref, k_ref, v_ref, u_ref, xn_ref):
    xn_ref[...] = _rms(x_ref[...], g_ref[...]).astype(BF16)
    _write_q(xn_ref, w_ref, rope_ref, q_ref)
    _write_kvu(xn_ref, w_ref, rope_ref, k_ref, v_ref, u_ref)


def _write_kvu(xn_ref, w_ref, rope_ref, k_ref, v_ref, u_ref):
    acc = _proj(xn_ref, w_ref, K_OFF, 2 * KV_WIDTH)
    for cc in range(KV_WIDTH // LANES):
        k_ref[:, cc * LANES:(cc + 1) * LANES] = _rope_rows(acc[:, cc * LANES:(cc + 1) * LANES], rope_ref)
    v_ref[...] = acc[:, KV_WIDTH:]
    per = GLU_CHUNK // LANES
    for c in range(CONV_CH // GLU_CHUNK):
        u = _glu(xn_ref, w_ref, c)
        for cc in range(per):
            u_ref[c * per + cc] = u[:, cc * LANES:(cc + 1) * LANES]


def _inproj_prompt(x, g, w_in, rope_tab, tm):
    m = x.shape[0]
    nrt = rope_tab.shape[1] // tm
    return pl.pallas_call(
        _inproj_prompt_kernel,
        grid=(m // tm,),
        in_specs=[
            pl.BlockSpec((tm, D_MODEL), lambda i: (i, 0)),
            _const_spec((1, D_MODEL)),
            _const_spec((D_MODEL, IN_COLS)),
            pl.BlockSpec((3, tm, LANES), lambda i: (0, i % nrt, 0)),
        ],
        out_specs=[
            pl.BlockSpec((tm, ATTN_WIDTH), lambda i: (i, 0)),
            pl.BlockSpec((tm, KV_WIDTH), lambda i: (i, 0)),
            pl.BlockSpec((tm, KV_WIDTH), lambda i: (i, 0)),
            pl.BlockSpec((N_SLAB, tm, LANES), lambda i: (0, i, 0)),
        ],
        out_shape=[
            jax.ShapeDtypeStruct((m, ATTN_WIDTH), BF16),
            jax.ShapeDtypeStruct((m, KV_WIDTH), F32),
            jax.ShapeDtypeStruct((m, KV_WIDTH), F32),
            jax.ShapeDtypeStruct((N_SLAB, m, LANES), F32),
        ],
        scratch_shapes=[pltpu.VMEM((tm, D_MODEL), BF16)],
        compiler_params=_params("parallel"),
        name="inproj_prompt",
    )(x, g, w_in, rope_tab)


def _inproj_sample_kernel(x_ref, g_ref, w_ref, rope_ref, ropet_ref, meta_ref, ropem_ref,
                          q_ref, kt_ref, vt_ref, u_ref, km_ref, vm_ref, um_ref, xn_ref, xm_ref):
    @pl.when(pl.program_id(0) == 0)
    def _():
        xm_ref[...] = _rms(meta_ref[...], g_ref[...]).astype(BF16)
        _write_kvu(xm_ref, w_ref, ropem_ref, km_ref, vm_ref, um_ref)

    xn_ref[...] = _rms(x_ref[...], g_ref[...]).astype(BF16)
    _write_q(xn_ref, w_ref, rope_ref, q_ref)
    kvt = _proj(xn_ref, w_ref, K_OFF, 2 * KV_WIDTH).T
    kt = kvt[0:KV_WIDTH]
    sh = ROT_DIM // 2
    up = jnp.concatenate([kt[sh:], kt[:sh]], axis=0)
    dn = jnp.concatenate([kt[-sh:], kt[:-sh]], axis=0)
    kt_ref[...] = kt * ropet_ref[0] + up * ropet_ref[1] + dn * ropet_ref[2]
    vt_ref[...] = kvt[KV_WIDTH:]
    per = GLU_CHUNK // LANES
    for c in range(CONV_CH // GLU_CHUNK):
        u = _glu(xn_ref, w_ref, c)
        for cc in range(per):
            u_ref[c * per + cc] = u[:, cc * LANES:(cc + 1) * LANES]


def _inproj_sample(x, g, w_in, rope_tab, ropet_tab, meta, rope_meta, tm):
    m = x.shape[0]
    n_meta = meta.shape[0]
    return pl.pallas_call(
        _inproj_sample_kernel,
        grid=(m // tm,),
        in_specs=[
            pl.BlockSpec((tm, D_MODEL), lambda i: (i, 0)),
            _const_spec((1, D_MODEL)),
            _const_spec((D_MODEL, IN_COLS)),
            _const_spec((3, tm, LANES)),
            _const_spec((3, KV_WIDTH, tm)),
            _const_spec((n_meta, D_MODEL)),
            _const_spec((3, n_meta, LANES)),
        ],
        out_specs=[
            pl.BlockSpec((tm, ATTN_WIDTH), lambda i: (i, 0)),
            pl.BlockSpec((KV_WIDTH, tm), lambda i: (0, i)),
            pl.BlockSpec((KV_WIDTH, tm), lambda i: (0, i)),
            pl.BlockSpec((N_SLAB, tm, LANES), lambda i: (0, i, 0)),
            pl.BlockSpec((n_meta, KV_WIDTH), lambda i: (0, 0)),
            pl.BlockSpec((n_meta, KV_WIDTH), lambda i: (0, 0)),
            pl.BlockSpec((N_SLAB, n_meta, LANES), lambda i: (0, 0, 0)),
        ],
        out_shape=[
            jax.ShapeDtypeStruct((m, ATTN_WIDTH), F32),
            jax.ShapeDtypeStruct((KV_WIDTH, m), F32),
            jax.ShapeDtypeStruct((KV_WIDTH, m), F32),
            jax.ShapeDtypeStruct((N_SLAB, m, LANES), F32),
            jax.ShapeDtypeStruct((n_meta, KV_WIDTH), F32),
            jax.ShapeDtypeStruct((n_meta, KV_WIDTH), F32),
            jax.ShapeDtypeStruct((N_SLAB, n_meta, LANES), F32),
        ],
        scratch_shapes=[pltpu.VMEM((tm, D_MODEL), BF16), pltpu.VMEM((n_meta, D_MODEL), BF16)],
        compiler_params=_params("arbitrary"),
        name="inproj_sample",
    )(x, g, w_in, rope_tab, ropet_tab, meta, rope_meta)


def _rope_table(pos):
    half = ROT_DIM // 2
    inv = jnp.power(jnp.float32(ROPE_THETA), -jnp.arange(half, dtype=F32) * 2.0 / ROT_DIM)
    ang = pos.astype(F32)[:, None] * inv[None, :]
    cos, sin = jnp.cos(ang), jnp.sin(ang)
    t = pos.shape[0]
    rest = HEAD_DIM - ROT_DIM
    one = jnp.ones((t, rest), F32)
    zero = jnp.zeros((t, rest), F32)
    zh = jnp.zeros((t, half), F32)
    c = jnp.concatenate([cos, cos, one], axis=1)
    s_lo = jnp.concatenate([-sin, zh, zero], axis=1)
    s_hi = jnp.concatenate([zh, sin, zero], axis=1)
    tab = jnp.stack([c, s_lo, s_hi])
    return jnp.tile(tab, (1, 1, LANES // HEAD_DIM))


ATTN_NB = 4


def _attn_prompt_kernel(sink_ref, q_ref, kc_ref, kp_ref, vc_ref, vp_ref, km_ref, vm_ref, o_ref,
                        kk_ref, vv_ref, s_ref, p_ref, inv_ref):
    n = pl.program_id(1)

    @pl.when(n == 0)
    def _():
        pad = jnp.zeros((BLOCK - N_META, KV_WIDTH), F32)
        kk_ref[0, 0:BLOCK - N_META] = pad
        vv_ref[0, 0:BLOCK - N_META] = pad
        kk_ref[0, BLOCK - N_META:BLOCK] = km_ref[...]
        vv_ref[0, BLOCK - N_META:BLOCK] = vm_ref[...]

    @pl.when(n > 0)
    def _():
        kk_ref[0, 0:BLOCK] = kp_ref[...]
        vv_ref[0, 0:BLOCK] = vp_ref[...]

    for sb in range(ATTN_NB):
        k_cur = kc_ref[sb * BLOCK:(sb + 1) * BLOCK]
        v_cur = vc_ref[sb * BLOCK:(sb + 1) * BLOCK]
        kk_ref[sb, BLOCK:2 * BLOCK] = k_cur
        vv_ref[sb, BLOCK:2 * BLOCK] = v_cur
        if sb + 1 < ATTN_NB:
            kk_ref[sb + 1, 0:BLOCK] = k_cur
            vv_ref[sb + 1, 0:BLOCK] = v_cur
    for sb in range(ATTN_NB):
        vv_ref[sb, 0:1, :] = jnp.zeros((1, KV_WIDTH), F32)

    lo = jnp.where(n == 0, BLOCK - N_META, 0)
    r = lax.broadcasted_iota(jnp.int32, (BLOCK, 2 * BLOCK), 0)
    j = lax.broadcasted_iota(jnp.int32, (BLOCK, 2 * BLOCK), 1)
    band = (j > r) & (j <= r + BLOCK)
    shape3 = (BLOCK // SUBLANES, SUBLANES, 2 * BLOCK)
    col0 = lax.broadcasted_iota(jnp.int32, (1, SUBLANES, 2 * BLOCK), 2) == 0
    low_half = lax.broadcasted_iota(jnp.int32, (BLOCK, LANES), 1) < HEAD_DIM
    low_half2 = lax.broadcasted_iota(jnp.int32, (2 * BLOCK, LANES), 1) < HEAD_DIM

    def dup_halves(ref, sb, h):
        xf = ref[sb, :, (h // 2) * LANES:(h // 2 + 1) * LANES]
        x_sw = pltpu.roll(xf, HEAD_DIM, 1)
        both = jnp.where(low_half2, xf, x_sw) if h % 2 == 0 else jnp.where(low_half2, x_sw, xf)
        return both.astype(BF16)

    rows_kv = GQA_GROUP * BLOCK
    rows_blk = N_HEADS * BLOCK
    for sb in range(ATTN_NB):
        mask3 = (band & (j >= lo) if sb == 0 else band).reshape(shape3)
        q_rows = slice(sb * BLOCK, (sb + 1) * BLOCK)
        for h in range(N_KV_HEADS):
            qa = q_ref[q_rows, (2 * h) * LANES:(2 * h + 1) * LANES]
            qb = q_ref[q_rows, (2 * h + 1) * LANES:(2 * h + 2) * LANES]
            zero = jnp.zeros_like(qa)
            lhs = jnp.concatenate([jnp.where(low_half, qa, zero), jnp.where(low_half, qb, zero),
                                   jnp.where(low_half, zero, qa), jnp.where(low_half, zero, qb)], axis=0)
            s_all = lax.dot_general(lhs, dup_halves(kk_ref, sb, h), NT_DIMS, preferred_element_type=F32)
            heads = (4 * h, 4 * h + 2, 4 * h + 1, 4 * h + 3)
            for g in range(GQA_GROUP):
                fill = jnp.where(col0, sink_ref[heads[g]], MASK_VALUE)
                s = jnp.where(mask3, s_all[g * BLOCK:(g + 1) * BLOCK].reshape(shape3), fill)
                row0 = sb * rows_blk + h * rows_kv + g * BLOCK
                s_ref[row0:row0 + BLOCK] = s.reshape(BLOCK, 2 * BLOCK)

    s = s_ref[...]
    mx = jnp.max(s, axis=-1, keepdims=True)
    e = jnp.exp2(s - mx)
    p_ref[...] = e.astype(BF16)
    inv_ref[...] = jnp.broadcast_to(1.0 / jnp.sum(e, axis=-1, keepdims=True), inv_ref.shape)

    for sb in range(ATTN_NB):
        q_rows = slice(sb * BLOCK, (sb + 1) * BLOCK)
        for h in range(N_KV_HEADS):
            base = sb * rows_blk + h * rows_kv
            o_all = jnp.dot(p_ref[base:base + rows_kv], dup_halves(vv_ref, sb, h), preferred_element_type=F32)
            o = [o_all[g * BLOCK:(g + 1) * BLOCK] * inv_ref[base + g * BLOCK:base + (g + 1) * BLOCK]
                 for g in range(GQA_GROUP)]
            o_ref[q_rows, (2 * h) * LANES:(2 * h + 1) * LANES] = jnp.where(low_half, o[0], o[2]).astype(o_ref.dtype)
            o_ref[q_rows, (2 * h + 1) * LANES:(2 * h + 2) * LANES] = (
                jnp.where(low_half, o[1], o[3]).astype(o_ref.dtype))


def _attn_prompt(sinks, q, k, v, k_meta, v_meta, n_batch, n_blk):
    assert n_blk % ATTN_NB == 0
    n_step = n_blk // ATTN_NB
    rows = ATTN_NB * BLOCK
    kv_cur = pl.BlockSpec((rows, KV_WIDTH), lambda b, n: (b * n_step + n, 0))
    kv_prev = pl.BlockSpec((BLOCK, KV_WIDTH), lambda b, n: (b * n_blk + jnp.maximum(n * ATTN_NB - 1, 0), 0))
    return pl.pallas_call(
        _attn_prompt_kernel,
        grid=(n_batch, n_step),
        in_specs=[
            pl.BlockSpec(memory_space=pltpu.SMEM),
            pl.BlockSpec((rows, ATTN_WIDTH), lambda b, n: (b * n_step + n, 0)),
            kv_cur, kv_prev, kv_cur, kv_prev,
            _const_spec((N_META, KV_WIDTH)),
            _const_spec((N_META, KV_WIDTH)),
        ],
        out_specs=pl.BlockSpec((rows, ATTN_WIDTH), lambda b, n: (b * n_step + n, 0)),
        out_shape=jax.ShapeDtypeStruct((n_batch * n_blk * BLOCK, ATTN_WIDTH), BF16),
        scratch_shapes=[
            pltpu.VMEM((ATTN_NB, 2 * BLOCK, KV_WIDTH), F32),
            pltpu.VMEM((ATTN_NB, 2 * BLOCK, KV_WIDTH), F32),
            pltpu.VMEM((ATTN_NB * N_HEADS * BLOCK, 2 * BLOCK), F32),
            pltpu.VMEM((ATTN_NB * N_HEADS * BLOCK, 2 * BLOCK), BF16),
            pltpu.VMEM((ATTN_NB * N_HEADS * BLOCK, LANES), F32),
        ],
        compiler_params=_params("parallel", "arbitrary"),
        name="attn_prompt",
    )(sinks, q, k, k, v, v, k_meta, v_meta)


def _attn_sample_kernel(t_new, bt, sink_ref, q_ref, kn_ref, vn_ref, ck_ref, cv_ref,
                        o_ref, ko_ref, vo_ref, s_ref, p_ref, inv_ref, osc_ref):
    nrow = N_HEADS * t_new
    w = ck_ref.shape[2]
    assert bt * t_new == LANES and w == LANES and nrow == LANES
    zero_blk = jnp.zeros((t_new, LANES), F32)

    r = lax.broadcasted_iota(jnp.int32, (nrow, 2 * w), 0) % t_new
    j = lax.broadcasted_iota(jnp.int32, (nrow, 2 * w), 1)
    cache_ok = (j > r) & (j < w)
    fill = jnp.where(j == 0, sink_ref[...][:, 0:1], MASK_VALUE)
    key0 = lax.broadcasted_iota(jnp.int32, (bt * nrow, w), 1) == 0
    low_half =lax.broadcasted_iota(jnp.int32, (t_new, LANES), 1) < HEAD_DIM
    tail = lax.broadcasted_iota(jnp.int32, (KV_WIDTH, LANES), 1) >= w - t_new
    kn = kn_ref[...]
    vn = vn_ref[...]
    kn_bf = kn.astype(BF16)
    vn_bf = vn.astype(BF16)

    for i in range(bt):
        new_shift = (w - t_new - i * t_new) % LANES
        ko_ref[i] = jnp.where(tail, pltpu.roll(kn, new_shift, 1), pltpu.roll(ck_ref[i], w - t_new, 1))
        vo_ref[i] = jnp.where(tail, pltpu.roll(vn, new_shift, 1), pltpu.roll(cv_ref[i], w - t_new, 1))

    for i in range(bt):
        qrows = []
        for t in range(N_HEADS):
            h = t // GQA_GROUP
            qv = q_ref[i * t_new:(i + 1) * t_new, (t // 2) * LANES:(t // 2 + 1) * LANES]
            if t % 2 != h % 2:
                qv = pltpu.roll(qv, HEAD_DIM, 1)
            keep = low_half if h % 2 == 0 else jnp.logical_not(low_half)
            cols = [zero_blk] * (KV_WIDTH // LANES)
            cols[h // 2] = jnp.where(keep, qv, 0.0)
            qrows.append(jnp.concatenate(cols, axis=1))
        qrow = jnp.concatenate(qrows, axis=0).astype(BF16)

        rhs_k = jnp.concatenate([ck_ref[i].astype(BF16), kn_bf], axis=1)
        s = jnp.dot(qrow, rhs_k, preferred_element_type=F32)
        own = w + i * t_new
        mask = cache_ok | ((j >= own) & (j <= own + r))
        s_ref[i * nrow:(i + 1) * nrow] = jnp.where(mask, s, fill)

    s = s_ref[...]
    mx = jnp.max(s, axis=-1, keepdims=True)
    e = jnp.exp2(s - mx)
    den = jnp.sum(e, axis=-1, keepdims=True)
    p_ref[:, 0:w] = jnp.where(key0, 0.0, e[:, 0:w]).astype(BF16)
    p_ref[:, w:2 * w] = e[:, w:2 * w].astype(BF16)
    inv_ref[...] = jnp.broadcast_to(1.0 / den, inv_ref.shape)

    for i in range(bt):
        rhs_v = jnp.concatenate([cv_ref[i].astype(BF16), vn_bf], axis=1)
        o_all = lax.dot_general(p_ref[i * nrow:(i + 1) * nrow], rhs_v, NT_DIMS, preferred_element_type=F32)
        for t2 in range(N_HEADS // 2):
            parts = []
            for side in range(2):
                t = 2 * t2 + side
                h = t // GQA_GROUP
                blk = o_all[t * t_new:(t + 1) * t_new, (h // 2) * LANES:(h // 2 + 1) * LANES]
                blk = blk * inv_ref[i * nrow + t * t_new:i * nrow + (t + 1) * t_new, :]
                if t % 2 != h % 2:
                    blk = pltpu.roll(blk, HEAD_DIM, 1)
                parts.append(blk)
            osc_ref[i * t_new:(i + 1) * t_new, t2 * LANES:(t2 + 1) * LANES] = (
                jnp.where(low_half, parts[0], parts[1]))

    o_ref[...] = osc_ref[...].astype(o_ref.dtype)


def _attn_sample(sink_rows, q, kt_new, vt_new, cache_kt, cache_vt, t_new, bt):
    nb, _, w = cache_kt.shape
    rows = bt * t_new
    cache_spec = pl.BlockSpec((bt, KV_WIDTH, w), lambda s: (s, 0, 0))
    new_spec = pl.BlockSpec((KV_WIDTH, rows), lambda s: (0, s))
    return pl.pallas_call(
        functools.partial(_attn_sample_kernel, t_new, bt),
        grid=(nb // bt,),
        in_specs=[
            _const_spec(sink_rows.shape),
            pl.BlockSpec((rows, ATTN_WIDTH), lambda s: (s, 0)),
            new_spec, new_spec, cache_spec, cache_spec,
        ],
        out_specs=[pl.BlockSpec((rows, ATTN_WIDTH), lambda s: (s, 0)), cache_spec, cache_spec],
        out_shape=[
            jax.ShapeDtypeStruct((nb * t_new, ATTN_WIDTH), BF16),
            jax.ShapeDtypeStruct(cache_kt.shape, F32),
            jax.ShapeDtypeStruct(cache_vt.shape, F32),
        ],
        scratch_shapes=[
            pltpu.VMEM((bt * N_HEADS * t_new, 2 * w), F32),
            pltpu.VMEM((bt * N_HEADS * t_new, 2 * w), BF16),
            pltpu.VMEM((bt * N_HEADS * t_new, LANES), F32),
            pltpu.VMEM((rows, ATTN_WIDTH), F32),
        ],
        compiler_params=_params("parallel"),
        name="attn_sample",
    )(sink_rows, q, kt_new, vt_new, cache_kt, cache_vt)


def _ln_swish(y, g, b):
    mu = jnp.mean(y, axis=-1, keepdims=True)
    d = y - mu
    var = jnp.mean(d * d, axis=-1, keepdims=True)
    yn = d * lax.rsqrt(var + LN_EPS) * g + b
    return yn * jax.nn.sigmoid(yn)


CONV_HALO = 32
CONV_SEG = 252
CONV_SEG_STEP = 126


def _conv_prompt_kernel(seq, u_ref, meta_ref, wb_ref, o_ref, win_ref):
    pad = CONV_HALO - N_META
    for c in range(N_SLAB):
        win_ref[c, 0:pad] = jnp.zeros((pad, LANES), F32)
        win_ref[c, pad:CONV_HALO] = meta_ref[c]
        win_ref[c, CONV_HALO:CONV_HALO + seq] = u_ref[c]
    first = CONV_HALO - (CONV_K - 1)

    rem_base = SUBLANES * CONV_SEG
    rem_seg = (seq - rem_base) // SUBLANES
    n_main = CONV_SEG // CONV_SEG_STEP

    def taps(c, wts, base, seg, n_g, i0):
        accs = [None] * n_g
        for m in range(n_g + CONV_K - 1):
            v = win_ref[c, pl.ds(base + first + i0 + m, SUBLANES, stride=seg), :]
            for g in range(max(0, m - CONV_K + 1), min(n_g, m + 1)):
                term = v * wts[m - g]
                accs[g] = term if accs[g] is None else accs[g] + term
        for g in range(n_g):
            o_ref[c, pl.ds(base + i0 + g, SUBLANES, stride=seg), :] = accs[g]

    for c in range(N_SLAB):
        wts = [wb_ref[tap, c] for tap in range(CONV_K)]

        def body(ci, carry, c=c, wts=wts):
            taps(c, wts, 0, CONV_SEG, CONV_SEG_STEP, ci * CONV_SEG_STEP)
            return carry

        lax.fori_loop(0, n_main, body, 0)
        taps(c, wts, rem_base, rem_seg, rem_seg, 0)


def _conv_prompt(u_slabs, u_meta, wb, n_batch, seq):
    assert CONV_SEG % CONV_SEG_STEP == 0 and seq > SUBLANES * CONV_SEG
    assert (seq - SUBLANES * CONV_SEG) % SUBLANES == 0
    blk = pl.BlockSpec((N_SLAB, seq, LANES), lambda b: (0, b, 0))
    return pl.pallas_call(
        functools.partial(_conv_prompt_kernel, seq),
        grid=(n_batch,),
        in_specs=[
            blk,
            _const_spec((N_SLAB, N_META, LANES)),
            _const_spec((CONV_K, N_SLAB, SUBLANES, LANES)),
        ],
        out_specs=blk,
        out_shape=jax.ShapeDtypeStruct((N_SLAB, n_batch * seq, LANES), F32),
        scratch_shapes=[pltpu.VMEM((N_SLAB, CONV_HALO + seq, LANES), F32)],
        compiler_params=_params("parallel"),
        name="conv_prompt",
    )(u_slabs, u_meta, wb)


def _conv_sample_kernel(t_new, st_ref, u_ref, wb_ref, o_ref, so_ref):
    hist = st_ref.shape[0]
    bb = st_ref.shape[1]
    so_ref[0:hist - t_new] = st_ref[t_new:hist]

    for c in range(N_SLAB):
        lanes = slice(c * LANES, (c + 1) * LANES)
        for t in range(t_new):
            so_ref[hist - t_new + t, :, lanes] = u_ref[c, pl.ds(t, bb, stride=t_new), :]

        def src(k, rc, c=c, lanes=lanes):
            if k < hist:
                return st_ref[k, pl.ds(pl.multiple_of(rc * SUBLANES, SUBLANES), SUBLANES), lanes]
            return u_ref[c, pl.ds(rc * SUBLANES * t_new + (k - hist), SUBLANES, stride=t_new), :]

        wts = [wb_ref[tap, c] for tap in range(CONV_K)]

        def per_rows(rc, carry, c=c, src=src, wts=wts):
            accs = [None] * t_new
            for k in range(hist + t_new):
                v = src(k, rc)
                for t in range(max(0, k - CONV_K + 1), min(t_new, k + 1)):
                    term = v * wts[k - t]
                    accs[t] = term if accs[t] is None else accs[t] + term
            for t in range(t_new):
                o_ref[c, pl.ds(rc * SUBLANES * t_new + t, SUBLANES, stride=t_new), :] = accs[t]
            return carry

        for rc in range(bb // SUBLANES):
            per_rows(rc, 0)


def _conv_sample(state_t, u_slabs, wb, t_new, bb):
    hist, nb, _ = state_t.shape
    rows = bb * t_new
    slab_spec = pl.BlockSpec((N_SLAB, rows, LANES), lambda s: (0, s, 0))
    return pl.pallas_call(
        functools.partial(_conv_sample_kernel, t_new),
        grid=(nb // bb,),
        in_specs=[
            pl.BlockSpec((hist, bb, CONV_CH), lambda s: (0, s, 0)),
            slab_spec,
            _const_spec((CONV_K, N_SLAB, SUBLANES, LANES)),
        ],
        out_specs=[slab_spec, pl.BlockSpec((hist, bb, CONV_CH), lambda s: (0, s, 0))],
        out_shape=[
            jax.ShapeDtypeStruct((N_SLAB, nb * t_new, LANES), F32),
            jax.ShapeDtypeStruct(state_t.shape, F32),
        ],
        compiler_params=_params("parallel"),
        name="conv_sample",
    )(state_t, u_slabs, wb)


def _outproj_kernel(x_ref, a_ref, c_ref, bdw_ref, g_ref, b_ref, wo_ref, h_ref, cbf_ref):
    h_ref[...] = x_ref[...] + jnp.dot(a_ref[...], wo_ref[0:ATTN_WIDTH].astype(BF16), preferred_element_type=F32)
    y = jnp.concatenate([c_ref[c] for c in range(N_SLAB)], axis=1)
    cbf_ref[...] = _ln_swish(y + bdw_ref[...], g_ref[...], b_ref[...]).astype(BF16)
    h_ref[...] += jnp.dot(cbf_ref[...], wo_ref[ATTN_WIDTH:].astype(BF16), preferred_element_type=F32)


def _outproj(x, attn, conv_taps, b_dw, ln_g, ln_b, w_out, tm):
    m = x.shape[0]
    return pl.pallas_call(
        _outproj_kernel,
        grid=(m // tm,),
        in_specs=[
            pl.BlockSpec((tm, D_MODEL), lambda i: (i, 0)),
            pl.BlockSpec((tm, ATTN_WIDTH), lambda i: (i, 0)),
            pl.BlockSpec((N_SLAB, tm, LANES), lambda i: (0, i, 0)),
            _const_spec((1, CONV_CH)),
            _const_spec((1, CONV_CH)),
            _const_spec((1, CONV_CH)),
            _const_spec((D_MODEL, D_MODEL)),
        ],
        out_specs=pl.BlockSpec((tm, D_MODEL), lambda i: (i, 0)),
        out_shape=jax.ShapeDtypeStruct((m, D_MODEL), F32),
        scratch_shapes=[pltpu.VMEM((tm, CONV_CH), BF16)],
        compiler_params=_params("parallel"),
        name="outproj",
    )(x, attn, conv_taps, b_dw, ln_g, ln_b, w_out)


def _mlp_kernel(h_ref, g2_ref, wu_ref, wd_ref, gf_ref, y_ref, xn_ref):
    f = pl.program_id(1)

    @pl.when(f == 0)
    def _():
        h1 = h_ref[...]
        y_ref[...] = h1
        xn_ref[...] = _rms(h1, g2_ref[...]).astype(BF16)

    up = jnp.dot(xn_ref[...], wu_ref[...].astype(BF16), preferred_element_type=F32)
    act = jnp.maximum(up, 0.0)
    act = (act * act).astype(BF16)
    y_ref[...] += jnp.dot(act, wd_ref[...].astype(BF16), preferred_element_type=F32)

    @pl.when(f == pl.num_programs(1) - 1)
    def _():
        y_ref[...] = _rms(y_ref[...], gf_ref[...])


def _mlp(h1, g2, wu_bf, wd_bf, gf, tm, tf):
    m = h1.shape[0]
    return pl.pallas_call(
        _mlp_kernel,
        grid=(m // tm, D_FF // tf),
        in_specs=[
            pl.BlockSpec((tm, D_MODEL), lambda i, f: (i, 0)),
            _const_spec((1, D_MODEL)),
            pl.BlockSpec((D_MODEL, tf), lambda i, f: (0, f)),
            pl.BlockSpec((tf, D_MODEL), lambda i, f: (f, 0)),
            _const_spec((1, D_MODEL)),
        ],
        out_specs=pl.BlockSpec((tm, D_MODEL), lambda i, f: (i, 0)),
        out_shape=jax.ShapeDtypeStruct((m, D_MODEL), F32),
        scratch_shapes=[pltpu.VMEM((tm, D_MODEL), BF16)],
        compiler_params=pltpu.CompilerParams(dimension_semantics=("parallel", "arbitrary"),
                                             vmem_limit_bytes=VMEM_LIMIT_MLP),
        name="mlp",
    )(h1, g2, wu_bf, wd_bf, gf)


def kernel(x_prompt, x_sample, cache_k, cache_v, state_conv, meta_tokens, norm_mix, w_in, attn_sinks,
           w_dw, b_dw, conv_ln_g, conv_ln_b, w_out, norm_mlp, w_up, w_down, norm_final):
    n_batch, seq, _ = x_prompt.shape
    n_dec, t_new, _ = x_sample.shape
    depth = w_in.shape[0]
    window = cache_k.shape[2]
    assert depth == 1 and seq % BLOCK == 0 and t_new == SUBLANES and window == WINDOW
    l = 0
    n_blk = seq // BLOCK
    tm = 512
    tm_mlp = 1024
    tf = 512
    bt = LANES // t_new
    conv_bb = 32
    assert (n_batch * seq) % tm_mlp == 0 and (n_dec * t_new) % tm_mlp == 0 and n_dec % conv_bb == 0

    g_mix = norm_mix[l][None, :]
    g_mlp = norm_mlp[l][None, :]
    g_fin = norm_final[None, :]
    sinks = attn_sinks[l].astype(F32) * LOG2_E
    bdw = b_dw[l][None, :]
    ln_g = conv_ln_g[l][None, :]
    ln_b = conv_ln_b[l][None, :]
    wb_slabs = jnp.broadcast_to(w_dw[l].reshape(CONV_K, N_SLAB, 1, LANES), (CONV_K, N_SLAB, SUBLANES, LANES))

    rope_p = _rope_table(N_META + jnp.arange(seq, dtype=jnp.int32))
    rope_s = _rope_table(PAST_LEN + (jnp.arange(tm, dtype=jnp.int32) % t_new))
    ropet_s = jnp.tile(rope_s[:, :, :HEAD_DIM].transpose(0, 2, 1), (1, N_KV_HEADS, 1))
    rope_m = _rope_table(jnp.arange(N_META, dtype=jnp.int32))

    xp = x_prompt.reshape(n_batch * seq, D_MODEL)
    xs = x_sample.reshape(n_dec * t_new, D_MODEL)

    q_p, k_p, v_p, u_p = _inproj_prompt(xp, g_mix, w_in[l], rope_p, tm)
    q_s, kt_s, vt_s, u_s, k_m, v_m, u_m = _inproj_sample(xs, g_mix, w_in[l], rope_s, ropet_s,
                                                         meta_tokens.astype(F32), rope_m, tm)

    attn_p = _attn_prompt(sinks, q_p, k_p, v_p, k_m, v_m, n_batch, n_blk)
    conv_p = _conv_prompt(u_p, u_m, wb_slabs, n_batch, seq)

    ckt = cache_k[l].transpose(0, 2, 3, 1).reshape(n_dec, KV_WIDTH, window)
    cvt = cache_v[l].transpose(0, 2, 3, 1).reshape(n_dec, KV_WIDTH, window)
    sink_rows = jnp.broadcast_to(jnp.repeat(sinks, t_new)[:, None], (N_HEADS * t_new, LANES))
    attn_s, kt_out, vt_out = _attn_sample(sink_rows, q_s, kt_s, vt_s, ckt, cvt, t_new, bt)
    state_t = state_conv[l].transpose(1, 0, 2)
    conv_s, state_out_t = _conv_sample(state_t, u_s, wb_slabs, t_new, conv_bb)

    h_p = _outproj(xp, attn_p, conv_p, bdw, ln_g, ln_b, w_out[l], tm)
    h_s = _outproj(xs, attn_s, conv_s, bdw, ln_g, ln_b, w_out[l], tm)
    y_p = _mlp(h_p, g_mlp, w_up[l], w_down[l], g_fin, tm_mlp, tf)
    y_s = _mlp(h_s, g_mlp, w_up[l], w_down[l], g_fin, tm_mlp, tf)

    y_prompt = y_p.reshape(n_batch, seq, D_MODEL)
    y_sample = y_s.reshape(n_dec, t_new, D_MODEL)
    k_last = k_p.reshape(n_batch, seq, KV_WIDTH)[:, seq - WINDOW:]
    v_last = v_p.reshape(n_batch, seq, KV_WIDTH)[:, seq - WINDOW:]
    new_k_prompt = k_last.reshape(n_batch, WINDOW, N_KV_HEADS, HEAD_DIM)[None]
    new_v_prompt = v_last.reshape(n_batch, WINDOW, N_KV_HEADS, HEAD_DIM)[None]
    u_last = u_p.reshape(N_SLAB, n_batch, seq, LANES)[:, :, seq - (CONV_K - 1):]
    new_conv_prompt = u_last.transpose(1, 2, 0, 3).reshape(n_batch, CONV_K - 1, CONV_CH)[None]
    new_k_sample = kt_out.reshape(n_dec, N_KV_HEADS, HEAD_DIM, window).transpose(0, 3, 1, 2)[None]
    new_v_sample = vt_out.reshape(n_dec, N_KV_HEADS, HEAD_DIM, window).transpose(0, 3, 1, 2)[None]
    new_conv_sample = state_out_t.transpose(1, 0, 2)[None]
    return (y_prompt, y_sample, new_k_prompt, new_v_prompt, new_conv_prompt,
            new_k_sample, new_v_sample, new_conv_sample)
```
